```python
import math
import jax
import jax.numpy as jnp
from jax import lax
import numpy as np

D_MODEL = 1024
BATCH = 16
SEQ = 2048
DEPTH = 1

MEM_LEN = 256
NORM_EPS = 1e-5
QK_EPS = 1e-6

SSM_WIDTH = D_MODEL // 2
SSM_GROUP = 16
SSM_GROUPS = SSM_WIDTH // SSM_GROUP
SSM_STATE = 64
SSM_DT_MIN = 0.001
SSM_DT_MAX = 0.1

HEAD_DIM = 64
SWA_Q_HEADS = D_MODEL // HEAD_DIM
SWA_KV_HEADS = 4
SWA_GROUP = SWA_Q_HEADS // SWA_KV_HEADS
WINDOW = 128
BLOCK = 128
ROPE_THETA = 500000.0
ROPE_DIM = HEAD_DIM // 4

MEM_HEADS = 4
MEM_HEAD_DIM = 128

SWA_Q_WIDTH = SWA_Q_HEADS * HEAD_DIM
SWA_KV_WIDTH = SWA_KV_HEADS * HEAD_DIM
MEM_WIDTH = MEM_HEADS * MEM_HEAD_DIM
N_BRANCH = 3
IN_WIDTHS = (SSM_WIDTH, SWA_Q_WIDTH, SWA_KV_WIDTH, SWA_KV_WIDTH, MEM_WIDTH, N_BRANCH * D_MODEL)
IN_WIDTH = sum(IN_WIDTHS)

N_EXPERTS = 32
TOP_K = 4
D_EXPERT = D_MODEL
SWIGLU_ALPHA = 1.702
SWIGLU_LIMIT = 7.0

kernel_name = 'hybrid_gated_s5_swa_mem_moe'


def rmsnorm(x, g, eps):
    xf = x.astype(jnp.float32)
    y = xf * lax.rsqrt(jnp.mean(xf * xf, axis=-1, keepdims=True) + eps)
    return (y * g.astype(jnp.float32)).astype(x.dtype)


def partial_rope(t, positions):
    half = ROPE_DIM // 2
    inv_freq = ROPE_THETA ** (-jnp.arange(half, dtype=jnp.float32) / half)
    ang = positions.astype(jnp.float32)[..., None] * inv_freq
    cos = jnp.cos(ang)[:, :, None, :]
    sin = jnp.sin(ang)[:, :, None, :]
    tr = t[..., :ROPE_DIM].astype(jnp.float32)
    t1, t2 = tr[..., :half], tr[..., half:]
    rot = jnp.concatenate([t1 * cos - t2 * sin, t2 * cos + t1 * sin], axis=-1).astype(t.dtype)
    return jnp.concatenate([rot, t[..., ROPE_DIM:]], axis=-1)


def s5_branch(u, log_dt, a_re, a_im, b_re, b_im, c_re, c_im, d_skip, w_glu_v, w_glu_g):
    f32 = jnp.float32
    bsz, seq, _ = u.shape
    dt = jnp.exp(log_dt.astype(f32))[:, None]
    lam = lax.complex(jnp.minimum(a_re.astype(f32), -1e-4), a_im.astype(f32))
    a_bar = jnp.exp(lam * dt)
    b = lax.complex(b_re.astype(f32), b_im.astype(f32))
    b_bar = ((a_bar - 1.0) / lam)[..., None] * b
    c = lax.complex(c_re.astype(f32), c_im.astype(f32))
    uf = u.astype(f32)
    ug = uf.reshape(bsz, seq, SSM_GROUPS, SSM_GROUP).astype(jnp.complex64)
    bu = jnp.einsum('bsgc,gpc->bsgp', ug, b_bar)
    a_elems = jnp.broadcast_to(a_bar[None, None], (1, seq, SSM_GROUPS, SSM_STATE))

    def combine(left, right):
        a_l, b_l = left
        a_r, b_r = right
        return a_r * a_l, a_r * b_l + b_r

    _, states = lax.associative_scan(combine, (a_elems, bu), axis=1)
    y = jnp.einsum('bsgp,gcp->bsgc', states, c).real.reshape(bsz, seq, SSM_WIDTH)
    y = y + d_skip.astype(f32) * uf
    y = jax.nn.gelu(y).astype(u.dtype)
    return (y @ w_glu_v) * jax.nn.sigmoid(y @ w_glu_g)


def swa_branch(q, k, v, positions, q_gain, k_gain, sinks):
    f32 = jnp.float32
    bsz, seq = q.shape[0], q.shape[1]
    nb = seq // BLOCK
    q = partial_rope(rmsnorm(q, q_gain, QK_EPS), positions)
    k = partial_rope(rmsnorm(k, k_gain, QK_EPS), positions)
    qb = q.reshape(bsz, nb, BLOCK, SWA_KV_HEADS, SWA_GROUP, HEAD_DIM)

    def with_prev(t):
        tb = t.reshape(bsz, nb, BLOCK, SWA_KV_HEADS, HEAD_DIM)
        prev = jnp.pad(tb[:, :-1], ((0, 0), (1, 0), (0, 0), (0, 0), (0, 0)))
        return jnp.concatenate([prev, tb], axis=2)

    kb, vb = with_prev(k), with_prev(v)
    scores = jnp.einsum('bnqhgd,bnkhd->bnhgqk', qb, kb, preferred_element_type=f32) * (HEAD_DIM ** -0.5)
    qi = jnp.arange(BLOCK)[:, None]
    ki = jnp.arange(2 * BLOCK)[None, :] - BLOCK
    rel = qi - ki
    band = (rel >= 0) & (rel < WINDOW)
    valid = band[None] & ((jnp.arange(nb)[:, None, None] > 0) | (ki[None] >= 0))
    scores = jnp.where(valid[None, :, None, None], scores, -jnp.inf)
    sink = sinks.astype(f32).reshape(SWA_KV_HEADS, SWA_GROUP)[None, None, :, :, None, None]
    m = jnp.maximum(jnp.max(scores, axis=-1, keepdims=True), sink)
    p = jnp.exp(scores - m)
    p = p / (jnp.sum(p, axis=-1, keepdims=True) + jnp.exp(sink - m))
    out = jnp.einsum('bnhgqk,bnkhd->bnqhgd', p.astype(v.dtype), vb)
    return out.reshape(bsz, seq, SWA_Q_WIDTH)


def mem_branch(q, mk, mv, q_gain, k_gain):
    bsz, seq = q.shape[0], q.shape[1]
    q = rmsnorm(q, q_gain, QK_EPS)
    mk = rmsnorm(mk, k_gain, QK_EPS)
    s = jnp.einsum('bshd,bmhd->bhsm', q, mk, preferred_element_type=jnp.float32) * (MEM_HEAD_DIM ** -0.5)
    p = jax.nn.softmax(s, axis=-1)
    out = jnp.einsum('bhsm,bmhd->bshd', p.astype(mv.dtype), mv)
    return out.reshape(bsz, seq, MEM_WIDTH)


def moe_ffn(h, w_router, b_router, w1, b1, w2, b2):
    f32 = jnp.float32
    bsz, seq, d = h.shape
    t = h.reshape(-1, d)
    logits = (t @ w_router + b_router).astype(f32)
    top_vals, top_idx = lax.top_k(logits, TOP_K)
    top_w = jax.nn.softmax(top_vals, axis=-1)
    comb = jnp.sum(jax.nn.one_hot(top_idx, N_EXPERTS, dtype=f32) * top_w[..., None], axis=1)
    out = jnp.zeros((t.shape[0], d), f32)
    for e in range(N_EXPERTS):
        hid = t @ w1[e] + b1[e]
        x_glu = jnp.minimum(hid[:, ::2], SWIGLU_LIMIT)
        x_lin = jnp.clip(hid[:, 1::2], -SWIGLU_LIMIT, SWIGLU_LIMIT)
        act = x_glu * jax.nn.sigmoid(SWIGLU_ALPHA * x_glu) * (x_lin + 1.0)
        y = act @ w2[e] + b2[e]
        out = out + comb[:, e:e + 1] * y.astype(f32)
    return out.astype(h.dtype).reshape(bsz, seq, d)


def setup_inputs(seed: int = 0) -> dict:
    key = jax.random.key(seed)
    ks = jax.random.split(key, 40)
    f32 = jnp.float32
    L = DEPTH

    def nrm(k, shape, scale):
        return jax.random.normal(k, shape, f32) * scale

    def gain(k, n):
        return 1.0 + 0.02 * jax.random.normal(k, (L, n), f32)

    x = nrm(ks[0], (BATCH, SEQ, D_MODEL), 1.0)
    mem = nrm(ks[1], (BATCH, MEM_LEN, D_MODEL), 1.0)
    offset = jax.random.randint(ks[2], (BATCH, 1), 0, 4096, dtype=jnp.int32)
    positions = offset + jnp.arange(SEQ, dtype=jnp.int32)[None, :]
    n_idx = jnp.arange(SSM_STATE, dtype=f32)
    return {
        'x': x,
        'mem': mem,
        'positions': positions,
        'attn_norm_g': gain(ks[3], D_MODEL),
        'mem_norm_g': gain(ks[4], D_MODEL),
        'w_in': nrm(ks[5], (L, D_MODEL, IN_WIDTH), D_MODEL ** -0.5),
        'b_gate': nrm(ks[6], (L, N_BRANCH * D_MODEL), 0.02),
        'ssm_log_dt': jax.random.uniform(ks[7], (L, SSM_GROUPS), f32, math.log(SSM_DT_MIN), math.log(SSM_DT_MAX)),
        'ssm_a_re': -0.5 + nrm(ks[8], (L, SSM_GROUPS, SSM_STATE), 0.01),
        'ssm_a_im': jnp.pi * n_idx + nrm(ks[9], (L, SSM_GROUPS, SSM_STATE), 0.01),
        'ssm_b_re': nrm(ks[10], (L, SSM_GROUPS, SSM_STATE, SSM_GROUP), (2.0 * SSM_GROUP) ** -0.5),
        'ssm_b_im': nrm(ks[11], (L, SSM_GROUPS, SSM_STATE, SSM_GROUP), (2.0 * SSM_GROUP) ** -0.5),
        'ssm_c_re': nrm(ks[12], (L, SSM_GROUPS, SSM_GROUP, SSM_STATE), (2.0 * SSM_STATE) ** -0.5),
        'ssm_c_im': nrm(ks[13], (L, SSM_GROUPS, SSM_GROUP, SSM_STATE), (2.0 * SSM_STATE) ** -0.5),
        'ssm_d': nrm(ks[14], (L, SSM_WIDTH), 1.0),
        'w_glu_v': nrm(ks[15], (L, SSM_WIDTH, SSM_WIDTH), SSM_WIDTH ** -0.5),
        'w_glu_g': nrm(ks[16], (L, SSM_WIDTH, SSM_WIDTH), SSM_WIDTH ** -0.5),
        'swa_q_gain': gain(ks[17], HEAD_DIM),
        'swa_k_gain': gain(ks[18], HEAD_DIM),
        'swa_sinks': nrm(ks[19], (L, SWA_Q_HEADS), 0.5),
        'w_mem_kv': nrm(ks[20], (L, D_MODEL, 2 * MEM_WIDTH), D_MODEL ** -0.5),
        'mem_q_gain': gain(ks[21], MEM_HEAD_DIM),
        'mem_k_gain': gain(ks[22], MEM_HEAD_DIM),
        'w_br_ssm': nrm(ks[23], (L, SSM_WIDTH, D_MODEL), SSM_WIDTH ** -0.5),
        'w_br_swa': nrm(ks[24], (L, SWA_Q_WIDTH, D_MODEL), SWA_Q_WIDTH ** -0.5),
        'w_br_mem': nrm(ks[25], (L, MEM_WIDTH, D_MODEL), MEM_WIDTH ** -0.5),
        'w_out': nrm(ks[26], (L, D_MODEL, D_MODEL), D_MODEL ** -0.5),
        'ffn_norm_g': gain(ks[27], D_MODEL),
        'w_router': nrm(ks[28], (L, D_MODEL, N_EXPERTS), D_MODEL ** -0.5),
        'b_router': nrm(ks[29], (L, N_EXPERTS), 0.01),
        'w_mlp1': nrm(ks[30], (L, N_EXPERTS, D_MODEL, 2 * D_EXPERT), D_MODEL ** -0.5),
        'b_mlp1': nrm(ks[31], (L, N_EXPERTS, 2 * D_EXPERT), 0.02),
        'w_mlp2': nrm(ks[32], (L, N_EXPERTS, D_EXPERT, D_MODEL), D_EXPERT ** -0.5),
        'b_mlp2': nrm(ks[33], (L, N_EXPERTS, D_MODEL), 0.02),
    }


def reference(x, mem, positions, attn_norm_g, mem_norm_g, w_in, b_gate,
              ssm_log_dt, ssm_a_re, ssm_a_im, ssm_b_re, ssm_b_im, ssm_c_re, ssm_c_im, ssm_d,
              w_glu_v, w_glu_g, swa_q_gain, swa_k_gain, swa_sinks,
              w_mem_kv, mem_q_gain, mem_k_gain,
              w_br_ssm, w_br_swa, w_br_mem, w_out,
              ffn_norm_g, w_router, b_router, w_mlp1, b_mlp1, w_mlp2, b_mlp2):
    bsz, seq, _ = x.shape
    offs = np.cumsum(IN_WIDTHS)[:-1].tolist()
    for l in range(DEPTH):
        h = rmsnorm(x, attn_norm_g[l], NORM_EPS)
        proj = h @ w_in[l]
        u, q_swa, k_swa, v_swa, q_mem, gate_logits = jnp.split(proj, offs, axis=-1)
        y_ssm = s5_branch(u, ssm_log_dt[l], ssm_a_re[l], ssm_a_im[l], ssm_b_re[l], ssm_b_im[l],
                          ssm_c_re[l], ssm_c_im[l], ssm_d[l], w_glu_v[l], w_glu_g[l])
        y_swa = swa_branch(q_swa.reshape(bsz, seq, SWA_Q_HEADS, HEAD_DIM),
                           k_swa.reshape(bsz, seq, SWA_KV_HEADS, HEAD_DIM),
                           v_swa.reshape(bsz, seq, SWA_KV_HEADS, HEAD_DIM),
                           positions, swa_q_gain[l], swa_k_gain[l], swa_sinks[l])
        mem_h = rmsnorm(mem, mem_norm_g[l], NORM_EPS)
        mk, mv = jnp.split(mem_h @ w_mem_kv[l], 2, axis=-1)
        y_mem = mem_branch(q_mem.reshape(bsz, seq, MEM_HEADS, MEM_HEAD_DIM),
                           mk.reshape(bsz, -1, MEM_HEADS, MEM_HEAD_DIM),
                           mv.reshape(bsz, -1, MEM_HEADS, MEM_HEAD_DIM),
                           mem_q_gain[l], mem_k_gain[l])
        gates = jax.nn.sigmoid((gate_logits + b_gate[l]).astype(jnp.float32)).astype(x.dtype)
        gates = gates.reshape(bsz, seq, N_BRANCH, D_MODEL)
        merged = (gates[:, :, 0] * (y_ssm @ w_br_ssm[l])
                  + gates[:, :, 1] * (y_swa @ w_br_swa[l])
                  + gates[:, :, 2] * (y_mem @ w_br_mem[l]))
        x = x + merged @ w_out[l]
        x = x + moe_ffn(rmsnorm(x, ffn_norm_g[l], NORM_EPS), w_router[l], b_router[l],
                        w_mlp1[l], b_mlp1[l], w_mlp2[l], b_mlp2[l])
    return x
```

```python
import functools
import math

import jax
import jax.numpy as jnp
from jax import lax
from jax.experimental import pallas as pl
from jax.experimental.pallas import tpu as pltpu

F32 = jnp.float32
BF16 = jnp.bfloat16

D_MODEL = 1024
MEM_LEN = 256
NORM_EPS = 1e-5
QK_EPS = 1e-6

SSM_WIDTH = 512
SSM_GROUP = 16
SSM_GROUPS = 32
SSM_STATE = 64
SSM_COMPLEX = SSM_GROUPS * SSM_STATE

HEAD_DIM = 64
SWA_Q_HEADS = 16
SWA_KV_HEADS = 4
SWA_GROUP = 4
BLOCK = 128
ROPE_THETA = 500000.0
ROPE_DIM = 16
ROPE_HALF = 8

MEM_HEADS = 4
MEM_HEAD_DIM = 128

SWA_Q_WIDTH = 1024
SWA_KV_WIDTH = 256
MEM_WIDTH = 512
N_BRANCH = 3
GATE_WIDTH = N_BRANCH * D_MODEL
OFF_U, OFF_Q, OFF_K, OFF_V, OFF_QM, OFF_G, OFF_END = 0, 512, 1536, 1792, 2048, 2560, 5632

N_EXPERTS = 32
TOP_K = 4
D_EXPERT = 1024
SWIGLU_ALPHA = 1.702
SWIGLU_LIMIT = 7.0

LANES = 128
VMEM_LIMIT = 56 * 1024 * 1024
NEG_BIG = -1e30


def _sigmoid(x):
    return 1.0 / (1.0 + jnp.exp(-x))


def _cparams(sem):
    return pltpu.CompilerParams(dimension_semantics=sem, vmem_limit_bytes=VMEM_LIMIT)


def _const_spec(shape):
    nd = len(shape)
    return pl.BlockSpec(shape, lambda *_: (0,) * nd)


def _inproj_kernel(x_ref, g_ref, w_ref, bg_ref, u_ref, q_ref, k_ref, v_ref, qm_ref, gate_ref):
    x = x_ref[...]
    h = x * lax.rsqrt(jnp.mean(x * x, axis=-1, keepdims=True) + NORM_EPS) * g_ref[...]
    hb = h.astype(BF16)

    def proj(lo, hi):
        return jnp.dot(hb, w_ref[:, lo:hi], preferred_element_type=F32)

    u_ref[...] = proj(OFF_U, OFF_Q).astype(BF16)
    q_ref[...] = proj(OFF_Q, OFF_K).astype(BF16)
    k_ref[...] = proj(OFF_K, OFF_V).astype(BF16)
    v_ref[...] = proj(OFF_V, OFF_QM).astype(BF16)
    qm_ref[...] = proj(OFF_QM, OFF_G).astype(BF16)
    gate_ref[...] = _sigmoid(proj(OFF_G, OFF_END) + bg_ref[...]).astype(BF16)


def _in_projection(x2, g, w_in_b, b_gate, bsz, seq, tm):
    t = bsz * seq
    nj = seq // tm
    row = lambda b, j: (b * nj + j, 0)
    return pl.pallas_call(
        _inproj_kernel,
        grid=(bsz, nj),
        in_specs=[
            pl.BlockSpec((tm, D_MODEL), row),
            _const_spec((1, D_MODEL)),
            _const_spec((D_MODEL, OFF_END)),
            _const_spec((1, GATE_WIDTH)),
        ],
        out_specs=[
            pl.BlockSpec((tm, SSM_WIDTH), lambda b, j: (j, b)),
            pl.BlockSpec((tm, SWA_Q_WIDTH), row),
            pl.BlockSpec((tm, SWA_KV_WIDTH), row),
            pl.BlockSpec((tm, SWA_KV_WIDTH), row),
            pl.BlockSpec((tm, MEM_WIDTH), row),
            pl.BlockSpec((tm, GATE_WIDTH), row),
        ],
        out_shape=[
            jax.ShapeDtypeStruct((seq, bsz * SSM_WIDTH), BF16),
            jax.ShapeDtypeStruct((t, SWA_Q_WIDTH), BF16),
            jax.ShapeDtypeStruct((t, SWA_KV_WIDTH), BF16),
            jax.ShapeDtypeStruct((t, SWA_KV_WIDTH), BF16),
            jax.ShapeDtypeStruct((t, MEM_WIDTH), BF16),
            jax.ShapeDtypeStruct((t, GATE_WIDTH), BF16),
        ],
        compiler_params=_cparams(("parallel", "parallel")),
    )(x2, g, w_in_b, b_gate)


def _gelu_tanh(x):
    c = math.sqrt(2.0 / math.pi)
    return 0.5 * x * (1.0 + jnp.tanh(c * (x + 0.044715 * (x * x * x))))


def _s5_kernel(u_ref, bmat_ref, are_ref, aim_ref, cmat_ref, d_ref, wv_ref, wg_ref, o_ref,
               st_ref, bu_ref, *, bsz, steps, col_chunk):
    nc = SSM_COMPLEX

    @pl.when(pl.program_id(0) == 0)
    def _():
        st_ref[...] = jnp.zeros_like(st_ref)

    u = u_ref[...]
    for lo in range(0, 2 * nc, 1024):
        bu_ref[:, lo:lo + 1024] = jnp.dot(u, bmat_ref[:, lo:lo + 1024], preferred_element_type=F32)

    for lo in range(0, nc, col_chunk):
        re = slice(lo, lo + col_chunk)
        im = slice(nc + lo, nc + lo + col_chunk)
        ar = jnp.broadcast_to(are_ref[:, re], (bsz, col_chunk))
        ai = jnp.broadcast_to(aim_ref[:, re], (bsz, col_chunk))

        def step(t, carry):
            sr, si = carry
            rows = pl.ds(pl.multiple_of(t * bsz, bsz), bsz)
            nr = ar * sr - ai * si + bu_ref[rows, re]
            ni = ar * si + ai * sr + bu_ref[rows, im]
            bu_ref[rows, re] = nr
            bu_ref[rows, im] = ni
            return nr, ni

        sr, si = lax.fori_loop(0, steps, step, (st_ref[:, re], st_ref[:, im]))
        st_ref[:, re] = sr
        st_ref[:, im] = si

    y = jnp.dot(bu_ref[...].astype(BF16), cmat_ref[...], preferred_element_type=F32)
    y = y + d_ref[...] * u.astype(F32)
    yb = _gelu_tanh(y).astype(BF16)
    val = jnp.dot(yb, wv_ref[...], preferred_element_type=F32)
    gate = jnp.dot(yb, wg_ref[...], preferred_element_type=F32)
    o_ref[...] = (val * _sigmoid(gate)).astype(BF16)


def _s5_branch(u_tm, bmat, a_re, a_im, cmat, d_skip, wv, wg, bsz, seq, steps):
    rows = steps * bsz
    kern = functools.partial(_s5_kernel, bsz=bsz, steps=steps, col_chunk=512)
    return pl.pallas_call(
        kern,
        grid=(seq // steps,),
        in_specs=[
            pl.BlockSpec((rows, SSM_WIDTH), lambda i: (i, 0)),
            _const_spec((SSM_WIDTH, 2 * SSM_COMPLEX)),
            _const_spec((1, SSM_COMPLEX)),
            _const_spec((1, SSM_COMPLEX)),
            _const_spec((2 * SSM_COMPLEX, SSM_WIDTH)),
            _const_spec((1, SSM_WIDTH)),
            _const_spec((SSM_WIDTH, SSM_WIDTH)),
            _const_spec((SSM_WIDTH, SSM_WIDTH)),
        ],
        out_specs=pl.BlockSpec((rows, SSM_WIDTH), lambda i: (i, 0)),
        out_shape=jax.ShapeDtypeStruct((seq * bsz, SSM_WIDTH), BF16),
        scratch_shapes=[
            pltpu.VMEM((bsz, 2 * SSM_COMPLEX), F32),
            pltpu.VMEM((rows, 2 * SSM_COMPLEX), F32),
        ],
        compiler_params=_cparams(("arbitrary",)),
    )(u_tm, bmat, a_re, a_im, cmat, d_skip, wv, wg)


def _s5_discretize(log_dt, a_re, a_im, b_re, b_im, c_re, c_im):
    dt = jnp.exp(log_dt.astype(F32))[:, None]
    lr = jnp.minimum(a_re.astype(F32), -1e-4)
    li = a_im.astype(F32)
    mag = jnp.exp(lr * dt)
    abr = mag * jnp.cos(li * dt)
    abi = mag * jnp.sin(li * dt)
    nr, ni = abr - 1.0, abi
    den = lr * lr + li * li
    fr = (nr * lr + ni * li) / den
    fi = (ni * lr - nr * li) / den
    br, bi = b_re.astype(F32), b_im.astype(F32)
    bbr = fr[..., None] * br - fi[..., None] * bi
    bbi = fr[..., None] * bi + fi[..., None] * br
    eye = jnp.eye(SSM_GROUPS, dtype=F32)

    def blockdiag_in(m):
        return jnp.einsum('gpc,gh->gchp', m, eye).reshape(SSM_WIDTH, SSM_COMPLEX)

    def blockdiag_out(m):
        return jnp.einsum('gcp,gh->gphc', m, eye).reshape(SSM_COMPLEX, SSM_WIDTH)

    bmat = jnp.concatenate([blockdiag_in(bbr), blockdiag_in(bbi)], axis=1).astype(BF16)
    cmat = jnp.concatenate([blockdiag_out(c_re.astype(F32)), -blockdiag_out(c_im.astype(F32))],
                           axis=0).astype(BF16)
    return bmat, abr.reshape(1, SSM_COMPLEX), abi.reshape(1, SSM_COMPLEX), cmat


def _memkv_kernel(mem_ref, g_ref, w_ref, kg_ref, mk_ref, mv_ref):
    x = mem_ref[...]
    h = x * lax.rsqrt(jnp.mean(x * x, axis=-1, keepdims=True) + NORM_EPS) * g_ref[...]
    kv = jnp.dot(h.astype(BF16), w_ref[...], preferred_element_type=F32)
    parts = []
    for hd in range(MEM_HEADS):
        kh = kv[:, hd * MEM_HEAD_DIM:(hd + 1) * MEM_HEAD_DIM]
        kh = kh * lax.rsqrt(jnp.mean(kh * kh, axis=-1, keepdims=True) + QK_EPS) * kg_ref[...]
        parts.append(kh)
    mk_ref[...] = jnp.concatenate(parts, axis=1).astype(BF16)
    mv_ref[...] = kv[:, MEM_WIDTH:].astype(BF16)


def _mem_kv(mem2, g, w_b, k_gain, bsz):
    return pl.pallas_call(
        _memkv_kernel,
        grid=(bsz,),
        in_specs=[
            pl.BlockSpec((MEM_LEN, D_MODEL), lambda b: (b, 0)),
            _const_spec((1, D_MODEL)),
            _const_spec((D_MODEL, 2 * MEM_WIDTH)),
            _const_spec((1, MEM_HEAD_DIM)),
        ],
        out_specs=[
            pl.BlockSpec((MEM_LEN, MEM_WIDTH), lambda b: (b, 0)),
            pl.BlockSpec((MEM_LEN, MEM_WIDTH), lambda b: (b, 0)),
        ],
        out_shape=[
            jax.ShapeDtypeStruct((bsz * MEM_LEN, MEM_WIDTH), BF16),
            jax.ShapeDtypeStruct((bsz * MEM_LEN, MEM_WIDTH), BF16),
        ],
        compiler_params=_cparams(("parallel",)),
    )(mem2, g, w_b, k_gain)


def _rope_tables(pos, invf):
    ang = pos * invf
    c = jnp.cos(ang)
    s = jnp.sin(ang)
    lane = lax.broadcasted_iota(jnp.int32, ang.shape, 1) & (HEAD_DIM - 1)
    s_lo = jnp.where(lane < ROPE_HALF, -s, 0.0)
    s_hi = jnp.where((lane >= ROPE_HALF) & (lane < ROPE_DIM), s, 0.0)
    return c, s_lo, s_hi


def _rope(x, tables):
    c, s_lo, s_hi = tables
    width = x.shape[1]
    rep = width // LANES
    wide = lambda a: jnp.concatenate([a] * rep, axis=1)
    return (x * wide(c) + pltpu.roll(x, width - ROPE_HALF, 1) * wide(s_lo)
            + pltpu.roll(x, ROPE_HALF, 1) * wide(s_hi))


def _head_rs(x, eps):
    return lax.rsqrt(jnp.mean(x * x, axis=-1, keepdims=True) + eps)


def _attn_kernel(sink_ref, q_ref, kc_ref, vc_ref, kp_ref, vp_ref, pos_ref, posp_ref, qm_ref,
                 mk_ref, mv_ref, gq_ref, gk_ref, invf_ref, gmq_ref, oswa_ref, omem_ref, *, tq):
    nblk = tq // BLOCK
    first_tile = pl.program_id(1) == 0
    invf = invf_ref[...]

    k_raw = jnp.concatenate([kp_ref[...], kc_ref[...]], axis=0).astype(F32)
    v_all = jnp.concatenate([vp_ref[...], vc_ref[...]], axis=0)
    pos_k = jnp.concatenate([posp_ref[...], pos_ref[...]], axis=0)
    k_rot = _rope(k_raw * gk_ref[...], _rope_tables(pos_k, invf))
    k_heads, v_heads = [], []
    for h in range(SWA_KV_HEADS):
        sl = slice(h * HEAD_DIM, (h + 1) * HEAD_DIM)
        k_heads.append((k_rot[:, sl] * _head_rs(k_raw[:, sl], QK_EPS)).astype(BF16))
        v_heads.append(v_all[:, sl])

    q_raw = q_ref[...].astype(F32)
    q_rot = _rope(q_raw * gq_ref[...], _rope_tables(pos_ref[...], invf))
    q_heads = []
    for hd in range(SWA_Q_HEADS):
        sl = slice(hd * HEAD_DIM, (hd + 1) * HEAD_DIM)
        scale = _head_rs(q_raw[:, sl], QK_EPS) * (HEAD_DIM ** -0.5)
        q_heads.append((q_rot[:, sl] * scale).astype(BF16))

    rows = SWA_GROUP * BLOCK
    qi = lax.broadcasted_iota(jnp.int32, (rows, 2 * BLOCK), 0) & (BLOCK - 1)
    kj = lax.broadcasted_iota(jnp.int32, (rows, 2 * BLOCK), 1)
    in_cur = kj >= BLOCK
    cur_ok = in_cur & ((kj - BLOCK) <= qi)
    band = cur_ok | (jnp.logical_not(in_cur) & (kj > qi))
    first_off = jnp.where(first_tile, BLOCK, 0)
    band_first = cur_ok | (jnp.logical_not(in_cur) & (kj > qi + first_off))

    for n in range(nblk):
        valid = band_first if n == 0 else band
        out_heads = [None] * SWA_Q_HEADS
        for h in range(SWA_KV_HEADS):
            qs = jnp.concatenate(
                [q_heads[h * SWA_GROUP + g][n * BLOCK:(n + 1) * BLOCK] for g in range(SWA_GROUP)], axis=0)
            kb = k_heads[h][n * BLOCK:(n + 2) * BLOCK]
            vb = v_heads[h][n * BLOCK:(n + 2) * BLOCK]
            s = lax.dot_general(qs, kb, (((1,), (1,)), ((), ())), preferred_element_type=F32)
            s = jnp.where(valid, s, NEG_BIG)
            sink = jnp.concatenate(
                [jnp.full((BLOCK, 1), sink_ref[h * SWA_GROUP + g], F32) for g in range(SWA_GROUP)], axis=0)
            m = jnp.maximum(jnp.max(s, axis=-1, keepdims=True), sink)
            p = jnp.exp(s - m)
            den = jnp.sum(p, axis=-1, keepdims=True) + jnp.exp(sink - m)
            o = jnp.dot(p.astype(BF16), vb, preferred_element_type=F32) / den
            for g in range(SWA_GROUP):
                out_heads[h * SWA_GROUP + g] = o[g * BLOCK:(g + 1) * BLOCK]
        oswa_ref[n * BLOCK:(n + 1) * BLOCK, :] = jnp.concatenate(out_heads, axis=1).astype(BF16)

    qm = qm_ref[...].astype(F32)
    outs = []
    for hd in range(MEM_HEADS):
        sl = slice(hd * MEM_HEAD_DIM, (hd + 1) * MEM_HEAD_DIM)
        qh = qm[:, sl]
        qh = qh * (_head_rs(qh, QK_EPS) * (MEM_HEAD_DIM ** -0.5)) * gmq_ref[...]
        s = lax.dot_general(qh.astype(BF16), mk_ref[:, sl], (((1,), (1,)), ((), ())),
                            preferred_element_type=F32)
        m = jnp.max(s, axis=-1, keepdims=True)
        p = jnp.exp(s - m)
        den = jnp.sum(p, axis=-1, keepdims=True)
        outs.append(jnp.dot(p.astype(BF16), mv_ref[:, sl], preferred_element_type=F32) / den)
    omem_ref[...] = jnp.concatenate(outs, axis=1).astype(BF16)


def _attention(sinks, q, k, v, posb, qm, mk, mv, gq, gk, invf, gmq, bsz, seq, tq):
    t = bsz * seq
    nj = seq // tq
    per = tq // BLOCK
    row = lambda b, j: (b * nj + j, 0)
    prev = lambda b, j: (b * (seq // BLOCK) + jnp.maximum(j * per - 1, 0), 0)
    mem = lambda b, j: (b, 0)
    kern = functools.partial(_attn_kernel, tq=tq)
    return pl.pallas_call(
        kern,
        grid=(bsz, nj),
        in_specs=[
            pl.BlockSpec(memory_space=pltpu.SMEM),
            pl.BlockSpec((tq, SWA_Q_WIDTH), row),
            pl.BlockSpec((tq, SWA_KV_WIDTH), row),
            pl.BlockSpec((tq, SWA_KV_WIDTH), row),
            pl.BlockSpec((BLOCK, SWA_KV_WIDTH), prev),
            pl.BlockSpec((BLOCK, SWA_KV_WIDTH), prev),
            pl.BlockSpec((tq, LANES), row),
            pl.BlockSpec((BLOCK, LANES), prev),
            pl.BlockSpec((tq, MEM_WIDTH), row),
            pl.BlockSpec((MEM_LEN, MEM_WIDTH), mem),
            pl.BlockSpec((MEM_LEN, MEM_WIDTH), mem),
            _const_spec((1, SWA_Q_WIDTH)),
            _const_spec((1, SWA_KV_WIDTH)),
            _const_spec((1, LANES)),
            _const_spec((1, MEM_HEAD_DIM)),
        ],
        out_specs=[
            pl.BlockSpec((tq, SWA_Q_WIDTH), row),
            pl.BlockSpec((tq, MEM_WIDTH), row),
        ],
        out_shape=[
            jax.ShapeDtypeStruct((t, SWA_Q_WIDTH), BF16),
            jax.ShapeDtypeStruct((t, MEM_WIDTH), BF16),
        ],
        compiler_params=_cparams(("parallel", "arbitrary")),
    )(sinks, q, k, v, k, v, posb, posb, qm, mk, mv, gq, gk, invf, gmq)


def _split_bf16(a):
    hi = a.astype(BF16)
    lo = (a - hi.astype(F32)).astype(BF16)
    return hi, lo


def _merge_kernel(x_ref, ys_ref, yw_ref, ym_ref, gate_ref, w0_ref, w1_ref, w2_ref, wo_ref, gf_ref,
                  wr_hi_ref, wr_lo_ref, br_ref, x1_ref, t_ref, comb_ref):
    gates = gate_ref[...]
    dot = lambda a, b: jnp.dot(a, b, preferred_element_type=F32)
    merged = (gates[:, 0:D_MODEL].astype(F32) * dot(ys_ref[...], w0_ref[...])
              + gates[:, D_MODEL:2 * D_MODEL].astype(F32) * dot(yw_ref[...], w1_ref[...])
              + gates[:, 2 * D_MODEL:].astype(F32) * dot(ym_ref[...], w2_ref[...]))
    x1 = x_ref[...] + dot(merged.astype(BF16), wo_ref[...])
    x1_ref[...] = x1
    t = x1 * lax.rsqrt(jnp.mean(x1 * x1, axis=-1, keepdims=True) + NORM_EPS) * gf_ref[...]
    t_ref[...] = t.astype(BF16)

    t_hi, t_lo = _split_bf16(t)
    logits = (dot(t_hi, wr_hi_ref[...]) + dot(t_hi, wr_lo_ref[...]) + dot(t_lo, wr_hi_ref[...])
              + br_ref[...])
    lane = lax.broadcasted_iota(jnp.int32, logits.shape, 1).astype(F32)
    work = logits
    vals, sels = [], []
    for _ in range(TOP_K):
        m = jnp.max(work, axis=-1, keepdims=True)
        first = jnp.min(jnp.where(work == m, lane, float(LANES)), axis=-1, keepdims=True)
        sel = lane == first
        work = jnp.where(sel, NEG_BIG * 2.0, work)
        vals.append(m)
        sels.append(sel)
    exps = [jnp.exp(v - vals[0]) for v in vals]
    den = exps[0] + exps[1] + exps[2] + exps[3]
    comb = jnp.zeros_like(logits)
    for e, sel in zip(exps, sels):
        comb = comb + jnp.where(sel, e / den, 0.0)
    comb_ref[...] = comb


def _merge(x2, y_ssm_tm, y_swa, y_mem, gates, w0, w1, w2, wo, gf, wr_hi, wr_lo, br, bsz, seq, tm):
    t = bsz * seq
    nj = seq // tm
    row = lambda b, j: (b * nj + j, 0)
    return pl.pallas_call(
        _merge_kernel,
        grid=(bsz, nj),
        in_specs=[
            pl.BlockSpec((tm, D_MODEL), row),
            pl.BlockSpec((tm, SSM_WIDTH), lambda b, j: (j, b)),
            pl.BlockSpec((tm, SWA_Q_WIDTH), row),
            pl.BlockSpec((tm, MEM_WIDTH), row),
            pl.BlockSpec((tm, GATE_WIDTH), row),
            _const_spec((SSM_WIDTH, D_MODEL)),
            _const_spec((SWA_Q_WIDTH, D_MODEL)),
            _const_spec((MEM_WIDTH, D_MODEL)),
            _const_spec((D_MODEL, D_MODEL)),
            _const_spec((1, D_MODEL)),
            _const_spec((D_MODEL, LANES)),
            _const_spec((D_MODEL, LANES)),
            _const_spec((1, LANES)),
        ],
        out_specs=[
            pl.BlockSpec((tm, D_MODEL), row),
            pl.BlockSpec((tm, D_MODEL), row),
            pl.BlockSpec((tm, LANES), row),
        ],
        out_shape=[
            jax.ShapeDtypeStruct((t, D_MODEL), F32),
            jax.ShapeDtypeStruct((t, D_MODEL), BF16),
            jax.ShapeDtypeStruct((t, LANES), F32),
        ],
        compiler_params=_cparams(("parallel", "parallel")),
    )(x2, y_ssm_tm, y_swa, y_mem, gates, w0, w1, w2, wo, gf, wr_hi, wr_lo, br)


def _moe_dense_kernel(x1_ref, t_ref, comb_ref, wg_ref, wl_ref, bg_ref, bl_ref, w2_ref, b2_ref, o_ref):
    e = pl.program_id(1)

    @pl.when(e == 0)
    def _():
        o_ref[...] = x1_ref[...]

    t = t_ref[...]
    hg = jnp.dot(t, wg_ref[0], preferred_element_type=F32) + bg_ref[0]
    hl = jnp.dot(t, wl_ref[0], preferred_element_type=F32) + bl_ref[0]
    xg = jnp.minimum(hg, SWIGLU_LIMIT)
    xl = jnp.clip(hl, -SWIGLU_LIMIT, SWIGLU_LIMIT)
    act = xg * _sigmoid(SWIGLU_ALPHA * xg) * (xl + 1.0)
    y = jnp.dot(act.astype(BF16), w2_ref[0], preferred_element_type=F32) + b2_ref[0]
    comb = comb_ref[...]
    lane = lax.broadcasted_iota(jnp.int32, comb.shape, 1)
    w = jnp.sum(jnp.where(lane == e, comb, 0.0), axis=-1, keepdims=True)
    o_ref[...] += w * y


def _moe_dense(x1, t, comb, wg, wl, bg, bl, w2, b2, tm):
    tt = x1.shape[0]
    tok = lambda i, e: (i, 0)
    exp3 = lambda i, e: (e, 0, 0)
    return pl.pallas_call(
        _moe_dense_kernel,
        grid=(tt // tm, N_EXPERTS),
        in_specs=[
            pl.BlockSpec((tm, D_MODEL), tok),
            pl.BlockSpec((tm, D_MODEL), tok),
            pl.BlockSpec((tm, LANES), tok),
            pl.BlockSpec((1, D_MODEL, D_EXPERT), exp3),
            pl.BlockSpec((1, D_MODEL, D_EXPERT), exp3),
            pl.BlockSpec((1, 1, D_EXPERT), exp3),
            pl.BlockSpec((1, 1, D_EXPERT), exp3),
            pl.BlockSpec((1, D_EXPERT, D_MODEL), exp3),
            pl.BlockSpec((1, 1, D_MODEL), exp3),
        ],
        out_specs=pl.BlockSpec((tm, D_MODEL), tok),
        out_shape=jax.ShapeDtypeStruct((tt, D_MODEL), F32),
        compiler_params=_cparams(("parallel", "arbitrary")),
    )(x1, t, comb, wg, wl, bg, bl, w2, b2)


def kernel(x, mem, positions, attn_norm_g, mem_norm_g, w_in, b_gate, ssm_log_dt, ssm_a_re, ssm_a_im,
           ssm_b_re, ssm_b_im, ssm_c_re, ssm_c_im, ssm_d, w_glu_v, w_glu_g, swa_q_gain, swa_k_gain,
           swa_sinks, w_mem_kv, mem_q_gain, mem_k_gain, w_br_ssm, w_br_swa, w_br_mem, w_out,
           ffn_norm_g, w_router, b_router, w_mlp1, b_mlp1, w_mlp2, b_mlp2):
    bsz, seq, _ = x.shape
    depth = w_in.shape[0]
    t = bsz * seq
    tm = min(512, seq)
    tq = min(512, seq)
    steps = min(32, seq)

    inv_freq = ROPE_THETA ** (-jnp.arange(ROPE_HALF, dtype=F32) / ROPE_HALF)
    lane = jnp.arange(LANES) % HEAD_DIM
    invf = jnp.where(lane < ROPE_DIM, inv_freq[lane % ROPE_HALF], 0.0).reshape(1, LANES).astype(F32)
    posb = jnp.broadcast_to(positions.astype(F32).reshape(t, 1), (t, LANES))
    mem2 = mem.reshape(bsz * MEM_LEN, D_MODEL)

    x2 = x.reshape(t, D_MODEL)
    for l in range(depth):
        row = lambda a: a[l].reshape(1, -1).astype(F32)
        u_tm, q, k, v, qm, gates = _in_projection(
            x2, row(attn_norm_g), w_in[l].astype(BF16), row(b_gate), bsz, seq, tm)

        bmat, abr, abi, cmat = _s5_discretize(ssm_log_dt[l], ssm_a_re[l], ssm_a_im[l], ssm_b_re[l],
                                              ssm_b_im[l], ssm_c_re[l], ssm_c_im[l])
        y_ssm_tm = _s5_branch(u_tm.reshape(seq * bsz, SSM_WIDTH), bmat, abr, abi, cmat, row(ssm_d),
                              w_glu_v[l].astype(BF16), w_glu_g[l].astype(BF16), bsz, seq, steps)
        y_ssm_tm = y_ssm_tm.reshape(seq, bsz * SSM_WIDTH)

        mk, mv = _mem_kv(mem2, row(mem_norm_g), w_mem_kv[l].astype(BF16), row(mem_k_gain), bsz)
        y_swa, y_mem = _attention(
            swa_sinks[l].astype(F32), q, k, v, posb, qm, mk, mv,
            jnp.tile(row(swa_q_gain), (1, SWA_Q_HEADS)), jnp.tile(row(swa_k_gain), (1, SWA_KV_HEADS)),
            invf, row(mem_q_gain), bsz, seq, tq)

        wr = jnp.pad(w_router[l].astype(F32), ((0, 0), (0, LANES - N_EXPERTS)))
        wr_hi = wr.astype(BF16)
        wr_lo = (wr - wr_hi.astype(F32)).astype(BF16)
        br = jnp.pad(b_router[l].astype(F32), (0, LANES - N_EXPERTS), constant_values=NEG_BIG).reshape(1, LANES)
        x1, tn, comb = _merge(x2, y_ssm_tm, y_swa, y_mem, gates, w_br_ssm[l].astype(BF16),
                              w_br_swa[l].astype(BF16), w_br_mem[l].astype(BF16), w_out[l].astype(BF16),
                              row(ffn_norm_g), wr_hi, wr_lo, br, bsz, seq, tm)

        w1 = w_mlp1[l]
        b1 = b_mlp1[l].astype(F32)
        x2 = _moe_dense(x1, tn, comb, w1[:, :, 0::2].astype(BF16), w1[:, :, 1::2].astype(BF16),
                        b1[:, None, 0::2], b1[:, None, 1::2], w_mlp2[l].astype(BF16),
                        b_mlp2[l].astype(F32)[:, None, :], min(1024, t))
    return x2.reshape(bsz, seq, D_MODEL)
```

```python
import functools
import math

import jax
import jax.numpy as jnp
from jax import lax
from jax.experimental import pallas as pl
from jax.experimental.pallas import tpu as pltpu

F32 = jnp.float32
BF16 = jnp.bfloat16

D_MODEL = 1024
MEM_LEN = 256
NORM_EPS = 1e-5
QK_EPS = 1e-6

SSM_WIDTH = 512
SSM_GROUP = 16
SSM_GROUPS = 32
SSM_STATE = 64
SSM_COMPLEX = SSM_GROUPS * SSM_STATE

HEAD_DIM = 64
SWA_Q_HEADS = 16
SWA_KV_HEADS = 4
SWA_GROUP = 4
BLOCK = 128
ROPE_THETA = 500000.0
ROPE_DIM = 16
ROPE_HALF = 8

MEM_HEADS = 4
MEM_HEAD_DIM = 128

SWA_Q_WIDTH = 1024
SWA_KV_WIDTH = 256
MEM_WIDTH = 512
N_BRANCH = 3
GATE_WIDTH = N_BRANCH * D_MODEL
OFF_U, OFF_Q, OFF_K, OFF_V, OFF_QM, OFF_G, OFF_END = 0, 512, 1536, 1792, 2048, 2560, 5632

N_EXPERTS = 32
TOP_K = 4
D_EXPERT = 1024
SWIGLU_ALPHA = 1.702
SWIGLU_LIMIT = 7.0

LANES = 128
SUBLANES = 8
VMEM_LIMIT = 56 * 1024 * 1024
NEG_BIG = -1e30

ROW_SUBLANES = D_MODEL // LANES
MOE_TOKEN_BLOCK = 4096
MOE_TILE = 256


def _sigmoid(x):
    return 1.0 / (1.0 + jnp.exp(-x))


def _cparams(sem):
    return pltpu.CompilerParams(dimension_semantics=sem, vmem_limit_bytes=VMEM_LIMIT)


def _const_spec(shape):
    nd = len(shape)
    return pl.BlockSpec(shape, lambda *_: (0,) * nd)


def _inproj_kernel(x_ref, g_ref, w_ref, bg_ref, u_ref, q_ref, k_ref, v_ref, qm_ref, gate_ref):
    x = x_ref[...]
    h = x * lax.rsqrt(jnp.mean(x * x, axis=-1, keepdims=True) + NORM_EPS) * g_ref[...]
    hb = h.astype(BF16)

    def proj(lo, hi):
        return jnp.dot(hb, w_ref[:, lo:hi], preferred_element_type=F32)

    u_ref[...] = proj(OFF_U, OFF_Q).astype(BF16)
    q_ref[...] = proj(OFF_Q, OFF_K).astype(BF16)
    k_ref[...] = proj(OFF_K, OFF_V).astype(BF16)
    v_ref[...] = proj(OFF_V, OFF_QM).astype(BF16)
    qm_ref[...] = proj(OFF_QM, OFF_G).astype(BF16)
    gate_ref[...] = _sigmoid(proj(OFF_G, OFF_END) + bg_ref[...]).astype(BF16)


def _in_projection(x2, g, w_in_b, b_gate, bsz, seq, tm):
    t = bsz * seq
    nj = seq // tm
    row = lambda b, j: (b * nj + j, 0)
    return pl.pallas_call(
        _inproj_kernel,
        grid=(bsz, nj),
        in_specs=[
            pl.BlockSpec((tm, D_MODEL), row),
            _const_spec((1, D_MODEL)),
            _const_spec((D_MODEL, OFF_END)),
            _const_spec((1, GATE_WIDTH)),
        ],
        out_specs=[
            pl.BlockSpec((tm, SSM_WIDTH), lambda b, j: (j, b)),
            pl.BlockSpec((tm, SWA_Q_WIDTH), row),
            pl.BlockSpec((tm, SWA_KV_WIDTH), row),
            pl.BlockSpec((tm, SWA_KV_WIDTH), row),
            pl.BlockSpec((tm, MEM_WIDTH), row),
            pl.BlockSpec((tm, GATE_WIDTH), row),
        ],
        out_shape=[
            jax.ShapeDtypeStruct((seq, bsz * SSM_WIDTH), BF16),
            jax.ShapeDtypeStruct((t, SWA_Q_WIDTH), BF16),
            jax.ShapeDtypeStruct((t, SWA_KV_WIDTH), BF16),
            jax.ShapeDtypeStruct((t, SWA_KV_WIDTH), BF16),
            jax.ShapeDtypeStruct((t, MEM_WIDTH), BF16),
            jax.ShapeDtypeStruct((t, GATE_WIDTH), BF16),
        ],
        compiler_params=_cparams(("parallel", "parallel")),
    )(x2, g, w_in_b, b_gate)


def _gelu_tanh(x):
    c = math.sqrt(2.0 / math.pi)
    return 0.5 * x * (1.0 + jnp.tanh(c * (x + 0.044715 * (x * x * x))))


def _s5_kernel(u_ref, bmat_ref, are_ref, aim_ref, cmat_ref, d_ref, wv_ref, wg_ref, o_ref,
               st_ref, bu_ref, *, bsz, steps, col_chunk):
    nc = SSM_COMPLEX

    @pl.when(pl.program_id(0) == 0)
    def _():
        st_ref[...] = jnp.zeros_like(st_ref)

    u = u_ref[...]
    for lo in range(0, 2 * nc, 1024):
        bu_ref[:, lo:lo + 1024] = jnp.dot(u, bmat_ref[:, lo:lo + 1024], preferred_element_type=F32)

    for lo in range(0, nc, col_chunk):
        re = slice(lo, lo + col_chunk)
        im = slice(nc + lo, nc + lo + col_chunk)
        ar = jnp.broadcast_to(are_ref[:, re], (bsz, col_chunk))
        ai = jnp.broadcast_to(aim_ref[:, re], (bsz, col_chunk))

        def step(t, carry):
            sr, si = carry
            rows = pl.ds(pl.multiple_of(t * bsz, bsz), bsz)
            nr = ar * sr - ai * si + bu_ref[rows, re]
            ni = ar * si + ai * sr + bu_ref[rows, im]
            bu_ref[rows, re] = nr
            bu_ref[rows, im] = ni
            return nr, ni

        sr, si = lax.fori_loop(0, steps, step, (st_ref[:, re], st_ref[:, im]))
        st_ref[:, re] = sr
        st_ref[:, im] = si

    y = jnp.dot(bu_ref[...].astype(BF16), cmat_ref[...], preferred_element_type=F32)
    y = y + d_ref[...] * u.astype(F32)
    yb = _gelu_tanh(y).astype(BF16)
    val = jnp.dot(yb, wv_ref[...], preferred_element_type=F32)
    gate = jnp.dot(yb, wg_ref[...], preferred_element_type=F32)
    o_ref[...] = (val * _sigmoid(gate)).astype(BF16)


def _s5_branch(u_tm, bmat, a_re, a_im, cmat, d_skip, wv, wg, bsz, seq, steps):
    rows = steps * bsz
    kern = functools.partial(_s5_kernel, bsz=bsz, steps=steps, col_chunk=512)
    return pl.pallas_call(
        kern,
        grid=(seq // steps,),
        in_specs=[
            pl.BlockSpec((rows, SSM_WIDTH), lambda i: (i, 0)),
            _const_spec((SSM_WIDTH, 2 * SSM_COMPLEX)),
            _const_spec((1, SSM_COMPLEX)),
            _const_spec((1, SSM_COMPLEX)),
            _const_spec((2 * SSM_COMPLEX, SSM_WIDTH)),
            _const_spec((1, SSM_WIDTH)),
            _const_spec((SSM_WIDTH, SSM_WIDTH)),
            _const_spec((SSM_WIDTH, SSM_WIDTH)),
        ],
        out_specs=pl.BlockSpec((rows, SSM_WIDTH), lambda i: (i, 0)),
        out_shape=jax.ShapeDtypeStruct((seq * bsz, SSM_WIDTH), BF16),
        scratch_shapes=[
            pltpu.VMEM((bsz, 2 * SSM_COMPLEX), F32),
            pltpu.VMEM((rows, 2 * SSM_COMPLEX), F32),
        ],
        compiler_params=_cparams(("arbitrary",)),
    )(u_tm, bmat, a_re, a_im, cmat, d_skip, wv, wg)


def _s5_discretize(log_dt, a_re, a_im, b_re, b_im, c_re, c_im):
    dt = jnp.exp(log_dt.astype(F32))[:, None]
    lr = jnp.minimum(a_re.astype(F32), -1e-4)
    li = a_im.astype(F32)
    mag = jnp.exp(lr * dt)
    abr = mag * jnp.cos(li * dt)
    abi = mag * jnp.sin(li * dt)
    nr, ni = abr - 1.0, abi
    den = lr * lr + li * li
    fr = (nr * lr + ni * li) / den
    fi = (ni * lr - nr * li) / den
    br, bi = b_re.astype(F32), b_im.astype(F32)
    bbr = fr[..., None] * br - fi[..., None] * bi
    bbi = fr[..., None] * bi + fi[..., None] * br
    eye = jnp.eye(SSM_GROUPS, dtype=F32)

    def blockdiag_in(m):
        return jnp.einsum('gpc,gh->gchp', m, eye).reshape(SSM_WIDTH, SSM_COMPLEX)

    def blockdiag_out(m):
        return jnp.einsum('gcp,gh->gphc', m, eye).reshape(SSM_COMPLEX, SSM_WIDTH)

    bmat = jnp.concatenate([blockdiag_in(bbr), blockdiag_in(bbi)], axis=1).astype(BF16)
    cmat = jnp.concatenate([blockdiag_out(c_re.astype(F32)), -blockdiag_out(c_im.astype(F32))],
                           axis=0).astype(BF16)
    return bmat, abr.reshape(1, SSM_COMPLEX), abi.reshape(1, SSM_COMPLEX), cmat


def _memkv_kernel(mem_ref, g_ref, w_ref, kg_ref, mk_ref, mv_ref):
    x = mem_ref[...]
    h = x * lax.rsqrt(jnp.mean(x * x, axis=-1, keepdims=True) + NORM_EPS) * g_ref[...]
    kv = jnp.dot(h.astype(BF16), w_ref[...], preferred_element_type=F32)
    parts = []
    for hd in range(MEM_HEADS):
        kh = kv[:, hd * MEM_HEAD_DIM:(hd + 1) * MEM_HEAD_DIM]
        kh = kh * lax.rsqrt(jnp.mean(kh * kh, axis=-1, keepdims=True) + QK_EPS) * kg_ref[...]
        parts.append(kh)
    mk_ref[...] = jnp.concatenate(parts, axis=1).astype(BF16)
    mv_ref[...] = kv[:, MEM_WIDTH:].astype(BF16)


def _mem_kv(mem2, g, w_b, k_gain, bsz):
    return pl.pallas_call(
        _memkv_kernel,
        grid=(bsz,),
        in_specs=[
            pl.BlockSpec((MEM_LEN, D_MODEL), lambda b: (b, 0)),
            _const_spec((1, D_MODEL)),
            _const_spec((D_MODEL, 2 * MEM_WIDTH)),
            _const_spec((1, MEM_HEAD_DIM)),
        ],
        out_specs=[
            pl.BlockSpec((MEM_LEN, MEM_WIDTH), lambda b: (b, 0)),
            pl.BlockSpec((MEM_LEN, MEM_WIDTH), lambda b: (b, 0)),
        ],
        out_shape=[
            jax.ShapeDtypeStruct((bsz * MEM_LEN, MEM_WIDTH), BF16),
            jax.ShapeDtypeStruct((bsz * MEM_LEN, MEM_WIDTH), BF16),
        ],
        compiler_params=_cparams(("parallel",)),
    )(mem2, g, w_b, k_gain)


def _rope_tables(pos, invf):
    ang = pos * invf
    c = jnp.cos(ang)
    s = jnp.sin(ang)
    lane = lax.broadcasted_iota(jnp.int32, ang.shape, 1) & (HEAD_DIM - 1)
    s_lo = jnp.where(lane < ROPE_HALF, -s, 0.0)
    s_hi = jnp.where((lane >= ROPE_HALF) & (lane < ROPE_DIM), s, 0.0)
    return c, s_lo, s_hi


def _rope(x, tables):
    c, s_lo, s_hi = tables
    width = x.shape[1]
    rep = width // LANES
    wide = lambda a: jnp.concatenate([a] * rep, axis=1)
    return (x * wide(c) + pltpu.roll(x, width - ROPE_HALF, 1) * wide(s_lo)
            + pltpu.roll(x, ROPE_HALF, 1) * wide(s_hi))


def _head_rs(x, eps):
    return lax.rsqrt(jnp.mean(x * x, axis=-1, keepdims=True) + eps)


def _attn_kernel(sink_ref, q_ref, kc_ref, vc_ref, kp_ref, vp_ref, pos_ref, posp_ref, qm_ref,
                 mk_ref, mv_ref, gq_ref, gk_ref, invf_ref, gmq_ref, oswa_ref, omem_ref, *, tq):
    nblk = tq // BLOCK
    first_tile = pl.program_id(1) == 0
    invf = invf_ref[...]

    k_raw = jnp.concatenate([kp_ref[...], kc_ref[...]], axis=0).astype(F32)
    v_all = jnp.concatenate([vp_ref[...], vc_ref[...]], axis=0)
    pos_k = jnp.concatenate([posp_ref[...], pos_ref[...]], axis=0)
    k_rot = _rope(k_raw * gk_ref[...], _rope_tables(pos_k, invf))
    k_heads, v_heads = [], []
    for h in range(SWA_KV_HEADS):
        sl = slice(h * HEAD_DIM, (h + 1) * HEAD_DIM)
        k_heads.append((k_rot[:, sl] * _head_rs(k_raw[:, sl], QK_EPS)).astype(BF16))
        v_heads.append(v_all[:, sl])

    q_raw = q_ref[...].astype(F32)
    q_rot = _rope(q_raw * gq_ref[...], _rope_tables(pos_ref[...], invf))
    q_heads = []
    for hd in range(SWA_Q_HEADS):
        sl = slice(hd * HEAD_DIM, (hd + 1) * HEAD_DIM)
        scale = _head_rs(q_raw[:, sl], QK_EPS) * (HEAD_DIM ** -0.5)
        q_heads.append((q_rot[:, sl] * scale).astype(BF16))

    rows = SWA_GROUP * BLOCK
    qi = lax.broadcasted_iota(jnp.int32, (rows, 2 * BLOCK), 0) & (BLOCK - 1)
    kj = lax.broadcasted_iota(jnp.int32, (rows, 2 * BLOCK), 1)
    in_cur = kj >= BLOCK
    cur_ok = in_cur & ((kj - BLOCK) <= qi)
    band = cur_ok | (jnp.logical_not(in_cur) & (kj > qi))
    first_off = jnp.where(first_tile, BLOCK, 0)
    band_first = cur_ok | (jnp.logical_not(in_cur) & (kj > qi + first_off))

    for n in range(nblk):
        valid = band_first if n == 0 else band
        out_heads = [None] * SWA_Q_HEADS
        for h in range(SWA_KV_HEADS):
            qs = jnp.concatenate(
                [q_heads[h * SWA_GROUP + g][n * BLOCK:(n + 1) * BLOCK] for g in range(SWA_GROUP)], axis=0)
            kb = k_heads[h][n * BLOCK:(n + 2) * BLOCK]
            vb = v_heads[h][n * BLOCK:(n + 2) * BLOCK]
            s = lax.dot_general(qs, kb, (((1,), (1,)), ((), ())), preferred_element_type=F32)
            s = jnp.where(valid, s, NEG_BIG)
            sink = jnp.concatenate(
                [jnp.full((BLOCK, 1), sink_ref[h * SWA_GROUP + g], F32) for g in range(SWA_GROUP)], axis=0)
            m = jnp.maximum(jnp.max(s, axis=-1, keepdims=True), sink)
            p = jnp.exp(s - m)
            den = jnp.sum(p, axis=-1, keepdims=True) + jnp.exp(sink - m)
            o = jnp.dot(p.astype(BF16), vb, preferred_element_type=F32) / den
            for g in range(SWA_GROUP):
                out_heads[h * SWA_GROUP + g] = o[g * BLOCK:(g + 1) * BLOCK]
        oswa_ref[n * BLOCK:(n + 1) * BLOCK, :] = jnp.concatenate(out_heads, axis=1).astype(BF16)

    qm = qm_ref[...].astype(F32)
    outs = []
    for hd in range(MEM_HEADS):
        sl = slice(hd * MEM_HEAD_DIM, (hd + 1) * MEM_HEAD_DIM)
        qh = qm[:, sl]
        qh = qh * (_head_rs(qh, QK_EPS) * (MEM_HEAD_DIM ** -0.5)) * gmq_ref[...]
        s = lax.dot_general(qh.astype(BF16), mk_ref[:, sl], (((1,), (1,)), ((), ())),
                            preferred_element_type=F32)
        m = jnp.max(s, axis=-1, keepdims=True)
        p = jnp.exp(s - m)
        den = jnp.sum(p, axis=-1, keepdims=True)
        outs.append(jnp.dot(p.astype(BF16), mv_ref[:, sl], preferred_element_type=F32) / den)
    omem_ref[...] = jnp.concatenate(outs, axis=1).astype(BF16)


def _attention(sinks, q, k, v, posb, qm, mk, mv, gq, gk, invf, gmq, bsz, seq, tq):
    t = bsz * seq
    nj = seq // tq
    per = tq // BLOCK
    row = lambda b, j: (b * nj + j, 0)
    prev = lambda b, j: (b * (seq // BLOCK) + jnp.maximum(j * per - 1, 0), 0)
    mem = lambda b, j: (b, 0)
    kern = functools.partial(_attn_kernel, tq=tq)
    return pl.pallas_call(
        kern,
        grid=(bsz, nj),
        in_specs=[
            pl.BlockSpec(memory_space=pltpu.SMEM),
            pl.BlockSpec((tq, SWA_Q_WIDTH), row),
            pl.BlockSpec((tq, SWA_KV_WIDTH), row),
            pl.BlockSpec((tq, SWA_KV_WIDTH), row),
            pl.BlockSpec((BLOCK, SWA_KV_WIDTH), prev),
            pl.BlockSpec((BLOCK, SWA_KV_WIDTH), prev),
            pl.BlockSpec((tq, LANES), row),
            pl.BlockSpec((BLOCK, LANES), prev),
            pl.BlockSpec((tq, MEM_WIDTH), row),
            pl.BlockSpec((MEM_LEN, MEM_WIDTH), mem),
            pl.BlockSpec((MEM_LEN, MEM_WIDTH), mem),
            _const_spec((1, SWA_Q_WIDTH)),
            _const_spec((1, SWA_KV_WIDTH)),
            _const_spec((1, LANES)),
            _const_spec((1, MEM_HEAD_DIM)),
        ],
        out_specs=[
            pl.BlockSpec((tq, SWA_Q_WIDTH), row),
            pl.BlockSpec((tq, MEM_WIDTH), row),
        ],
        out_shape=[
            jax.ShapeDtypeStruct((t, SWA_Q_WIDTH), BF16),
            jax.ShapeDtypeStruct((t, MEM_WIDTH), BF16),
        ],
        compiler_params=_cparams(("parallel", "arbitrary")),
    )(sinks, q, k, v, k, v, posb, posb, qm, mk, mv, gq, gk, invf, gmq)


def _split_bf16(a):
    hi = a.astype(BF16)
    lo = (a - hi.astype(F32)).astype(BF16)
    return hi, lo


def _merge_kernel(x_ref, ys_ref, yw_ref, ym_ref, gate_ref, w0_ref, w1_ref, w2_ref, wo_ref, gf_ref,
                  wr_hi_ref, wr_lo_ref, br_ref, x1_ref, tsl_ref, idx_ref, wt_ref, *, tm):
    gates = gate_ref[...]
    dot = lambda a, b: jnp.dot(a, b, preferred_element_type=F32)
    merged = (gates[:, 0:D_MODEL].astype(F32) * dot(ys_ref[...], w0_ref[...])
              + gates[:, D_MODEL:2 * D_MODEL].astype(F32) * dot(yw_ref[...], w1_ref[...])
              + gates[:, 2 * D_MODEL:].astype(F32) * dot(ym_ref[...], w2_ref[...]))
    x1 = x_ref[...] + dot(merged.astype(BF16), wo_ref[...])
    x1_ref[...] = x1
    t = x1 * lax.rsqrt(jnp.mean(x1 * x1, axis=-1, keepdims=True) + NORM_EPS) * gf_ref[...]

    for j in range(ROW_SUBLANES):
        tsl_ref[pl.ds(j, tm, stride=ROW_SUBLANES), :] = t[:, j * LANES:(j + 1) * LANES]

    t_hi, t_lo = _split_bf16(t)
    logits = (dot(t_hi, wr_hi_ref[...]) + dot(t_hi, wr_lo_ref[...]) + dot(t_lo, wr_hi_ref[...])
              + br_ref[...])
    lane = lax.broadcasted_iota(jnp.int32, logits.shape, 1).astype(F32)
    work = logits
    vals, firsts = [], []
    for _ in range(TOP_K):
        m = jnp.max(work, axis=-1, keepdims=True)
        first = jnp.min(jnp.where(work == m, lane, float(LANES)), axis=-1, keepdims=True)
        work = jnp.where(lane == first, NEG_BIG * 2.0, work)
        vals.append(m)
        firsts.append(first)
    exps = [jnp.exp(v - vals[0]) for v in vals]
    den = exps[0] + exps[1] + exps[2] + exps[3]
    idx_out = jnp.zeros_like(logits)
    wt_out = jnp.zeros_like(logits)
    for k in range(TOP_K):
        idx_out = jnp.where(lane == float(k), firsts[k], idx_out)
        wt_out = jnp.where(lane == float(k), exps[k] / den, wt_out)
    idx_ref[...] = idx_out.astype(jnp.int32)
    wt_ref[...] = wt_out


def _merge(x2, y_ssm_tm, y_swa, y_mem, gates, w0, w1, w2, wo, gf, wr_hi, wr_lo, br, bsz, seq, tm):
    t = bsz * seq
    nj = seq // tm
    row = lambda b, j: (b * nj + j, 0)
    return pl.pallas_call(
        functools.partial(_merge_kernel, tm=tm),
        grid=(bsz, nj),
        in_specs=[
            pl.BlockSpec((tm, D_MODEL), row),
            pl.BlockSpec((tm, SSM_WIDTH), lambda b, j: (j, b)),
            pl.BlockSpec((tm, SWA_Q_WIDTH), row),
            pl.BlockSpec((tm, MEM_WIDTH), row),
            pl.BlockSpec((tm, GATE_WIDTH), row),
            _const_spec((SSM_WIDTH, D_MODEL)),
            _const_spec((SWA_Q_WIDTH, D_MODEL)),
            _const_spec((MEM_WIDTH, D_MODEL)),
            _const_spec((D_MODEL, D_MODEL)),
            _const_spec((1, D_MODEL)),
            _const_spec((D_MODEL, LANES)),
            _const_spec((D_MODEL, LANES)),
            _const_spec((1, LANES)),
        ],
        out_specs=[
            pl.BlockSpec((tm, D_MODEL), row),
            pl.BlockSpec((tm * ROW_SUBLANES, LANES), row),
            pl.BlockSpec((tm, LANES), row),
            pl.BlockSpec((tm, LANES), row),
        ],
        out_shape=[
            jax.ShapeDtypeStruct((t, D_MODEL), F32),
            jax.ShapeDtypeStruct((t * ROW_SUBLANES, LANES), F32),
            jax.ShapeDtypeStruct((t, LANES), jnp.int32),
            jax.ShapeDtypeStruct((t, LANES), F32),
        ],
        compiler_params=_cparams(("parallel", "parallel")),
    )(x2, y_ssm_tm, y_swa, y_mem, gates, w0, w1, w2, wo, gf, wr_hi, wr_lo, br)


DEINT_BLOCK = 2 * LANES


def _expert_prep_kernel(w1_ref, perm_ref, wg_ref, wl_ref):
    perm = perm_ref[...]
    for blk in range(2 * D_EXPERT // DEINT_BLOCK):
        cols = w1_ref[0, :, blk * DEINT_BLOCK:(blk + 1) * DEINT_BLOCK].astype(BF16)
        z = jnp.dot(cols, perm, preferred_element_type=F32).astype(BF16)
        wg_ref[0, :, blk * LANES:(blk + 1) * LANES] = z[:, :LANES]
        wl_ref[0, :, blk * LANES:(blk + 1) * LANES] = z[:, LANES:]


def _expert_prep(w1):
    src = jnp.arange(DEINT_BLOCK)
    dst = jnp.where(src % 2 == 0, src // 2, LANES + src // 2)
    perm = (dst[:, None] == jnp.arange(DEINT_BLOCK)[None, :]).astype(BF16)
    exp3 = lambda e: (e, 0, 0)
    return pl.pallas_call(
        _expert_prep_kernel,
        grid=(N_EXPERTS,),
        in_specs=[pl.BlockSpec((1, D_MODEL, 2 * D_EXPERT), exp3), _const_spec((DEINT_BLOCK, DEINT_BLOCK))],
        out_specs=[pl.BlockSpec((1, D_MODEL, D_EXPERT), exp3), pl.BlockSpec((1, D_MODEL, D_EXPERT), exp3)],
        out_shape=[jax.ShapeDtypeStruct((N_EXPERTS, D_MODEL, D_EXPERT), BF16)] * 2,
        compiler_params=_cparams(("parallel",)),
    )(w1, perm)


SCATTER_BATCH = 8


def _moe_kernel(cnt_ref, off_ref, tok_ref, wt_ref, src_ref, wg_ref, wl_ref, bg_ref, bl_ref, w2_ref,
                b2_ref, out_hbm, acc_ref, gbuf_ref, ybuf_ref, sem, *, nb_tokens, tile):
    blk = pl.program_id(0)
    e = pl.program_id(1)
    acc_rows = nb_tokens * ROW_SUBLANES

    @pl.when(e == 0)
    def _():
        acc_ref[...] = jnp.zeros_like(acc_ref)

    n = cnt_ref[blk * N_EXPERTS + e]
    start = off_ref[blk * N_EXPERTS + e]

    def tile_body(ti, carry):
        base = start + ti * tile
        nvalid = n - ti * tile
        for i in range(tile):
            tok = tok_ref[0, 0, base + i]
            gbuf_ref[pl.ds(i * ROW_SUBLANES, ROW_SUBLANES), :] = (
                src_ref[pl.ds(pl.multiple_of(tok * ROW_SUBLANES, ROW_SUBLANES), ROW_SUBLANES), :])
        x = jnp.concatenate(
            [gbuf_ref[pl.ds(j, tile, stride=ROW_SUBLANES), :] for j in range(ROW_SUBLANES)],
            axis=1).astype(BF16)
        hg = jnp.dot(x, wg_ref[0], preferred_element_type=F32) + bg_ref[0]
        hl = jnp.dot(x, wl_ref[0], preferred_element_type=F32) + bl_ref[0]
        xg = jnp.minimum(hg, SWIGLU_LIMIT)
        xl = jnp.clip(hl, -SWIGLU_LIMIT, SWIGLU_LIMIT)
        act = xg * _sigmoid(SWIGLU_ALPHA * xg) * (xl + 1.0)
        y = jnp.dot(act.astype(BF16), w2_ref[0], preferred_element_type=F32) + b2_ref[0]
        for j in range(ROW_SUBLANES):
            ybuf_ref[pl.ds(j, tile, stride=ROW_SUBLANES), :] = y[:, j * LANES:(j + 1) * LANES]
        for i0 in range(0, tile, SCATTER_BATCH):
            sums, dsts = [], []
            for i in range(i0, i0 + SCATTER_BATCH):
                live = i < nvalid
                tok = jnp.where(live, tok_ref[0, 0, base + i], nb_tokens)
                w = jnp.where(live, wt_ref[0, 0, base + i], 0.0)
                dst = pl.ds(pl.multiple_of(tok * ROW_SUBLANES, ROW_SUBLANES), ROW_SUBLANES)
                sums.append(acc_ref[dst, :] + w * ybuf_ref[pl.ds(i * ROW_SUBLANES, ROW_SUBLANES), :])
                dsts.append(dst)
            for dst, s in zip(dsts, sums):
                acc_ref[dst, :] = s
        return carry

    lax.fori_loop(0, (n + tile - 1) // tile, tile_body, 0)

    @pl.when(e == N_EXPERTS - 1)
    def _():
        cp = pltpu.make_async_copy(acc_ref.at[pl.ds(0, acc_rows), :],
                                   out_hbm.at[pl.ds(blk * acc_rows, acc_rows), :], sem)
        cp.start()
        cp.wait()


def _moe(counts, offsets, tok_sorted, wt_sorted, tsl, wg, wl, bg, bl, w2, b2, t, nb_tokens, tile):
    nblk = t // nb_tokens
    slots = tok_sorted.shape[-1]
    exp3 = lambda b, e, *_: (e, 0, 0)
    blk3 = lambda b, e, *_: (b, 0, 0)
    grid_spec = pltpu.PrefetchScalarGridSpec(
        num_scalar_prefetch=2,
        grid=(nblk, N_EXPERTS),
        in_specs=[
            pl.BlockSpec((1, 1, slots), blk3, memory_space=pltpu.SMEM),
            pl.BlockSpec((1, 1, slots), blk3, memory_space=pltpu.SMEM),
            pl.BlockSpec((nb_tokens * ROW_SUBLANES, LANES), lambda b, e, *_: (b, 0),
                         pipeline_mode=pl.Buffered(1)),
            pl.BlockSpec((1, D_MODEL, D_EXPERT), exp3),
            pl.BlockSpec((1, D_MODEL, D_EXPERT), exp3),
            pl.BlockSpec((1, 1, D_EXPERT), exp3),
            pl.BlockSpec((1, 1, D_EXPERT), exp3),
            pl.BlockSpec((1, D_EXPERT, D_MODEL), exp3),
            pl.BlockSpec((1, 1, D_MODEL), exp3),
        ],
        out_specs=pl.BlockSpec(memory_space=pl.ANY),
        scratch_shapes=[
            pltpu.VMEM(((nb_tokens + 1) * ROW_SUBLANES, LANES), F32),
            pltpu.VMEM((tile * ROW_SUBLANES, LANES), F32),
            pltpu.VMEM((tile * ROW_SUBLANES, LANES), F32),
            pltpu.SemaphoreType.DMA,
        ],
    )
    return pl.pallas_call(
        functools.partial(_moe_kernel, nb_tokens=nb_tokens, tile=tile),
        grid_spec=grid_spec,
        out_shape=jax.ShapeDtypeStruct((t * ROW_SUBLANES, LANES), F32),
        compiler_params=_cparams(("arbitrary", "arbitrary")),
    )(counts, offsets, tok_sorted, wt_sorted, tsl, wg, wl, bg, bl, w2, b2)


def _route(idx, wts, t, nb_tokens, tile):
    nblk = t // nb_tokens
    eid = idx[:, :TOP_K].reshape(nblk, nb_tokens * TOP_K)
    w = wts[:, :TOP_K].reshape(nblk, nb_tokens * TOP_K)
    order = jnp.argsort(eid, axis=1, stable=True)
    tok_sorted = (order // TOP_K).astype(jnp.int32)
    wt_sorted = jnp.take_along_axis(w, order, axis=1)
    counts = jnp.sum((eid[:, :, None] == jnp.arange(N_EXPERTS)[None, None, :]).astype(jnp.int32), axis=1)
    offsets = jnp.cumsum(counts, axis=1) - counts
    pad = ((0, 0), (0, tile))
    tok_sorted = jnp.pad(tok_sorted, pad)[:, None, :]
    wt_sorted = jnp.pad(wt_sorted, pad)[:, None, :]
    return counts.reshape(-1), offsets.reshape(-1).astype(jnp.int32), tok_sorted, wt_sorted


def _residual_kernel(x1_ref, moe_ref, o_ref, *, tm):
    parts = [moe_ref[pl.ds(j, tm, stride=ROW_SUBLANES), :] for j in range(ROW_SUBLANES)]
    o_ref[...] = x1_ref[...] + jnp.concatenate(parts, axis=1)


def _residual(x1, moe, tm):
    t = x1.shape[0]
    return pl.pallas_call(
        functools.partial(_residual_kernel, tm=tm),
        grid=(t // tm,),
        in_specs=[pl.BlockSpec((tm, D_MODEL), lambda i: (i, 0)),
                  pl.BlockSpec((tm * ROW_SUBLANES, LANES), lambda i: (i, 0))],
        out_specs=pl.BlockSpec((tm, D_MODEL), lambda i: (i, 0)),
        out_shape=jax.ShapeDtypeStruct((t, D_MODEL), F32),
        compiler_params=_cparams(("parallel",)),
    )(x1, moe)


def kernel(x, mem, positions, attn_norm_g, mem_norm_g, w_in, b_gate, ssm_log_dt, ssm_a_re, ssm_a_im,
           ssm_b_re, ssm_b_im, ssm_c_re, ssm_c_im, ssm_d, w_glu_v, w_glu_g, swa_q_gain, swa_k_gain,
           swa_sinks, w_mem_kv, mem_q_gain, mem_k_gain, w_br_ssm, w_br_swa, w_br_mem, w_out,
           ffn_norm_g, w_router, b_router, w_mlp1, b_mlp1, w_mlp2, b_mlp2):
    bsz, seq, _ = x.shape
    depth = w_in.shape[0]
    t = bsz * seq
    tm = min(512, seq)
    tq = min(512, seq)
    steps = min(32, seq)

    inv_freq = ROPE_THETA ** (-jnp.arange(ROPE_HALF, dtype=F32) / ROPE_HALF)
    lane = jnp.arange(LANES) % HEAD_DIM
    invf = jnp.where(lane < ROPE_DIM, inv_freq[lane % ROPE_HALF], 0.0).reshape(1, LANES).astype(F32)
    posb = jnp.broadcast_to(positions.astype(F32).reshape(t, 1), (t, LANES))
    mem2 = mem.reshape(bsz * MEM_LEN, D_MODEL)

    x2 = x.reshape(t, D_MODEL)
    for l in range(depth):
        row = lambda a: a[l].reshape(1, -1).astype(F32)
        u_tm, q, k, v, qm, gates = _in_projection(
            x2, row(attn_norm_g), w_in[l].astype(BF16), row(b_gate), bsz, seq, tm)

        bmat, abr, abi, cmat = _s5_discretize(ssm_log_dt[l], ssm_a_re[l], ssm_a_im[l], ssm_b_re[l],
                                              ssm_b_im[l], ssm_c_re[l], ssm_c_im[l])
        y_ssm_tm = _s5_branch(u_tm.reshape(seq * bsz, SSM_WIDTH), bmat, abr, abi, cmat, row(ssm_d),
                              w_glu_v[l].astype(BF16), w_glu_g[l].astype(BF16), bsz, seq, steps)
        y_ssm_tm = y_ssm_tm.reshape(seq, bsz * SSM_WIDTH)

        mk, mv = _mem_kv(mem2, row(mem_norm_g), w_mem_kv[l].astype(BF16), row(mem_k_gain), bsz)
        y_swa, y_mem = _attention(
            swa_sinks[l].astype(F32), q, k, v, posb, qm, mk, mv,
            jnp.tile(row(swa_q_gain), (1, SWA_Q_HEADS)), jnp.tile(row(swa_k_gain), (1, SWA_KV_HEADS)),
            invf, row(mem_q_gain), bsz, seq, tq)

        wr = jnp.pad(w_router[l].astype(F32), ((0, 0), (0, LANES - N_EXPERTS)))
        wr_hi = wr.astype(BF16)
        wr_lo = (wr - wr_hi.astype(F32)).astype(BF16)
        br = jnp.pad(b_router[l].astype(F32), (0, LANES - N_EXPERTS), constant_values=NEG_BIG).reshape(1, LANES)
        x1, tsl, idx, wts = _merge(x2, y_ssm_tm, y_swa, y_mem, gates, w_br_ssm[l].astype(BF16),
                                   w_br_swa[l].astype(BF16), w_br_mem[l].astype(BF16),
                                   w_out[l].astype(BF16), row(ffn_norm_g), wr_hi, wr_lo, br, bsz, seq, tm)

        nb_tokens = min(MOE_TOKEN_BLOCK, t)
        counts, offsets, tok_sorted, wt_sorted = _route(idx, wts, t, nb_tokens, MOE_TILE)
        wg, wl = _expert_prep(w_mlp1[l])
        b1 = b_mlp1[l].astype(F32)
        moe = _moe(counts, offsets, tok_sorted, wt_sorted, tsl, wg, wl, b1[:, None, 0::2],
                   b1[:, None, 1::2], w_mlp2[l].astype(BF16), b_mlp2[l].astype(F32)[:, None, :],
                   t, nb_tokens, MOE_TILE)
        x2 = _residual(x1, moe, tm)
    return x2.reshape(bsz, seq, D_MODEL)
```

```python
import functools
import math

import jax
import jax.numpy as jnp
from jax import lax
from jax.experimental import pallas as pl
from jax.experimental.pallas import tpu as pltpu

F32 = jnp.float32
BF16 = jnp.bfloat16

D_MODEL = 1024
MEM_LEN = 256
NORM_EPS = 1e-5
QK_EPS = 1e-6

SSM_WIDTH = 512
SSM_GROUP = 16
SSM_GROUPS = 32
SSM_STATE = 64
SSM_COMPLEX = SSM_GROUPS * SSM_STATE

HEAD_DIM = 64
SWA_Q_HEADS = 16
SWA_KV_HEADS = 4
SWA_GROUP = 4
BLOCK = 128
ROPE_THETA = 500000.0
ROPE_DIM = 16
ROPE_HALF = 8

MEM_HEADS = 4
MEM_HEAD_DIM = 128

SWA_Q_WIDTH = 1024
SWA_KV_WIDTH = 256
MEM_WIDTH = 512
N_BRANCH = 3
GATE_WIDTH = N_BRANCH * D_MODEL
OFF_U, OFF_Q, OFF_K, OFF_V, OFF_QM, OFF_G, OFF_END = 0, 512, 1536, 1792, 2048, 2560, 5632

N_EXPERTS = 32
TOP_K = 4
D_EXPERT = 1024
SWIGLU_ALPHA = 1.702
SWIGLU_LIMIT = 7.0

LANES = 128
SUBLANES = 8
VMEM_LIMIT = 56 * 1024 * 1024
NEG_BIG = -1e30

ROW_SUBLANES = D_MODEL // LANES
MOE_TOKEN_BLOCK = 4096
MOE_TILE = 256


def _sigmoid(x):
    return 1.0 / (1.0 + jnp.exp(-x))


def _cparams(sem):
    return pltpu.CompilerParams(dimension_semantics=sem, vmem_limit_bytes=VMEM_LIMIT)


def _const_spec(shape):
    nd = len(shape)
    return pl.BlockSpec(shape, lambda *_: (0,) * nd)


def _inproj_kernel(x_ref, g_ref, w_ref, bg_ref, u_ref, q_ref, k_ref, v_ref, qm_ref, gate_ref):
    x = x_ref[...]
    h = x * lax.rsqrt(jnp.mean(x * x, axis=-1, keepdims=True) + NORM_EPS) * g_ref[...]
    hb = h.astype(BF16)

    def proj(lo, hi):
        return jnp.dot(hb, w_ref[:, lo:hi], preferred_element_type=F32)

    u_ref[...] = proj(OFF_U, OFF_Q).astype(BF16)
    q_ref[...] = proj(OFF_Q, OFF_K).astype(BF16)
    k_ref[...] = proj(OFF_K, OFF_V).astype(BF16)
    v_ref[...] = proj(OFF_V, OFF_QM).astype(BF16)
    qm_ref[...] = proj(OFF_QM, OFF_G).astype(BF16)
    gate_ref[...] = _sigmoid(proj(OFF_G, OFF_END) + bg_ref[...]).astype(BF16)


def _in_projection(x2, g, w_in_b, b_gate, bsz, seq, tm):
    t = bsz * seq
    nj = seq // tm
    row = lambda b, j: (b * nj + j, 0)
    return pl.pallas_call(
        _inproj_kernel,
        grid=(bsz, nj),
        in_specs=[
            pl.BlockSpec((tm, D_MODEL), row),
            _const_spec((1, D_MODEL)),
            _const_spec((D_MODEL, OFF_END)),
            _const_spec((1, GATE_WIDTH)),
        ],
        out_specs=[
            pl.BlockSpec((tm, SSM_WIDTH), lambda b, j: (j, b)),
            pl.BlockSpec((tm, SWA_Q_WIDTH), row),
            pl.BlockSpec((tm, SWA_KV_WIDTH), row),
            pl.BlockSpec((tm, SWA_KV_WIDTH), row),
            pl.BlockSpec((tm, MEM_WIDTH), row),
            pl.BlockSpec((tm, GATE_WIDTH), row),
        ],
        out_shape=[
            jax.ShapeDtypeStruct((seq, bsz * SSM_WIDTH), BF16),
            jax.ShapeDtypeStruct((t, SWA_Q_WIDTH), BF16),
            jax.ShapeDtypeStruct((t, SWA_KV_WIDTH), BF16),
            jax.ShapeDtypeStruct((t, SWA_KV_WIDTH), BF16),
            jax.ShapeDtypeStruct((t, MEM_WIDTH), BF16),
            jax.ShapeDtypeStruct((t, GATE_WIDTH), BF16),
        ],
        compiler_params=_cparams(("parallel", "parallel")),
    )(x2, g, w_in_b, b_gate)


def _gelu_tanh(x):
    c = math.sqrt(2.0 / math.pi)
    return 0.5 * x * (1.0 + jnp.tanh(c * (x + 0.044715 * (x * x * x))))


def _s5_kernel(u_ref, bmat_ref, are_ref, aim_ref, cmat_ref, d_ref, wv_ref, wg_ref, o_ref,
               st_ref, bu_ref, *, bsz, steps, col_chunk):
    nc = SSM_COMPLEX

    @pl.when(pl.program_id(0) == 0)
    def _():
        st_ref[...] = jnp.zeros_like(st_ref)

    u = u_ref[...]
    for lo in range(0, 2 * nc, 1024):
        bu_ref[:, lo:lo + 1024] = jnp.dot(u, bmat_ref[:, lo:lo + 1024], preferred_element_type=F32)

    for lo in range(0, nc, col_chunk):
        re = slice(lo, lo + col_chunk)
        im = slice(nc + lo, nc + lo + col_chunk)
        ar = jnp.broadcast_to(are_ref[:, re], (bsz, col_chunk))
        ai = jnp.broadcast_to(aim_ref[:, re], (bsz, col_chunk))

        def step(t, carry):
            sr, si = carry
            rows = pl.ds(pl.multiple_of(t * bsz, bsz), bsz)
            nr = ar * sr - ai * si + bu_ref[rows, re]
            ni = ar * si + ai * sr + bu_ref[rows, im]
            bu_ref[rows, re] = nr
            bu_ref[rows, im] = ni
            return nr, ni

        sr, si = lax.fori_loop(0, steps, step, (st_ref[:, re], st_ref[:, im]))
        st_ref[:, re] = sr
        st_ref[:, im] = si

    y = jnp.dot(bu_ref[...].astype(BF16), cmat_ref[...], preferred_element_type=F32)
    y = y + d_ref[...] * u.astype(F32)
    yb = _gelu_tanh(y).astype(BF16)
    val = jnp.dot(yb, wv_ref[...], preferred_element_type=F32)
    gate = jnp.dot(yb, wg_ref[...], preferred_element_type=F32)
    o_ref[...] = (val * _sigmoid(gate)).astype(BF16)


def _s5_branch(u_tm, bmat, a_re, a_im, cmat, d_skip, wv, wg, bsz, seq, steps):
    rows = steps * bsz
    kern = functools.partial(_s5_kernel, bsz=bsz, steps=steps, col_chunk=512)
    return pl.pallas_call(
        kern,
        grid=(seq // steps,),
        in_specs=[
            pl.BlockSpec((rows, SSM_WIDTH), lambda i: (i, 0)),
            _const_spec((SSM_WIDTH, 2 * SSM_COMPLEX)),
            _const_spec((1, SSM_COMPLEX)),
            _const_spec((1, SSM_COMPLEX)),
            _const_spec((2 * SSM_COMPLEX, SSM_WIDTH)),
            _const_spec((1, SSM_WIDTH)),
            _const_spec((SSM_WIDTH, SSM_WIDTH)),
            _const_spec((SSM_WIDTH, SSM_WIDTH)),
        ],
        out_specs=pl.BlockSpec((rows, SSM_WIDTH), lambda i: (i, 0)),
        out_shape=jax.ShapeDtypeStruct((seq * bsz, SSM_WIDTH), BF16),
        scratch_shapes=[
            pltpu.VMEM((bsz, 2 * SSM_COMPLEX), F32),
            pltpu.VMEM((rows, 2 * SSM_COMPLEX), F32),
        ],
        compiler_params=_cparams(("arbitrary",)),
    )(u_tm, bmat, a_re, a_im, cmat, d_skip, wv, wg)


def _s5_discretize(log_dt, a_re, a_im, b_re, b_im, c_re, c_im):
    dt = jnp.exp(log_dt.astype(F32))[:, None]
    lr = jnp.minimum(a_re.astype(F32), -1e-4)
    li = a_im.astype(F32)
    mag = jnp.exp(lr * dt)
    abr = mag * jnp.cos(li * dt)
    abi = mag * jnp.sin(li * dt)
    nr, ni = abr - 1.0, abi
    den = lr * lr + li * li
    fr = (nr * lr + ni * li) / den
    fi = (ni * lr - nr * li) / den
    br, bi = b_re.astype(F32), b_im.astype(F32)
    bbr = fr[..., None] * br - fi[..., None] * bi
    bbi = fr[..., None] * bi + fi[..., None] * br
    eye = jnp.eye(SSM_GROUPS, dtype=F32)

    def blockdiag_in(m):
        return jnp.einsum('gpc,gh->gchp', m, eye).reshape(SSM_WIDTH, SSM_COMPLEX)

    def blockdiag_out(m):
        return jnp.einsum('gcp,gh->gphc', m, eye).reshape(SSM_COMPLEX, SSM_WIDTH)

    bmat = jnp.concatenate([blockdiag_in(bbr), blockdiag_in(bbi)], axis=1).astype(BF16)
    cmat = jnp.concatenate([blockdiag_out(c_re.astype(F32)), -blockdiag_out(c_im.astype(F32))],
                           axis=0).astype(BF16)
    return bmat, abr.reshape(1, SSM_COMPLEX), abi.reshape(1, SSM_COMPLEX), cmat


def _memkv_kernel(mem_ref, g_ref, w_ref, kg_ref, mk_ref, mv_ref):
    x = mem_ref[...]
    h = x * lax.rsqrt(jnp.mean(x * x, axis=-1, keepdims=True) + NORM_EPS) * g_ref[...]
    kv = jnp.dot(h.astype(BF16), w_ref[...], preferred_element_type=F32)
    parts = []
    for hd in range(MEM_HEADS):
        kh = kv[:, hd * MEM_HEAD_DIM:(hd + 1) * MEM_HEAD_DIM]
        kh = kh * lax.rsqrt(jnp.mean(kh * kh, axis=-1, keepdims=True) + QK_EPS) * kg_ref[...]
        parts.append(kh)
    mk_ref[...] = jnp.concatenate(parts, axis=1).astype(BF16)
    mv_ref[...] = kv[:, MEM_WIDTH:].astype(BF16)


def _mem_kv(mem2, g, w_b, k_gain, bsz):
    return pl.pallas_call(
        _memkv_kernel,
        grid=(bsz,),
        in_specs=[
            pl.BlockSpec((MEM_LEN, D_MODEL), lambda b: (b, 0)),
            _const_spec((1, D_MODEL)),
            _const_spec((D_MODEL, 2 * MEM_WIDTH)),
            _const_spec((1, MEM_HEAD_DIM)),
        ],
        out_specs=[
            pl.BlockSpec((MEM_LEN, MEM_WIDTH), lambda b: (b, 0)),
            pl.BlockSpec((MEM_LEN, MEM_WIDTH), lambda b: (b, 0)),
        ],
        out_shape=[
            jax.ShapeDtypeStruct((bsz * MEM_LEN, MEM_WIDTH), BF16),
            jax.ShapeDtypeStruct((bsz * MEM_LEN, MEM_WIDTH), BF16),
        ],
        compiler_params=_cparams(("parallel",)),
    )(mem2, g, w_b, k_gain)


HEADS_PER_COL = LANES // HEAD_DIM
ROPE_CONST_ROWS = 16


def _split_bf16(a):
    hi = a.astype(BF16)
    lo = (a - hi.astype(F32)).astype(BF16)
    return hi, lo


def _rope_tables(pos_row, rc_ref, re_ref):
    tq = pos_row.shape[1]
    freq = jnp.concatenate([rc_ref[0:ROPE_HALF, :]] * (tq // LANES), axis=1)
    ang = freq * pos_row
    trig = jnp.concatenate([jnp.cos(ang), jnp.sin(ang),
                            jnp.zeros((LANES - 2 * ROPE_HALF, tq), F32)], axis=0)
    tab = jnp.dot(trig.T, re_ref[...], precision=lax.Precision.HIGHEST, preferred_element_type=F32)
    c = tab[:, 0:LANES] + rc_ref[ROPE_HALF:ROPE_HALF + 1, :]
    s_lo = tab[:, LANES:2 * LANES]
    s_hi = tab[:, 2 * LANES:3 * LANES]
    return c, s_lo, s_hi


def _attn_kernel(sink_ref, q_ref, k_ref, v_ref, pos_ref, qm_ref, mk_ref, mv_ref, gq_ref, gk_ref,
                 rc_ref, re_ref, seg_ref, gmq_ref, oswa_ref, omem_ref, kbuf_ref, vbuf_ref, *, tq):
    nblk = tq // BLOCK
    first_tile = pl.program_id(1) == 0
    dot = lambda a, b: jnp.dot(a, b, preferred_element_type=F32)

    @pl.when(first_tile)
    def _():
        kbuf_ref[:, 0:BLOCK, :] = jnp.zeros((kbuf_ref.shape[0], BLOCK, LANES), BF16)
        vbuf_ref[:, 0:BLOCK, :] = jnp.zeros((vbuf_ref.shape[0], BLOCK, LANES), BF16)

    c, s_lo, s_hi = _rope_tables(pos_ref[0:1, :], rc_ref, re_ref)
    seg = seg_ref[...]
    lo_half = lax.broadcasted_iota(jnp.int32, (tq, LANES), 1) < HEAD_DIM
    lo_half_blk = lax.broadcasted_iota(jnp.int32, (BLOCK, LANES), 1) < HEAD_DIM

    def norm_rope(raw, gain):
        hi, lo = _split_bf16(raw * raw)
        rs = lax.rsqrt((dot(hi, seg) + dot(lo, seg)) * (1.0 / HEAD_DIM) + QK_EPS)
        x = raw * gain
        rot = x * c + pltpu.roll(x, LANES - ROPE_HALF, 1) * s_lo + pltpu.roll(x, ROPE_HALF, 1) * s_hi
        return rot * rs

    for col in range(SWA_KV_WIDTH // LANES):
        cs = slice(col * LANES, (col + 1) * LANES)
        kc = norm_rope(k_ref[:, cs].astype(F32), gk_ref[:, cs])
        kbuf_ref[2 * col, BLOCK:BLOCK + tq, :] = kc.astype(BF16)
        kbuf_ref[2 * col + 1, BLOCK:BLOCK + tq, :] = pltpu.roll(kc, HEAD_DIM, 1).astype(BF16)
        vbuf_ref[2 * col, BLOCK:BLOCK + tq, :] = v_ref[:, cs]
        vbuf_ref[2 * col + 1, BLOCK:BLOCK + tq, :] = pltpu.roll(v_ref[:, cs].astype(F32), HEAD_DIM, 1).astype(BF16)

    q_half = []
    for col in range(SWA_Q_WIDTH // LANES):
        cs = slice(col * LANES, (col + 1) * LANES)
        qc = norm_rope(q_ref[:, cs].astype(F32), gq_ref[:, cs]) * (HEAD_DIM ** -0.5)
        q_half.append((jnp.where(lo_half, qc, 0.0).astype(BF16), jnp.where(lo_half, 0.0, qc).astype(BF16)))

    rows = 2 * BLOCK
    qi = lax.broadcasted_iota(jnp.int32, (rows, 2 * BLOCK), 0) & (BLOCK - 1)
    kj = lax.broadcasted_iota(jnp.int32, (rows, 2 * BLOCK), 1)
    in_cur = kj >= BLOCK
    cur_ok = in_cur & ((kj - BLOCK) <= qi)
    bias = jnp.where(cur_ok | (jnp.logical_not(in_cur) & (kj > qi)), 0.0, NEG_BIG)
    first_off = jnp.where(first_tile, BLOCK, 0)
    bias_first = jnp.where(cur_ok | (jnp.logical_not(in_cur) & (kj > qi + first_off)), 0.0, NEG_BIG)

    for n in range(nblk):
        r0 = n * BLOCK
        blk_bias = bias_first if n == 0 else bias
        for h in range(SWA_KV_HEADS):
            by_half = []
            for half in range(HEADS_PER_COL):
                ver = 2 * (h // HEADS_PER_COL) + (0 if half == h % HEADS_PER_COL else 1)
                qs = jnp.concatenate([q_half[2 * h][half][r0:r0 + BLOCK],
                                      q_half[2 * h + 1][half][r0:r0 + BLOCK]], axis=0)
                s = lax.dot_general(qs, kbuf_ref[ver, r0:r0 + 2 * BLOCK, :], (((1,), (1,)), ((), ())),
                                    preferred_element_type=F32) + blk_bias
                sink = jnp.concatenate(
                    [jnp.full((BLOCK, 1), sink_ref[h * SWA_GROUP + half], F32),
                     jnp.full((BLOCK, 1), sink_ref[h * SWA_GROUP + half + HEADS_PER_COL], F32)], axis=0)
                m = jnp.maximum(jnp.max(s, axis=-1, keepdims=True), sink)
                p = jnp.exp(s - m)
                den = jnp.sum(p, axis=-1, keepdims=True) + jnp.exp(sink - m)
                by_half.append(dot(p.astype(BF16), vbuf_ref[ver, r0:r0 + 2 * BLOCK, :]) / den)
            for sub in range(2):
                rs_ = slice(sub * BLOCK, (sub + 1) * BLOCK)
                colv = jnp.where(lo_half_blk, by_half[0][rs_], by_half[1][rs_])
                oswa_ref[r0:r0 + BLOCK, (2 * h + sub) * LANES:(2 * h + sub + 1) * LANES] = colv.astype(BF16)

    kbuf_ref[:, 0:BLOCK, :] = kbuf_ref[:, tq:tq + BLOCK, :]
    vbuf_ref[:, 0:BLOCK, :] = vbuf_ref[:, tq:tq + BLOCK, :]

    def _head_rs(x, eps):
        return lax.rsqrt(jnp.mean(x * x, axis=-1, keepdims=True) + eps)

    qm = qm_ref[...].astype(F32)
    outs = []
    for hd in range(MEM_HEADS):
        sl = slice(hd * MEM_HEAD_DIM, (hd + 1) * MEM_HEAD_DIM)
        qh = qm[:, sl]
        qh = qh * (_head_rs(qh, QK_EPS) * (MEM_HEAD_DIM ** -0.5)) * gmq_ref[...]
        s = lax.dot_general(qh.astype(BF16), mk_ref[:, sl], (((1,), (1,)), ((), ())),
                            preferred_element_type=F32)
        m = jnp.max(s, axis=-1, keepdims=True)
        p = jnp.exp(s - m)
        den = jnp.sum(p, axis=-1, keepdims=True)
        outs.append(jnp.dot(p.astype(BF16), mv_ref[:, sl], preferred_element_type=F32) / den)
    omem_ref[...] = jnp.concatenate(outs, axis=1).astype(BF16)


def _rope_constants():
    inv_freq = ROPE_THETA ** (-jnp.arange(ROPE_HALF, dtype=F32) / ROPE_HALF)
    lane = jnp.arange(LANES) % HEAD_DIM
    j = jnp.arange(ROPE_HALF)[:, None]
    e_cos = ((lane[None, :] < ROPE_DIM) & (lane[None, :] % ROPE_HALF == j)).astype(F32)
    e_lo = -(lane[None, :] == j).astype(F32)
    e_hi = (lane[None, :] == j + ROPE_HALF).astype(F32)
    ones = (lane >= ROPE_DIM).astype(F32)[None, :]
    rows = jnp.concatenate([jnp.broadcast_to(inv_freq[:, None], (ROPE_HALF, LANES)), ones,
                            jnp.zeros((ROPE_CONST_ROWS - ROPE_HALF - 1, LANES), F32)], axis=0)
    zero8 = jnp.zeros((ROPE_HALF, LANES), F32)
    expand = jnp.concatenate([
        jnp.concatenate([e_cos, zero8, zero8], axis=1),
        jnp.concatenate([zero8, e_lo, e_hi], axis=1),
        jnp.zeros((LANES - 2 * ROPE_HALF, 3 * LANES), F32)], axis=0)
    return rows, expand


def _attention(sinks, q, k, v, pos_rows, qm, mk, mv, gq, gk, gmq, bsz, seq, tq):
    t = bsz * seq
    nj = seq // tq
    row = lambda b, j: (b * nj + j, 0)
    mem = lambda b, j: (b, 0)
    half = jnp.arange(LANES) // HEAD_DIM
    seg = (half[:, None] == half[None, :]).astype(BF16)
    kern = functools.partial(_attn_kernel, tq=tq)
    nver = 2 * SWA_KV_WIDTH // LANES
    return pl.pallas_call(
        kern,
        grid=(bsz, nj),
        in_specs=[
            pl.BlockSpec(memory_space=pltpu.SMEM),
            pl.BlockSpec((tq, SWA_Q_WIDTH), row),
            pl.BlockSpec((tq, SWA_KV_WIDTH), row),
            pl.BlockSpec((tq, SWA_KV_WIDTH), row),
            pl.BlockSpec((SUBLANES, tq), row),
            pl.BlockSpec((tq, MEM_WIDTH), row),
            pl.BlockSpec((MEM_LEN, MEM_WIDTH), mem),
            pl.BlockSpec((MEM_LEN, MEM_WIDTH), mem),
            _const_spec((1, SWA_Q_WIDTH)),
            _const_spec((1, SWA_KV_WIDTH)),
            _const_spec((ROPE_CONST_ROWS, LANES)),
            _const_spec((LANES, 3 * LANES)),
            _const_spec((LANES, LANES)),
            _const_spec((1, MEM_HEAD_DIM)),
        ],
        out_specs=[
            pl.BlockSpec((tq, SWA_Q_WIDTH), row),
            pl.BlockSpec((tq, MEM_WIDTH), row),
        ],
        out_shape=[
            jax.ShapeDtypeStruct((t, SWA_Q_WIDTH), BF16),
            jax.ShapeDtypeStruct((t, MEM_WIDTH), BF16),
        ],
        scratch_shapes=[
            pltpu.VMEM((nver, BLOCK + tq, LANES), BF16),
            pltpu.VMEM((nver, BLOCK + tq, LANES), BF16),
        ],
        compiler_params=_cparams(("parallel", "arbitrary")),
    )(sinks, q, k, v, pos_rows, qm, mk, mv, gq, gk, *_rope_constants(), seg, gmq)


def _merge_kernel(x_ref, ys_ref, yw_ref, ym_ref, gate_ref, w0_ref, w1_ref, w2_ref, wo_ref, gf_ref,
                  wr_hi_ref, wr_lo_ref, br_ref, x1_ref, tsl_ref, idx_ref, wt_ref, *, tm):
    gates = gate_ref[...]
    dot = lambda a, b: jnp.dot(a, b, preferred_element_type=F32)
    merged = (gates[:, 0:D_MODEL].astype(F32) * dot(ys_ref[...], w0_ref[...])
              + gates[:, D_MODEL:2 * D_MODEL].astype(F32) * dot(yw_ref[...], w1_ref[...])
              + gates[:, 2 * D_MODEL:].astype(F32) * dot(ym_ref[...], w2_ref[...]))
    x1 = x_ref[...] + dot(merged.astype(BF16), wo_ref[...])
    x1_ref[...] = x1
    t = x1 * lax.rsqrt(jnp.mean(x1 * x1, axis=-1, keepdims=True) + NORM_EPS) * gf_ref[...]

    for j in range(ROW_SUBLANES):
        tsl_ref[pl.ds(j, tm, stride=ROW_SUBLANES), :] = t[:, j * LANES:(j + 1) * LANES]

    t_hi, t_lo = _split_bf16(t)
    logits = (dot(t_hi, wr_hi_ref[...]) + dot(t_hi, wr_lo_ref[...]) + dot(t_lo, wr_hi_ref[...])
              + br_ref[...])
    lane = lax.broadcasted_iota(jnp.int32, logits.shape, 1).astype(F32)
    work = logits
    vals, firsts = [], []
    for _ in range(TOP_K):
        m = jnp.max(work, axis=-1, keepdims=True)
        first = jnp.min(jnp.where(work == m, lane, float(LANES)), axis=-1, keepdims=True)
        work = jnp.where(lane == first, NEG_BIG * 2.0, work)
        vals.append(m)
        firsts.append(first)
    exps = [jnp.exp(v - vals[0]) for v in vals]
    den = exps[0] + exps[1] + exps[2] + exps[3]
    idx_out = jnp.zeros_like(logits)
    wt_out = jnp.zeros_like(logits)
    for k in range(TOP_K):
        idx_out = jnp.where(lane == float(k), firsts[k], idx_out)
        wt_out = jnp.where(lane == float(k), exps[k] / den, wt_out)
    idx_ref[...] = idx_out.astype(jnp.int32)
    wt_ref[...] = wt_out


def _merge(x2, y_ssm_tm, y_swa, y_mem, gates, w0, w1, w2, wo, gf, wr_hi, wr_lo, br, bsz, seq, tm):
    t = bsz * seq
    nj = seq // tm
    row = lambda b, j: (b * nj + j, 0)
    return pl.pallas_call(
        functools.partial(_merge_kernel, tm=tm),
        grid=(bsz, nj),
        in_specs=[
            pl.BlockSpec((tm, D_MODEL), row),
            pl.BlockSpec((tm, SSM_WIDTH), lambda b, j: (j, b)),
            pl.BlockSpec((tm, SWA_Q_WIDTH), row),
            pl.BlockSpec((tm, MEM_WIDTH), row),
            pl.BlockSpec((tm, GATE_WIDTH), row),
            _const_spec((SSM_WIDTH, D_MODEL)),
            _const_spec((SWA_Q_WIDTH, D_MODEL)),
            _const_spec((MEM_WIDTH, D_MODEL)),
            _const_spec((D_MODEL, D_MODEL)),
            _const_spec((1, D_MODEL)),
            _const_spec((D_MODEL, LANES)),
            _const_spec((D_MODEL, LANES)),
            _const_spec((1, LANES)),
        ],
        out_specs=[
            pl.BlockSpec((tm, D_MODEL), row),
            pl.BlockSpec((tm * ROW_SUBLANES, LANES), row),
            pl.BlockSpec((tm, LANES), row),
            pl.BlockSpec((tm, LANES), row),
        ],
        out_shape=[
            jax.ShapeDtypeStruct((t, D_MODEL), F32),
            jax.ShapeDtypeStruct((t * ROW_SUBLANES, LANES), F32),
            jax.ShapeDtypeStruct((t, LANES), jnp.int32),
            jax.ShapeDtypeStruct((t, LANES), F32),
        ],
        compiler_params=_cparams(("parallel", "parallel")),
    )(x2, y_ssm_tm, y_swa, y_mem, gates, w0, w1, w2, wo, gf, wr_hi, wr_lo, br)


DEINT_BLOCK = 2 * LANES


def _expert_prep_kernel(w1_ref, perm_ref, wg_ref, wl_ref):
    perm = perm_ref[...]
    for blk in range(2 * D_EXPERT // DEINT_BLOCK):
        cols = w1_ref[0, :, blk * DEINT_BLOCK:(blk + 1) * DEINT_BLOCK].astype(BF16)
        z = jnp.dot(cols, perm, preferred_element_type=F32).astype(BF16)
        wg_ref[0, :, blk * LANES:(blk + 1) * LANES] = z[:, :LANES]
        wl_ref[0, :, blk * LANES:(blk + 1) * LANES] = z[:, LANES:]


def _expert_prep(w1):
    src = jnp.arange(DEINT_BLOCK)
    dst = jnp.where(src % 2 == 0, src // 2, LANES + src // 2)
    perm = (dst[:, None] == jnp.arange(DEINT_BLOCK)[None, :]).astype(BF16)
    exp3 = lambda e: (e, 0, 0)
    return pl.pallas_call(
        _expert_prep_kernel,
        grid=(N_EXPERTS,),
        in_specs=[pl.BlockSpec((1, D_MODEL, 2 * D_EXPERT), exp3), _const_spec((DEINT_BLOCK, DEINT_BLOCK))],
        out_specs=[pl.BlockSpec((1, D_MODEL, D_EXPERT), exp3), pl.BlockSpec((1, D_MODEL, D_EXPERT), exp3)],
        out_shape=[jax.ShapeDtypeStruct((N_EXPERTS, D_MODEL, D_EXPERT), BF16)] * 2,
        compiler_params=_cparams(("parallel",)),
    )(w1, perm)


SCATTER_BATCH = 8


def _moe_kernel(cnt_ref, off_ref, tok_ref, wt_ref, src_ref, wg_ref, wl_ref, bg_ref, bl_ref, w2_ref,
                b2_ref, out_hbm, acc_ref, gbuf_ref, ybuf_ref, sem, *, nb_tokens, tile):
    blk = pl.program_id(0)
    e = pl.program_id(1)
    acc_rows = nb_tokens * ROW_SUBLANES

    @pl.when(e == 0)
    def _():
        acc_ref[...] = jnp.zeros_like(acc_ref)

    n = cnt_ref[blk * N_EXPERTS + e]
    start = off_ref[blk * N_EXPERTS + e]

    def tile_body(ti, carry):
        base = start + ti * tile
        nvalid = n - ti * tile
        for i in range(tile):
            tok = tok_ref[0, 0, base + i]
            gbuf_ref[pl.ds(i * ROW_SUBLANES, ROW_SUBLANES), :] = (
                src_ref[pl.ds(pl.multiple_of(tok * ROW_SUBLANES, ROW_SUBLANES), ROW_SUBLANES), :])
        x = jnp.concatenate(
            [gbuf_ref[pl.ds(j, tile, stride=ROW_SUBLANES), :] for j in range(ROW_SUBLANES)],
            axis=1).astype(BF16)
        hg = jnp.dot(x, wg_ref[0], preferred_element_type=F32) + bg_ref[0]
        hl = jnp.dot(x, wl_ref[0], preferred_element_type=F32) + bl_ref[0]
        xg = jnp.minimum(hg, SWIGLU_LIMIT)
        xl = jnp.clip(hl, -SWIGLU_LIMIT, SWIGLU_LIMIT)
        act = xg * _sigmoid(SWIGLU_ALPHA * xg) * (xl + 1.0)
        y = jnp.dot(act.astype(BF16), w2_ref[0], preferred_element_type=F32) + b2_ref[0]
        for j in range(ROW_SUBLANES):
            ybuf_ref[pl.ds(j, tile, stride=ROW_SUBLANES), :] = y[:, j * LANES:(j + 1) * LANES]
        for i0 in range(0, tile, SCATTER_BATCH):
            sums, dsts = [], []
            for i in range(i0, i0 + SCATTER_BATCH):
                live = i < nvalid
                tok = jnp.where(live, tok_ref[0, 0, base + i], nb_tokens)
                w = jnp.where(live, wt_ref[0, 0, base + i], 0.0)
                dst = pl.ds(pl.multiple_of(tok * ROW_SUBLANES, ROW_SUBLANES), ROW_SUBLANES)
                sums.append(acc_ref[dst, :] + w * ybuf_ref[pl.ds(i * ROW_SUBLANES, ROW_SUBLANES), :])
                dsts.append(dst)
            for dst, s in zip(dsts, sums):
                acc_ref[dst, :] = s
        return carry

    lax.fori_loop(0, (n + tile - 1) // tile, tile_body, 0)

    @pl.when(e == N_EXPERTS - 1)
    def _():
        cp = pltpu.make_async_copy(acc_ref.at[pl.ds(0, acc_rows), :],
                                   out_hbm.at[pl.ds(blk * acc_rows, acc_rows), :], sem)
        cp.start()
        cp.wait()


def _moe(counts, offsets, tok_sorted, wt_sorted, tsl, wg, wl, bg, bl, w2, b2, t, nb_tokens, tile):
    nblk = t // nb_tokens
    slots = tok_sorted.shape[-1]
    exp3 = lambda b, e, *_: (e, 0, 0)
    blk3 = lambda b, e, *_: (b, 0, 0)
    grid_spec = pltpu.PrefetchScalarGridSpec(
        num_scalar_prefetch=2,
        grid=(nblk, N_EXPERTS),
        in_specs=[
            pl.BlockSpec((1, 1, slots), blk3, memory_space=pltpu.SMEM),
            pl.BlockSpec((1, 1, slots), blk3, memory_space=pltpu.SMEM),
            pl.BlockSpec((nb_tokens * ROW_SUBLANES, LANES), lambda b, e, *_: (b, 0),
                         pipeline_mode=pl.Buffered(1)),
            pl.BlockSpec((1, D_MODEL, D_EXPERT), exp3),
            pl.BlockSpec((1, D_MODEL, D_EXPERT), exp3),
            pl.BlockSpec((1, 1, D_EXPERT), exp3),
            pl.BlockSpec((1, 1, D_EXPERT), exp3),
            pl.BlockSpec((1, D_EXPERT, D_MODEL), exp3),
            pl.BlockSpec((1, 1, D_MODEL), exp3),
        ],
        out_specs=pl.BlockSpec(memory_space=pl.ANY),
        scratch_shapes=[
            pltpu.VMEM(((nb_tokens + 1) * ROW_SUBLANES, LANES), F32),
            pltpu.VMEM((tile * ROW_SUBLANES, LANES), F32),
            pltpu.VMEM((tile * ROW_SUBLANES, LANES), F32),
            pltpu.SemaphoreType.DMA,
        ],
    )
    return pl.pallas_call(
        functools.partial(_moe_kernel, nb_tokens=nb_tokens, tile=tile),
        grid_spec=grid_spec,
        out_shape=jax.ShapeDtypeStruct((t * ROW_SUBLANES, LANES), F32),
        compiler_params=_cparams(("arbitrary", "arbitrary")),
    )(counts, offsets, tok_sorted, wt_sorted, tsl, wg, wl, bg, bl, w2, b2)


def _route(idx, wts, t, nb_tokens, tile):
    nblk = t // nb_tokens
    eid = idx[:, :TOP_K].reshape(nblk, nb_tokens * TOP_K)
    w = wts[:, :TOP_K].reshape(nblk, nb_tokens * TOP_K)
    order = jnp.argsort(eid, axis=1, stable=True)
    tok_sorted = (order // TOP_K).astype(jnp.int32)
    wt_sorted = jnp.take_along_axis(w, order, axis=1)
    counts = jnp.sum((eid[:, :, None] == jnp.arange(N_EXPERTS)[None, None, :]).astype(jnp.int32), axis=1)
    offsets = jnp.cumsum(counts, axis=1) - counts
    pad = ((0, 0), (0, tile))
    tok_sorted = jnp.pad(tok_sorted, pad)[:, None, :]
    wt_sorted = jnp.pad(wt_sorted, pad)[:, None, :]
    return counts.reshape(-1), offsets.reshape(-1).astype(jnp.int32), tok_sorted, wt_sorted


def _residual_kernel(x1_ref, moe_ref, o_ref, *, tm):
    parts = [moe_ref[pl.ds(j, tm, stride=ROW_SUBLANES), :] for j in range(ROW_SUBLANES)]
    o_ref[...] = x1_ref[...] + jnp.concatenate(parts, axis=1)


def _residual(x1, moe, tm):
    t = x1.shape[0]
    return pl.pallas_call(
        functools.partial(_residual_kernel, tm=tm),
        grid=(t // tm,),
        in_specs=[pl.BlockSpec((tm, D_MODEL), lambda i: (i, 0)),
                  pl.BlockSpec((tm * ROW_SUBLANES, LANES), lambda i: (i, 0))],
        out_specs=pl.BlockSpec((tm, D_MODEL), lambda i: (i, 0)),
        out_shape=jax.ShapeDtypeStruct((t, D_MODEL), F32),
        compiler_params=_cparams(("parallel",)),
    )(x1, moe)


def kernel(x, mem, positions, attn_norm_g, mem_norm_g, w_in, b_gate, ssm_log_dt, ssm_a_re, ssm_a_im,
           ssm_b_re, ssm_b_im, ssm_c_re, ssm_c_im, ssm_d, w_glu_v, w_glu_g, swa_q_gain, swa_k_gain,
           swa_sinks, w_mem_kv, mem_q_gain, mem_k_gain, w_br_ssm, w_br_swa, w_br_mem, w_out,
           ffn_norm_g, w_router, b_router, w_mlp1, b_mlp1, w_mlp2, b_mlp2):
    bsz, seq, _ = x.shape
    depth = w_in.shape[0]
    t = bsz * seq
    tm = min(512, seq)
    tq = min(512, seq)
    steps = min(32, seq)

    pos_rows = jnp.broadcast_to(positions.astype(F32).reshape(t // tq, 1, tq),
                                (t // tq, SUBLANES, tq)).reshape(t // tq * SUBLANES, tq)
    mem2 = mem.reshape(bsz * MEM_LEN, D_MODEL)

    x2 = x.reshape(t, D_MODEL)
    for l in range(depth):
        row = lambda a: a[l].reshape(1, -1).astype(F32)
        u_tm, q, k, v, qm, gates = _in_projection(
            x2, row(attn_norm_g), w_in[l].astype(BF16), row(b_gate), bsz, seq, tm)

        bmat, abr, abi, cmat = _s5_discretize(ssm_log_dt[l], ssm_a_re[l], ssm_a_im[l], ssm_b_re[l],
                                              ssm_b_im[l], ssm_c_re[l], ssm_c_im[l])
        y_ssm_tm = _s5_branch(u_tm.reshape(seq * bsz, SSM_WIDTH), bmat, abr, abi, cmat, row(ssm_d),
                              w_glu_v[l].astype(BF16), w_glu_g[l].astype(BF16), bsz, seq, steps)
        y_ssm_tm = y_ssm_tm.reshape(seq, bsz * SSM_WIDTH)

        mk, mv = _mem_kv(mem2, row(mem_norm_g), w_mem_kv[l].astype(BF16), row(mem_k_gain), bsz)
        y_swa, y_mem = _attention(
            swa_sinks[l].astype(F32), q, k, v, pos_rows, qm, mk, mv,
            jnp.tile(row(swa_q_gain), (1, SWA_Q_HEADS)), jnp.tile(row(swa_k_gain), (1, SWA_KV_HEADS)),
            row(mem_q_gain), bsz, seq, tq)

        wr = jnp.pad(w_router[l].astype(F32), ((0, 0), (0, LANES - N_EXPERTS)))
        wr_hi = wr.astype(BF16)
        wr_lo = (wr - wr_hi.astype(F32)).astype(BF16)
        br = jnp.pad(b_router[l].astype(F32), (0, LANES - N_EXPERTS), constant_values=NEG_BIG).reshape(1, LANES)
        x1, tsl, idx, wts = _merge(x2, y_ssm_tm, y_swa, y_mem, gates, w_br_ssm[l].astype(BF16),
                                   w_br_swa[l].astype(BF16), w_br_mem[l].astype(BF16),
                                   w_out[l].astype(BF16), row(ffn_norm_g), wr_hi, wr_lo, br, bsz, seq, tm)

        nb_tokens = min(MOE_TOKEN_BLOCK, t)
        counts, offsets, tok_sorted, wt_sorted = _route(idx, wts, t, nb_tokens, MOE_TILE)
        wg, wl = _expert_prep(w_mlp1[l])
        b1 = b_mlp1[l].astype(F32)
        moe = _moe(counts, offsets, tok_sorted, wt_sorted, tsl, wg, wl, b1[:, None, 0::2],
                   b1[:, None, 1::2], w_mlp2[l].astype(BF16), b_mlp2[l].astype(F32)[:, None, :],
                   t, nb_tokens, MOE_TILE)
        x2 = _residual(x1, moe, tm)
    return x2.reshape(bsz, seq, D_MODEL)
```

```python
import functools
import math

import jax
import jax.numpy as jnp
from jax import lax
from jax.experimental import pallas as pl
from jax.experimental.pallas import tpu as pltpu

F32 = jnp.float32
BF16 = jnp.bfloat16

D_MODEL = 1024
MEM_LEN = 256
NORM_EPS = 1e-5
QK_EPS = 1e-6

SSM_WIDTH = 512
SSM_GROUP = 16
SSM_GROUPS = 32
SSM_STATE = 64
SSM_COMPLEX = SSM_GROUPS * SSM_STATE
S5_SLABS = SSM_WIDTH // 128
S5_SLAB_STATES = SSM_COMPLEX // S5_SLABS

HEAD_DIM = 64
SWA_Q_HEADS = 16
SWA_KV_HEADS = 4
SWA_GROUP = 4
BLOCK = 128
ROPE_THETA = 500000.0
ROPE_DIM = 16
ROPE_HALF = 8

MEM_HEADS = 4
MEM_HEAD_DIM = 128

SWA_Q_WIDTH = 1024
SWA_KV_WIDTH = 256
MEM_WIDTH = 512
N_BRANCH = 3
GATE_WIDTH = N_BRANCH * D_MODEL
OFF_U, OFF_Q, OFF_K, OFF_V, OFF_QM, OFF_G, OFF_END = 0, 512, 1536, 1792, 2048, 2560, 5632

N_EXPERTS = 32
TOP_K = 4
D_EXPERT = 1024
SWIGLU_ALPHA = 1.702
SWIGLU_LIMIT = 7.0

LANES = 128
SUBLANES = 8
VMEM_LIMIT = 56 * 1024 * 1024
NEG_BIG = -1e30

ROW_SUBLANES = D_MODEL // LANES
MOE_TOKEN_BLOCK = 4096
MOE_TILE = 272


def _sigmoid(x):
    return 1.0 / (1.0 + jnp.exp(-x))


def _cparams(sem):
    return pltpu.CompilerParams(dimension_semantics=sem, vmem_limit_bytes=VMEM_LIMIT)


def _const_spec(shape):
    nd = len(shape)
    return pl.BlockSpec(shape, lambda *_: (0,) * nd)


def _inproj_kernel(x_ref, g_ref, w_ref, bg_ref, u_ref, q_ref, k_ref, v_ref, qm_ref, gate_ref):
    x = x_ref[...]
    h = x * lax.rsqrt(jnp.mean(x * x, axis=-1, keepdims=True) + NORM_EPS) * g_ref[...]
    hb = h.astype(BF16)

    def proj(lo, hi):
        return jnp.dot(hb, w_ref[:, lo:hi], preferred_element_type=F32)

    u_ref[...] = proj(OFF_U, OFF_Q).astype(BF16)
    q_ref[...] = proj(OFF_Q, OFF_K).astype(BF16)
    k_ref[...] = proj(OFF_K, OFF_V).astype(BF16)
    v_ref[...] = proj(OFF_V, OFF_QM).astype(BF16)
    qm_ref[...] = proj(OFF_QM, OFF_G).astype(BF16)
    gate_ref[...] = _sigmoid(proj(OFF_G, OFF_END) + bg_ref[...]).astype(BF16)


def _in_projection(x2, g, w_in_b, b_gate, bsz, seq, tm):
    t = bsz * seq
    nj = seq // tm
    row = lambda b, j: (b * nj + j, 0)
    return pl.pallas_call(
        _inproj_kernel,
        grid=(bsz, nj),
        in_specs=[
            pl.BlockSpec((tm, D_MODEL), row),
            _const_spec((1, D_MODEL)),
            _const_spec((D_MODEL, OFF_END)),
            _const_spec((1, GATE_WIDTH)),
        ],
        out_specs=[
            pl.BlockSpec((tm, SSM_WIDTH), lambda b, j: (j, b)),
            pl.BlockSpec((tm, SWA_Q_WIDTH), row),
            pl.BlockSpec((tm, SWA_KV_WIDTH), row),
            pl.BlockSpec((tm, SWA_KV_WIDTH), row),
            pl.BlockSpec((tm, MEM_WIDTH), row),
            pl.BlockSpec((tm, GATE_WIDTH), row),
        ],
        out_shape=[
            jax.ShapeDtypeStruct((seq, bsz * SSM_WIDTH), BF16),
            jax.ShapeDtypeStruct((t, SWA_Q_WIDTH), BF16),
            jax.ShapeDtypeStruct((t, SWA_KV_WIDTH), BF16),
            jax.ShapeDtypeStruct((t, SWA_KV_WIDTH), BF16),
            jax.ShapeDtypeStruct((t, MEM_WIDTH), BF16),
            jax.ShapeDtypeStruct((t, GATE_WIDTH), BF16),
        ],
        compiler_params=_cparams(("parallel", "parallel")),
    )(x2, g, w_in_b, b_gate)


def _gelu_tanh(x):
    c = math.sqrt(2.0 / math.pi)
    return 0.5 * x * (1.0 + jnp.tanh(c * (x + 0.044715 * (x * x * x))))


def _s5_kernel(u_ref, bmat_ref, are_ref, aim_ref, cmat_ref, d_ref, wv_ref, wg_ref, o_ref,
               st_ref, bu_ref, *, bsz, steps, col_chunk):
    nc = SSM_COMPLEX

    @pl.when(pl.program_id(0) == 0)
    def _():
        st_ref[...] = jnp.zeros_like(st_ref)

    u = u_ref[...]
    for m in range(S5_SLABS):
        um = u[:, m * LANES:(m + 1) * LANES]
        re = slice(m * S5_SLAB_STATES, (m + 1) * S5_SLAB_STATES)
        im = slice(nc + m * S5_SLAB_STATES, nc + (m + 1) * S5_SLAB_STATES)
        bu_ref[:, re] = jnp.dot(um, bmat_ref[m, :, :S5_SLAB_STATES], preferred_element_type=F32)
        bu_ref[:, im] = jnp.dot(um, bmat_ref[m, :, S5_SLAB_STATES:], preferred_element_type=F32)

    for lo in range(0, nc, col_chunk):
        re = slice(lo, lo + col_chunk)
        im = slice(nc + lo, nc + lo + col_chunk)
        ar = jnp.broadcast_to(are_ref[:, re], (bsz, col_chunk))
        ai = jnp.broadcast_to(aim_ref[:, re], (bsz, col_chunk))

        def step(t, carry):
            sr, si = carry
            rows = pl.ds(pl.multiple_of(t * bsz, bsz), bsz)
            nr = ar * sr - ai * si + bu_ref[rows, re]
            ni = ar * si + ai * sr + bu_ref[rows, im]
            bu_ref[rows, re] = nr
            bu_ref[rows, im] = ni
            return nr, ni

        sr, si = lax.fori_loop(0, steps, step, (st_ref[:, re], st_ref[:, im]))
        st_ref[:, re] = sr
        st_ref[:, im] = si

    ys = []
    for m in range(S5_SLABS):
        re = slice(m * S5_SLAB_STATES, (m + 1) * S5_SLAB_STATES)
        im = slice(nc + m * S5_SLAB_STATES, nc + (m + 1) * S5_SLAB_STATES)
        ys.append(jnp.dot(bu_ref[:, re].astype(BF16), cmat_ref[m, :S5_SLAB_STATES, :],
                          preferred_element_type=F32)
                  + jnp.dot(bu_ref[:, im].astype(BF16), cmat_ref[m, S5_SLAB_STATES:, :],
                            preferred_element_type=F32))
    y = jnp.concatenate(ys, axis=1) + d_ref[...] * u.astype(F32)
    yb = _gelu_tanh(y).astype(BF16)
    val = jnp.dot(yb, wv_ref[...], preferred_element_type=F32)
    gate = jnp.dot(yb, wg_ref[...], preferred_element_type=F32)
    o_ref[...] = (val * _sigmoid(gate)).astype(BF16)


def _s5_branch(u_tm, bmat, a_re, a_im, cmat, d_skip, wv, wg, bsz, seq, steps):
    rows = steps * bsz
    kern = functools.partial(_s5_kernel, bsz=bsz, steps=steps, col_chunk=512)
    return pl.pallas_call(
        kern,
        grid=(seq // steps,),
        in_specs=[
            pl.BlockSpec((rows, SSM_WIDTH), lambda i: (i, 0)),
            _const_spec((S5_SLABS, LANES, 2 * S5_SLAB_STATES)),
            _const_spec((1, SSM_COMPLEX)),
            _const_spec((1, SSM_COMPLEX)),
            _const_spec((S5_SLABS, 2 * S5_SLAB_STATES, LANES)),
            _const_spec((1, SSM_WIDTH)),
            _const_spec((SSM_WIDTH, SSM_WIDTH)),
            _const_spec((SSM_WIDTH, SSM_WIDTH)),
        ],
        out_specs=pl.BlockSpec((rows, SSM_WIDTH), lambda i: (i, 0)),
        out_shape=jax.ShapeDtypeStruct((seq * bsz, SSM_WIDTH), BF16),
        scratch_shapes=[
            pltpu.VMEM((bsz, 2 * SSM_COMPLEX), F32),
            pltpu.VMEM((rows, 2 * SSM_COMPLEX), F32),
        ],
        compiler_params=_cparams(("arbitrary",)),
    )(u_tm, bmat, a_re, a_im, cmat, d_skip, wv, wg)


def _s5_discretize(log_dt, a_re, a_im, b_re, b_im, c_re, c_im):
    dt = jnp.exp(log_dt.astype(F32))[:, None]
    lr = jnp.minimum(a_re.astype(F32), -1e-4)
    li = a_im.astype(F32)
    mag = jnp.exp(lr * dt)
    abr = mag * jnp.cos(li * dt)
    abi = mag * jnp.sin(li * dt)
    nr, ni = abr - 1.0, abi
    den = lr * lr + li * li
    fr = (nr * lr + ni * li) / den
    fi = (ni * lr - nr * li) / den
    br, bi = b_re.astype(F32), b_im.astype(F32)
    bbr = fr[..., None] * br - fi[..., None] * bi
    bbi = fr[..., None] * bi + fi[..., None] * br
    gps = SSM_GROUPS // S5_SLABS
    eye = jnp.eye(gps, dtype=F32)

    def blockdiag_in(m):
        m = m.reshape(S5_SLABS, gps, SSM_STATE, SSM_GROUP)
        return jnp.einsum('sgpc,gh->sgchp', m, eye).reshape(S5_SLABS, LANES, S5_SLAB_STATES)

    def blockdiag_out(m):
        m = m.reshape(S5_SLABS, gps, SSM_GROUP, SSM_STATE)
        return jnp.einsum('sgcp,gh->sgphc', m, eye).reshape(S5_SLABS, S5_SLAB_STATES, LANES)

    bmat = jnp.concatenate([blockdiag_in(bbr), blockdiag_in(bbi)], axis=2).astype(BF16)
    cmat = jnp.concatenate([blockdiag_out(c_re.astype(F32)), -blockdiag_out(c_im.astype(F32))],
                           axis=1).astype(BF16)
    return bmat, abr.reshape(1, SSM_COMPLEX), abi.reshape(1, SSM_COMPLEX), cmat


def _memkv_kernel(mem_ref, g_ref, w_ref, kg_ref, mk_ref, mv_ref):
    x = mem_ref[...]
    h = x * lax.rsqrt(jnp.mean(x * x, axis=-1, keepdims=True) + NORM_EPS) * g_ref[...]
    kv = jnp.dot(h.astype(BF16), w_ref[...], preferred_element_type=F32)
    parts = []
    for hd in range(MEM_HEADS):
        kh = kv[:, hd * MEM_HEAD_DIM:(hd + 1) * MEM_HEAD_DIM]
        kh = kh * lax.rsqrt(jnp.mean(kh * kh, axis=-1, keepdims=True) + QK_EPS) * kg_ref[...]
        parts.append(kh)
    mk_ref[...] = jnp.concatenate(parts, axis=1).astype(BF16)
    mv_ref[...] = kv[:, MEM_WIDTH:].astype(BF16)


def _mem_kv(mem2, g, w_b, k_gain, bsz):
    return pl.pallas_call(
        _memkv_kernel,
        grid=(bsz,),
        in_specs=[
            pl.BlockSpec((MEM_LEN, D_MODEL), lambda b: (b, 0)),
            _const_spec((1, D_MODEL)),
            _const_spec((D_MODEL, 2 * MEM_WIDTH)),
            _const_spec((1, MEM_HEAD_DIM)),
        ],
        out_specs=[
            pl.BlockSpec((MEM_LEN, MEM_WIDTH), lambda b: (b, 0)),
            pl.BlockSpec((MEM_LEN, MEM_WIDTH), lambda b: (b, 0)),
        ],
        out_shape=[
            jax.ShapeDtypeStruct((bsz * MEM_LEN, MEM_WIDTH), BF16),
            jax.ShapeDtypeStruct((bsz * MEM_LEN, MEM_WIDTH), BF16),
        ],
        compiler_params=_cparams(("parallel",)),
    )(mem2, g, w_b, k_gain)


HEADS_PER_COL = LANES // HEAD_DIM
ROPE_CONST_ROWS = 16


def _split_bf16(a):
    hi = a.astype(BF16)
    lo = (a - hi.astype(F32)).astype(BF16)
    return hi, lo


def _rope_tables(pos_row, rc_ref, re_ref):
    tq = pos_row.shape[1]
    freq = jnp.concatenate([rc_ref[0:ROPE_HALF, :]] * (tq // LANES), axis=1)
    ang = freq * pos_row
    trig = jnp.concatenate([jnp.cos(ang), jnp.sin(ang),
                            jnp.zeros((LANES - 2 * ROPE_HALF, tq), F32)], axis=0)
    tab = jnp.dot(trig.T, re_ref[...], precision=lax.Precision.HIGHEST, preferred_element_type=F32)
    c = tab[:, 0:LANES] + rc_ref[ROPE_HALF:ROPE_HALF + 1, :]
    s_lo = tab[:, LANES:2 * LANES]
    s_hi = tab[:, 2 * LANES:3 * LANES]
    return c, s_lo, s_hi


def _attn_kernel(sink_ref, q_ref, k_ref, v_ref, pos_ref, qm_ref, mk_ref, mv_ref, gq_ref, gk_ref,
                 rc_ref, re_ref, seg_ref, gmq_ref, oswa_ref, omem_ref, kbuf_ref, vbuf_ref, *, tq):
    nblk = tq // BLOCK
    first_tile = pl.program_id(1) == 0
    dot = lambda a, b: jnp.dot(a, b, preferred_element_type=F32)

    @pl.when(first_tile)
    def _():
        kbuf_ref[:, 0:BLOCK, :] = jnp.zeros((kbuf_ref.shape[0], BLOCK, LANES), BF16)
        vbuf_ref[:, 0:BLOCK, :] = jnp.zeros((vbuf_ref.shape[0], BLOCK, LANES), BF16)

    c, s_lo, s_hi = _rope_tables(pos_ref[0:1, :], rc_ref, re_ref)
    seg = seg_ref[...]
    lo_half = lax.broadcasted_iota(jnp.int32, (tq, LANES), 1) < HEAD_DIM
    lo_half_blk = lax.broadcasted_iota(jnp.int32, (BLOCK, LANES), 1) < HEAD_DIM

    def norm_rope(raw, gain):
        hi, lo = _split_bf16(raw * raw)
        rs = lax.rsqrt((dot(hi, seg) + dot(lo, seg)) * (1.0 / HEAD_DIM) + QK_EPS)
        x = raw * gain
        rot = x * c + pltpu.roll(x, LANES - ROPE_HALF, 1) * s_lo + pltpu.roll(x, ROPE_HALF, 1) * s_hi
        return rot * rs

    for col in range(SWA_KV_WIDTH // LANES):
        cs = slice(col * LANES, (col + 1) * LANES)
        kc = norm_rope(k_ref[:, cs].astype(F32), gk_ref[:, cs])
        kbuf_ref[2 * col, BLOCK:BLOCK + tq, :] = kc.astype(BF16)
        kbuf_ref[2 * col + 1, BLOCK:BLOCK + tq, :] = pltpu.roll(kc, HEAD_DIM, 1).astype(BF16)
        vbuf_ref[2 * col, BLOCK:BLOCK + tq, :] = v_ref[:, cs]
        vbuf_ref[2 * col + 1, BLOCK:BLOCK + tq, :] = pltpu.roll(v_ref[:, cs].astype(F32), HEAD_DIM, 1).astype(BF16)

    q_half = []
    for col in range(SWA_Q_WIDTH // LANES):
        cs = slice(col * LANES, (col + 1) * LANES)
        qc = norm_rope(q_ref[:, cs].astype(F32), gq_ref[:, cs]) * (HEAD_DIM ** -0.5)
        q_half.append((jnp.where(lo_half, qc, 0.0).astype(BF16), jnp.where(lo_half, 0.0, qc).astype(BF16)))

    rows = 2 * BLOCK
    qi = lax.broadcasted_iota(jnp.int32, (rows, 2 * BLOCK), 0) & (BLOCK - 1)
    kj = lax.broadcasted_iota(jnp.int32, (rows, 2 * BLOCK), 1)
    in_cur = kj >= BLOCK
    cur_ok = in_cur & ((kj - BLOCK) <= qi)
    bias = jnp.where(cur_ok | (jnp.logical_not(in_cur) & (kj > qi)), 0.0, NEG_BIG)
    first_off = jnp.where(first_tile, BLOCK, 0)
    bias_first = jnp.where(cur_ok | (jnp.logical_not(in_cur) & (kj > qi + first_off)), 0.0, NEG_BIG)

    for n in range(nblk):
        r0 = n * BLOCK
        blk_bias = bias_first if n == 0 else bias
        for h in range(SWA_KV_HEADS):
            by_half = []
            for half in range(HEADS_PER_COL):
                ver = 2 * (h // HEADS_PER_COL) + (0 if half == h % HEADS_PER_COL else 1)
                qs = jnp.concatenate([q_half[2 * h][half][r0:r0 + BLOCK],
                                      q_half[2 * h + 1][half][r0:r0 + BLOCK]], axis=0)
                s = lax.dot_general(qs, kbuf_ref[ver, r0:r0 + 2 * BLOCK, :], (((1,), (1,)), ((), ())),
                                    preferred_element_type=F32) + blk_bias
                sink = jnp.concatenate(
                    [jnp.full((BLOCK, 1), sink_ref[h * SWA_GROUP + half], F32),
                     jnp.full((BLOCK, 1), sink_ref[h * SWA_GROUP + half + HEADS_PER_COL], F32)], axis=0)
                m = jnp.maximum(jnp.max(s, axis=-1, keepdims=True), sink)
                p = jnp.exp(s - m)
                den = jnp.sum(p, axis=-1, keepdims=True) + jnp.exp(sink - m)
                by_half.append(dot(p.astype(BF16), vbuf_ref[ver, r0:r0 + 2 * BLOCK, :]) / den)
            for sub in range(2):
                rs_ = slice(sub * BLOCK, (sub + 1) * BLOCK)
                colv = jnp.where(lo_half_blk, by_half[0][rs_], by_half[1][rs_])
                oswa_ref[r0:r0 + BLOCK, (2 * h + sub) * LANES:(2 * h + sub + 1) * LANES] = colv.astype(BF16)

    kbuf_ref[:, 0:BLOCK, :] = kbuf_ref[:, tq:tq + BLOCK, :]
    vbuf_ref[:, 0:BLOCK, :] = vbuf_ref[:, tq:tq + BLOCK, :]

    def _head_rs(x, eps):
        return lax.rsqrt(jnp.mean(x * x, axis=-1, keepdims=True) + eps)

    qm = qm_ref[...].astype(F32)
    outs = []
    for hd in range(MEM_HEADS):
        sl = slice(hd * MEM_HEAD_DIM, (hd + 1) * MEM_HEAD_DIM)
        qh = qm[:, sl]
        qh = qh * (_head_rs(qh, QK_EPS) * (MEM_HEAD_DIM ** -0.5)) * gmq_ref[...]
        s = lax.dot_general(qh.astype(BF16), mk_ref[:, sl], (((1,), (1,)), ((), ())),
                            preferred_element_type=F32)
        m = jnp.max(s, axis=-1, keepdims=True)
        p = jnp.exp(s - m)
        den = jnp.sum(p, axis=-1, keepdims=True)
        outs.append(jnp.dot(p.astype(BF16), mv_ref[:, sl], preferred_element_type=F32) / den)
    omem_ref[...] = jnp.concatenate(outs, axis=1).astype(BF16)


def _rope_constants():
    inv_freq = ROPE_THETA ** (-jnp.arange(ROPE_HALF, dtype=F32) / ROPE_HALF)
    lane = jnp.arange(LANES) % HEAD_DIM
    j = jnp.arange(ROPE_HALF)[:, None]
    e_cos = ((lane[None, :] < ROPE_DIM) & (lane[None, :] % ROPE_HALF == j)).astype(F32)
    e_lo = -(lane[None, :] == j).astype(F32)
    e_hi = (lane[None, :] == j + ROPE_HALF).astype(F32)
    ones = (lane >= ROPE_DIM).astype(F32)[None, :]
    rows = jnp.concatenate([jnp.broadcast_to(inv_freq[:, None], (ROPE_HALF, LANES)), ones,
                            jnp.zeros((ROPE_CONST_ROWS - ROPE_HALF - 1, LANES), F32)], axis=0)
    zero8 = jnp.zeros((ROPE_HALF, LANES), F32)
    expand = jnp.concatenate([
        jnp.concatenate([e_cos, zero8, zero8], axis=1),
        jnp.concatenate([zero8, e_lo, e_hi], axis=1),
        jnp.zeros((LANES - 2 * ROPE_HALF, 3 * LANES), F32)], axis=0)
    return rows, expand


def _attention(sinks, q, k, v, pos_rows, qm, mk, mv, gq, gk, gmq, bsz, seq, tq):
    t = bsz * seq
    nj = seq // tq
    row = lambda b, j: (b * nj + j, 0)
    mem = lambda b, j: (b, 0)
    half = jnp.arange(LANES) // HEAD_DIM
    seg = (half[:, None] == half[None, :]).astype(BF16)
    kern = functools.partial(_attn_kernel, tq=tq)
    nver = 2 * SWA_KV_WIDTH // LANES
    return pl.pallas_call(
        kern,
        grid=(bsz, nj),
        in_specs=[
            pl.BlockSpec(memory_space=pltpu.SMEM),
            pl.BlockSpec((tq, SWA_Q_WIDTH), row),
            pl.BlockSpec((tq, SWA_KV_WIDTH), row),
            pl.BlockSpec((tq, SWA_KV_WIDTH), row),
            pl.BlockSpec((SUBLANES, tq), row),
            pl.BlockSpec((tq, MEM_WIDTH), row),
            pl.BlockSpec((MEM_LEN, MEM_WIDTH), mem),
            pl.BlockSpec((MEM_LEN, MEM_WIDTH), mem),
            _const_spec((1, SWA_Q_WIDTH)),
            _const_spec((1, SWA_KV_WIDTH)),
            _const_spec((ROPE_CONST_ROWS, LANES)),
            _const_spec((LANES, 3 * LANES)),
            _const_spec((LANES, LANES)),
            _const_spec((1, MEM_HEAD_DIM)),
        ],
        out_specs=[
            pl.BlockSpec((tq, SWA_Q_WIDTH), row),
            pl.BlockSpec((tq, MEM_WIDTH), row),
        ],
        out_shape=[
            jax.ShapeDtypeStruct((t, SWA_Q_WIDTH), BF16),
            jax.ShapeDtypeStruct((t, MEM_WIDTH), BF16),
        ],
        scratch_shapes=[
            pltpu.VMEM((nver, BLOCK + tq, LANES), BF16),
            pltpu.VMEM((nver, BLOCK + tq, LANES), BF16),
        ],
        compiler_params=_cparams(("parallel", "arbitrary")),
    )(sinks, q, k, v, pos_rows, qm, mk, mv, gq, gk, *_rope_constants(), seg, gmq)


def _merge_kernel(x_ref, ys_ref, yw_ref, ym_ref, gate_ref, w0_ref, w1_ref, w2_ref, wo_ref, gf_ref,
                  wr_hi_ref, wr_lo_ref, br_ref, x1_ref, tsl_ref, idx_ref, wt_ref, *, tm):
    gates = gate_ref[...]
    dot = lambda a, b: jnp.dot(a, b, preferred_element_type=F32)
    merged = (gates[:, 0:D_MODEL].astype(F32) * dot(ys_ref[...], w0_ref[...])
              + gates[:, D_MODEL:2 * D_MODEL].astype(F32) * dot(yw_ref[...], w1_ref[...])
              + gates[:, 2 * D_MODEL:].astype(F32) * dot(ym_ref[...], w2_ref[...]))
    x1 = x_ref[...] + dot(merged.astype(BF16), wo_ref[...])
    x1_ref[...] = x1
    t = x1 * lax.rsqrt(jnp.mean(x1 * x1, axis=-1, keepdims=True) + NORM_EPS) * gf_ref[...]

    for j in range(ROW_SUBLANES):
        tsl_ref[pl.ds(j, tm, stride=ROW_SUBLANES), :] = t[:, j * LANES:(j + 1) * LANES]

    t_hi, t_lo = _split_bf16(t)
    logits = (dot(t_hi, wr_hi_ref[...]) + dot(t_hi, wr_lo_ref[...]) + dot(t_lo, wr_hi_ref[...])
              + br_ref[...])
    lane = lax.broadcasted_iota(jnp.int32, logits.shape, 1).astype(F32)
    work = logits
    vals, firsts = [], []
    for _ in range(TOP_K):
        m = jnp.max(work, axis=-1, keepdims=True)
        first = jnp.min(jnp.where(work == m, lane, float(LANES)), axis=-1, keepdims=True)
        work = jnp.where(lane == first, NEG_BIG * 2.0, work)
        vals.append(m)
        firsts.append(first)
    exps = [jnp.exp(v - vals[0]) for v in vals]
    den = exps[0] + exps[1] + exps[2] + exps[3]
    idx_out = jnp.zeros_like(logits)
    wt_out = jnp.zeros_like(logits)
    for k in range(TOP_K):
        idx_out = jnp.where(lane == float(k), firsts[k], idx_out)
        wt_out = jnp.where(lane == float(k), exps[k] / den, wt_out)
    idx_ref[...] = idx_out.astype(jnp.int32)
    wt_ref[...] = wt_out


def _merge(x2, y_ssm_tm, y_swa, y_mem, gates, w0, w1, w2, wo, gf, wr_hi, wr_lo, br, bsz, seq, tm):
    t = bsz * seq
    nj = seq // tm
    row = lambda b, j: (b * nj + j, 0)
    return pl.pallas_call(
        functools.partial(_merge_kernel, tm=tm),
        grid=(bsz, nj),
        in_specs=[
            pl.BlockSpec((tm, D_MODEL), row),
            pl.BlockSpec((tm, SSM_WIDTH), lambda b, j: (j, b)),
            pl.BlockSpec((tm, SWA_Q_WIDTH), row),
            pl.BlockSpec((tm, MEM_WIDTH), row),
            pl.BlockSpec((tm, GATE_WIDTH), row),
            _const_spec((SSM_WIDTH, D_MODEL)),
            _const_spec((SWA_Q_WIDTH, D_MODEL)),
            _const_spec((MEM_WIDTH, D_MODEL)),
            _const_spec((D_MODEL, D_MODEL)),
            _const_spec((1, D_MODEL)),
            _const_spec((D_MODEL, LANES)),
            _const_spec((D_MODEL, LANES)),
            _const_spec((1, LANES)),
        ],
        out_specs=[
            pl.BlockSpec((tm, D_MODEL), row),
            pl.BlockSpec((tm * ROW_SUBLANES, LANES), row),
            pl.BlockSpec((tm, LANES), row),
            pl.BlockSpec((tm, LANES), row),
        ],
        out_shape=[
            jax.ShapeDtypeStruct((t, D_MODEL), F32),
            jax.ShapeDtypeStruct((t * ROW_SUBLANES, LANES), F32),
            jax.ShapeDtypeStruct((t, LANES), jnp.int32),
            jax.ShapeDtypeStruct((t, LANES), F32),
        ],
        compiler_params=_cparams(("parallel", "parallel")),
    )(x2, y_ssm_tm, y_swa, y_mem, gates, w0, w1, w2, wo, gf, wr_hi, wr_lo, br)


DEINT_BLOCK = 2 * LANES


def _expert_prep_kernel(w1_ref, perm_ref, wg_ref, wl_ref):
    perm = perm_ref[...]
    for blk in range(2 * D_EXPERT // DEINT_BLOCK):
        cols = w1_ref[0, :, blk * DEINT_BLOCK:(blk + 1) * DEINT_BLOCK].astype(BF16)
        z = jnp.dot(cols, perm, preferred_element_type=F32).astype(BF16)
        wg_ref[0, :, blk * LANES:(blk + 1) * LANES] = z[:, :LANES]
        wl_ref[0, :, blk * LANES:(blk + 1) * LANES] = z[:, LANES:]


def _expert_prep(w1):
    src = jnp.arange(DEINT_BLOCK)
    dst = jnp.where(src % 2 == 0, src // 2, LANES + src // 2)
    perm = (dst[:, None] == jnp.arange(DEINT_BLOCK)[None, :]).astype(BF16)
    exp3 = lambda e: (e, 0, 0)
    return pl.pallas_call(
        _expert_prep_kernel,
        grid=(N_EXPERTS,),
        in_specs=[pl.BlockSpec((1, D_MODEL, 2 * D_EXPERT), exp3), _const_spec((DEINT_BLOCK, DEINT_BLOCK))],
        out_specs=[pl.BlockSpec((1, D_MODEL, D_EXPERT), exp3), pl.BlockSpec((1, D_MODEL, D_EXPERT), exp3)],
        out_shape=[jax.ShapeDtypeStruct((N_EXPERTS, D_MODEL, D_EXPERT), BF16)] * 2,
        compiler_params=_cparams(("parallel",)),
    )(w1, perm)


SCATTER_BATCH = 8


def _moe_kernel(cnt_ref, off_ref, tok_ref, wt_ref, src_ref, wg_ref, wl_ref, bg_ref, bl_ref, w2_ref,
                b2_ref, out_hbm, acc_ref, gbuf_ref, ybuf_ref, state_ref, sem, *, nb_tokens, tile):
    blk = pl.program_id(0)
    e = pl.program_id(1)
    acc_rows = nb_tokens * ROW_SUBLANES
    buf_rows = tile * ROW_SUBLANES

    def gather(base, half):
        row0 = half * buf_rows
        for i in range(tile):
            tok = tok_ref[0, 0, base + i]
            gbuf_ref[pl.ds(pl.multiple_of(row0 + i * ROW_SUBLANES, ROW_SUBLANES), ROW_SUBLANES), :] = (
                src_ref[pl.ds(pl.multiple_of(tok * ROW_SUBLANES, ROW_SUBLANES), ROW_SUBLANES), :])

    def scatter(base, nvalid, half):
        row0 = half * buf_rows
        for i0 in range(0, tile, SCATTER_BATCH):
            sums, dsts = [], []
            for i in range(i0, i0 + SCATTER_BATCH):
                live = i < nvalid
                tok = jnp.where(live, tok_ref[0, 0, base + i], nb_tokens)
                w = jnp.where(live, wt_ref[0, 0, base + i], 0.0)
                dst = pl.ds(pl.multiple_of(tok * ROW_SUBLANES, ROW_SUBLANES), ROW_SUBLANES)
                src = pl.ds(pl.multiple_of(row0 + i * ROW_SUBLANES, ROW_SUBLANES), ROW_SUBLANES)
                sums.append(acc_ref[dst, :] + w * ybuf_ref[src, :])
                dsts.append(dst)
            for dst, s in zip(dsts, sums):
                acc_ref[dst, :] = s

    @pl.when(e == 0)
    def _():
        acc_ref[...] = jnp.zeros_like(acc_ref)
        ybuf_ref[...] = jnp.zeros_like(ybuf_ref)
        state_ref[0] = 0
        state_ref[1] = 0
        state_ref[2] = 0
        gather(0, 0)

    n = cnt_ref[blk * N_EXPERTS + e]
    start = off_ref[blk * N_EXPERTS + e]
    ntiles = (n + tile - 1) // tile

    def tile_body(ti, carry):
        prev_base, prev_n, half = carry
        base = start + ti * tile
        other = 1 - half
        gather(jnp.where(ti + 1 < ntiles, base + tile, start + n), other)
        row0 = half * buf_rows
        x = jnp.concatenate(
            [gbuf_ref[pl.ds(row0 + j, tile, stride=ROW_SUBLANES), :] for j in range(ROW_SUBLANES)],
            axis=1).astype(BF16)
        hg = jnp.dot(x, wg_ref[0], preferred_element_type=F32) + bg_ref[0]
        hl = jnp.dot(x, wl_ref[0], preferred_element_type=F32) + bl_ref[0]
        xg = jnp.minimum(hg, SWIGLU_LIMIT)
        xl = jnp.clip(hl, -SWIGLU_LIMIT, SWIGLU_LIMIT)
        act = xg * _sigmoid(SWIGLU_ALPHA * xg) * (xl + 1.0)
        y = jnp.dot(act.astype(BF16), w2_ref[0], preferred_element_type=F32) + b2_ref[0]
        scatter(prev_base, prev_n, other)
        for j in range(ROW_SUBLANES):
            ybuf_ref[pl.ds(row0 + j, tile, stride=ROW_SUBLANES), :] = y[:, j * LANES:(j + 1) * LANES]
        return base, n - ti * tile, other

    prev_base, prev_n, half = lax.fori_loop(0, ntiles, tile_body,
                                            (state_ref[0], state_ref[1], state_ref[2]))
    state_ref[0] = prev_base
    state_ref[1] = prev_n
    state_ref[2] = half

    @pl.when(e == N_EXPERTS - 1)
    def _():
        scatter(prev_base, prev_n, 1 - half)
        cp = pltpu.make_async_copy(acc_ref.at[pl.ds(0, acc_rows), :],
                                   out_hbm.at[pl.ds(blk * acc_rows, acc_rows), :], sem)
        cp.start()
        cp.wait()


def _moe(counts, offsets, tok_sorted, wt_sorted, tsl, wg, wl, bg, bl, w2, b2, t, nb_tokens, tile):
    nblk = t // nb_tokens
    slots = tok_sorted.shape[-1]
    exp3 = lambda b, e, *_: (e, 0, 0)
    blk3 = lambda b, e, *_: (b, 0, 0)
    grid_spec = pltpu.PrefetchScalarGridSpec(
        num_scalar_prefetch=2,
        grid=(nblk, N_EXPERTS),
        in_specs=[
            pl.BlockSpec((1, 1, slots), blk3, memory_space=pltpu.SMEM),
            pl.BlockSpec((1, 1, slots), blk3, memory_space=pltpu.SMEM),
            pl.BlockSpec((nb_tokens * ROW_SUBLANES, LANES), lambda b, e, *_: (b, 0),
                         pipeline_mode=pl.Buffered(1)),
            pl.BlockSpec((1, D_MODEL, D_EXPERT), exp3),
            pl.BlockSpec((1, D_MODEL, D_EXPERT), exp3),
            pl.BlockSpec((1, 1, D_EXPERT), exp3),
            pl.BlockSpec((1, 1, D_EXPERT), exp3),
            pl.BlockSpec((1, D_EXPERT, D_MODEL), exp3),
            pl.BlockSpec((1, 1, D_MODEL), exp3),
        ],
        out_specs=pl.BlockSpec(memory_space=pl.ANY),
        scratch_shapes=[
            pltpu.VMEM(((nb_tokens + 1) * ROW_SUBLANES, LANES), F32),
            pltpu.VMEM((2 * tile * ROW_SUBLANES, LANES), F32),
            pltpu.VMEM((2 * tile * ROW_SUBLANES, LANES), F32),
            pltpu.SMEM((3,), jnp.int32),
            pltpu.SemaphoreType.DMA,
        ],
    )
    return pl.pallas_call(
        functools.partial(_moe_kernel, nb_tokens=nb_tokens, tile=tile),
        grid_spec=grid_spec,
        out_shape=jax.ShapeDtypeStruct((t * ROW_SUBLANES, LANES), F32),
        compiler_params=_cparams(("arbitrary", "arbitrary")),
    )(counts, offsets, tok_sorted, wt_sorted, tsl, wg, wl, bg, bl, w2, b2)


def _route(idx, wts, t, nb_tokens, tile):
    nblk = t // nb_tokens
    eid = idx[:, :TOP_K].reshape(nblk, nb_tokens * TOP_K)
    w = wts[:, :TOP_K].reshape(nblk, nb_tokens * TOP_K)
    order = jnp.argsort(eid, axis=1, stable=True)
    tok_sorted = (order // TOP_K).astype(jnp.int32)
    wt_sorted = jnp.take_along_axis(w, order, axis=1)
    counts = jnp.sum((eid[:, :, None] == jnp.arange(N_EXPERTS)[None, None, :]).astype(jnp.int32), axis=1)
    offsets = jnp.cumsum(counts, axis=1) - counts
    pad = ((0, 0), (0, tile))
    tok_sorted = jnp.pad(tok_sorted, pad)[:, None, :]
    wt_sorted = jnp.pad(wt_sorted, pad)[:, None, :]
    return counts.reshape(-1), offsets.reshape(-1).astype(jnp.int32), tok_sorted, wt_sorted


def _residual_kernel(x1_ref, moe_ref, o_ref, *, tm):
    parts = [moe_ref[pl.ds(j, tm, stride=ROW_SUBLANES), :] for j in range(ROW_SUBLANES)]
    o_ref[...] = x1_ref[...] + jnp.concatenate(parts, axis=1)


def _residual(x1, moe, tm):
    t = x1.shape[0]
    return pl.pallas_call(
        functools.partial(_residual_kernel, tm=tm),
        grid=(t // tm,),
        in_specs=[pl.BlockSpec((tm, D_MODEL), lambda i: (i, 0)),
                  pl.BlockSpec((tm * ROW_SUBLANES, LANES), lambda i: (i, 0))],
        out_specs=pl.BlockSpec((tm, D_MODEL), lambda i: (i, 0)),
        out_shape=jax.ShapeDtypeStruct((t, D_MODEL), F32),
        compiler_params=_cparams(("parallel",)),
    )(x1, moe)


def kernel(x, mem, positions, attn_norm_g, mem_norm_g, w_in, b_gate, ssm_log_dt, ssm_a_re, ssm_a_im,
           ssm_b_re, ssm_b_im, ssm_c_re, ssm_c_im, ssm_d, w_glu_v, w_glu_g, swa_q_gain, swa_k_gain,
           swa_sinks, w_mem_kv, mem_q_gain, mem_k_gain, w_br_ssm, w_br_swa, w_br_mem, w_out,
           ffn_norm_g, w_router, b_router, w_mlp1, b_mlp1, w_mlp2, b_mlp2):
    bsz, seq, _ = x.shape
    depth = w_in.shape[0]
    t = bsz * seq
    tm = min(512, seq)
    tq = min(512, seq)
    steps = min(32, seq)

    pos_rows = jnp.broadcast_to(positions.astype(F32).reshape(t // tq, 1, tq),
                                (t // tq, SUBLANES, tq)).reshape(t // tq * SUBLANES, tq)
    mem2 = mem.reshape(bsz * MEM_LEN, D_MODEL)

    x2 = x.reshape(t, D_MODEL)
    for l in range(depth):
        row = lambda a: a[l].reshape(1, -1).astype(F32)
        u_tm, q, k, v, qm, gates = _in_projection(
            x2, row(attn_norm_g), w_in[l].astype(BF16), row(b_gate), bsz, seq, tm)

        bmat, abr, abi, cmat = _s5_discretize(ssm_log_dt[l], ssm_a_re[l], ssm_a_im[l], ssm_b_re[l],
                                              ssm_b_im[l], ssm_c_re[l], ssm_c_im[l])
        y_ssm_tm = _s5_branch(u_tm.reshape(seq * bsz, SSM_WIDTH), bmat, abr, abi, cmat, row(ssm_d),
                              w_glu_v[l].astype(BF16), w_glu_g[l].astype(BF16), bsz, seq, steps)
        y_ssm_tm = y_ssm_tm.reshape(seq, bsz * SSM_WIDTH)

        mk, mv = _mem_kv(mem2, row(mem_norm_g), w_mem_kv[l].astype(BF16), row(mem_k_gain), bsz)
        y_swa, y_mem = _attention(
            swa_sinks[l].astype(F32), q, k, v, pos_rows, qm, mk, mv,
            jnp.tile(row(swa_q_gain), (1, SWA_Q_HEADS)), jnp.tile(row(swa_k_gain), (1, SWA_KV_HEADS)),
            row(mem_q_gain), bsz, seq, tq)

        wr = jnp.pad(w_router[l].astype(F32), ((0, 0), (0, LANES - N_EXPERTS)))
        wr_hi = wr.astype(BF16)
        wr_lo = (wr - wr_hi.astype(F32)).astype(BF16)
        br = jnp.pad(b_router[l].astype(F32), (0, LANES - N_EXPERTS), constant_values=NEG_BIG).reshape(1, LANES)
        x1, tsl, idx, wts = _merge(x2, y_ssm_tm, y_swa, y_mem, gates, w_br_ssm[l].astype(BF16),
                                   w_br_swa[l].astype(BF16), w_br_mem[l].astype(BF16),
                                   w_out[l].astype(BF16), row(ffn_norm_g), wr_hi, wr_lo, br, bsz, seq, tm)

        nb_tokens = min(MOE_TOKEN_BLOCK, t)
        counts, offsets, tok_sorted, wt_sorted = _route(idx, wts, t, nb_tokens, MOE_TILE)
        wg, wl = _expert_prep(w_mlp1[l])
        b1 = b_mlp1[l].astype(F32)
        moe = _moe(counts, offsets, tok_sorted, wt_sorted, tsl, wg, wl, b1[:, None, 0::2],
                   b1[:, None, 1::2], w_mlp2[l].astype(BF16), b_mlp2[l].astype(F32)[:, None, :],
                   t, nb_tokens, MOE_TILE)
        x2 = _residual(x1, moe, tm)
    return x2.reshape(bsz, seq, D_MODEL)
```

```python
import functools
import math

import jax
import jax.numpy as jnp
from jax import lax
from jax.experimental import pallas as pl
from jax.experimental.pallas import tpu as pltpu

F32 = jnp.float32
BF16 = jnp.bfloat16

D_MODEL = 1024
MEM_LEN = 256
NORM_EPS = 1e-5
QK_EPS = 1e-6

SSM_WIDTH = 512
SSM_GROUP = 16
SSM_GROUPS = 32
SSM_STATE = 64
SSM_COMPLEX = SSM_GROUPS * SSM_STATE
S5_SLABS = SSM_WIDTH // 128
S5_SLAB_STATES = SSM_COMPLEX // S5_SLABS

HEAD_DIM = 64
SWA_Q_HEADS = 16
SWA_KV_HEADS = 4
SWA_GROUP = 4
BLOCK = 128
ROPE_THETA = 500000.0
ROPE_DIM = 16
ROPE_HALF = 8

MEM_HEADS = 4
MEM_HEAD_DIM = 128

SWA_Q_WIDTH = 1024
SWA_KV_WIDTH = 256
MEM_WIDTH = 512
N_BRANCH = 3
GATE_WIDTH = N_BRANCH * D_MODEL
OFF_U, OFF_Q, OFF_K, OFF_V, OFF_QM, OFF_G, OFF_END = 0, 512, 1536, 1792, 2048, 2560, 5632

N_EXPERTS = 32
TOP_K = 4
D_EXPERT = 1024
SWIGLU_ALPHA = 1.702
SWIGLU_LIMIT = 7.0

LANES = 128
SUBLANES = 8
VMEM_LIMIT = 56 * 1024 * 1024
NEG_BIG = -1e30

ROW_SUBLANES = D_MODEL // LANES
MOE_TOKEN_BLOCK = 4096
MOE_TILE = 272


def _sigmoid(x):
    return 1.0 / (1.0 + jnp.exp(-x))


def _cparams(sem):
    return pltpu.CompilerParams(dimension_semantics=sem, vmem_limit_bytes=VMEM_LIMIT)


def _const_spec(shape):
    nd = len(shape)
    return pl.BlockSpec(shape, lambda *_: (0,) * nd)


def _inproj_kernel(x_ref, g_ref, w_ref, bg_ref, u_ref, q_ref, k_ref, v_ref, qm_ref, gate_ref):
    x = x_ref[...]
    h = x * lax.rsqrt(jnp.mean(x * x, axis=-1, keepdims=True) + NORM_EPS) * g_ref[...]
    hb = h.astype(BF16)

    def proj(lo, hi):
        return jnp.dot(hb, w_ref[:, lo:hi], preferred_element_type=F32)

    u_ref[...] = proj(OFF_U, OFF_Q).astype(BF16)
    q_ref[...] = proj(OFF_Q, OFF_K).astype(BF16)
    k_ref[...] = proj(OFF_K, OFF_V).astype(BF16)
    v_ref[...] = proj(OFF_V, OFF_QM).astype(BF16)
    qm_ref[...] = proj(OFF_QM, OFF_G).astype(BF16)
    gate_ref[...] = _sigmoid(proj(OFF_G, OFF_END) + bg_ref[...]).astype(BF16)


def _in_projection(x2, g, w_in_b, b_gate, bsz, seq, tm):
    t = bsz * seq
    nj = seq // tm
    row = lambda b, j: (b * nj + j, 0)
    return pl.pallas_call(
        _inproj_kernel,
        grid=(bsz, nj),
        in_specs=[
            pl.BlockSpec((tm, D_MODEL), row),
            _const_spec((1, D_MODEL)),
            _const_spec((D_MODEL, OFF_END)),
            _const_spec((1, GATE_WIDTH)),
        ],
        out_specs=[
            pl.BlockSpec((tm, SSM_WIDTH), lambda b, j: (j, b)),
            pl.BlockSpec((tm, SWA_Q_WIDTH), row),
            pl.BlockSpec((tm, SWA_KV_WIDTH), row),
            pl.BlockSpec((tm, SWA_KV_WIDTH), row),
            pl.BlockSpec((tm, MEM_WIDTH), row),
            pl.BlockSpec((tm, GATE_WIDTH), row),
        ],
        out_shape=[
            jax.ShapeDtypeStruct((seq, bsz * SSM_WIDTH), BF16),
            jax.ShapeDtypeStruct((t, SWA_Q_WIDTH), BF16),
            jax.ShapeDtypeStruct((t, SWA_KV_WIDTH), BF16),
            jax.ShapeDtypeStruct((t, SWA_KV_WIDTH), BF16),
            jax.ShapeDtypeStruct((t, MEM_WIDTH), BF16),
            jax.ShapeDtypeStruct((t, GATE_WIDTH), BF16),
        ],
        compiler_params=_cparams(("parallel", "parallel")),
    )(x2, g, w_in_b, b_gate)


def _gelu_tanh(x):
    c = math.sqrt(2.0 / math.pi)
    return 0.5 * x * (1.0 + jnp.tanh(c * (x + 0.044715 * (x * x * x))))


def _s5_kernel(u_ref, bmat_ref, are_ref, aim_ref, cmat_ref, d_ref, wv_ref, wg_ref, o_ref,
               st_ref, bu_ref, *, bsz, steps, col_chunk):
    nc = SSM_COMPLEX

    @pl.when(pl.program_id(0) == 0)
    def _():
        st_ref[...] = jnp.zeros_like(st_ref)

    u = u_ref[...]
    for m in range(S5_SLABS):
        um = u[:, m * LANES:(m + 1) * LANES]
        re = slice(m * S5_SLAB_STATES, (m + 1) * S5_SLAB_STATES)
        im = slice(nc + m * S5_SLAB_STATES, nc + (m + 1) * S5_SLAB_STATES)
        bu_ref[:, re] = jnp.dot(um, bmat_ref[m, :, :S5_SLAB_STATES], preferred_element_type=F32)
        bu_ref[:, im] = jnp.dot(um, bmat_ref[m, :, S5_SLAB_STATES:], preferred_element_type=F32)

    for lo in range(0, nc, col_chunk):
        re = slice(lo, lo + col_chunk)
        im = slice(nc + lo, nc + lo + col_chunk)
        ar = jnp.broadcast_to(are_ref[:, re], (bsz, col_chunk))
        ai = jnp.broadcast_to(aim_ref[:, re], (bsz, col_chunk))

        def step(t, carry):
            sr, si = carry
            rows = pl.ds(pl.multiple_of(t * bsz, bsz), bsz)
            nr = ar * sr - ai * si + bu_ref[rows, re]
            ni = ar * si + ai * sr + bu_ref[rows, im]
            bu_ref[rows, re] = nr
            bu_ref[rows, im] = ni
            return nr, ni

        sr, si = lax.fori_loop(0, steps, step, (st_ref[:, re], st_ref[:, im]))
        st_ref[:, re] = sr
        st_ref[:, im] = si

    ys = []
    for m in range(S5_SLABS):
        re = slice(m * S5_SLAB_STATES, (m + 1) * S5_SLAB_STATES)
        im = slice(nc + m * S5_SLAB_STATES, nc + (m + 1) * S5_SLAB_STATES)
        ys.append(jnp.dot(bu_ref[:, re].astype(BF16), cmat_ref[m, :S5_SLAB_STATES, :],
                          preferred_element_type=F32)
                  + jnp.dot(bu_ref[:, im].astype(BF16), cmat_ref[m, S5_SLAB_STATES:, :],
                            preferred_element_type=F32))
    y = jnp.concatenate(ys, axis=1) + d_ref[...] * u.astype(F32)
    yb = _gelu_tanh(y).astype(BF16)
    val = jnp.dot(yb, wv_ref[...], preferred_element_type=F32)
    gate = jnp.dot(yb, wg_ref[...], preferred_element_type=F32)
    o_ref[...] = (val * _sigmoid(gate)).astype(BF16)


def _s5_branch(u_tm, bmat, a_re, a_im, cmat, d_skip, wv, wg, bsz, seq, steps):
    rows = steps * bsz
    kern = functools.partial(_s5_kernel, bsz=bsz, steps=steps, col_chunk=512)
    return pl.pallas_call(
        kern,
        grid=(seq // steps,),
        in_specs=[
            pl.BlockSpec((rows, SSM_WIDTH), lambda i: (i, 0)),
            _const_spec((S5_SLABS, LANES, 2 * S5_SLAB_STATES)),
            _const_spec((1, SSM_COMPLEX)),
            _const_spec((1, SSM_COMPLEX)),
            _const_spec((S5_SLABS, 2 * S5_SLAB_STATES, LANES)),
            _const_spec((1, SSM_WIDTH)),
            _const_spec((SSM_WIDTH, SSM_WIDTH)),
            _const_spec((SSM_WIDTH, SSM_WIDTH)),
        ],
        out_specs=pl.BlockSpec((rows, SSM_WIDTH), lambda i: (i, 0)),
        out_shape=jax.ShapeDtypeStruct((seq * bsz, SSM_WIDTH), BF16),
        scratch_shapes=[
            pltpu.VMEM((bsz, 2 * SSM_COMPLEX), F32),
            pltpu.VMEM((rows, 2 * SSM_COMPLEX), F32),
        ],
        compiler_params=_cparams(("arbitrary",)),
    )(u_tm, bmat, a_re, a_im, cmat, d_skip, wv, wg)


def _s5_discretize(log_dt, a_re, a_im, b_re, b_im, c_re, c_im):
    dt = jnp.exp(log_dt.astype(F32))[:, None]
    lr = jnp.minimum(a_re.astype(F32), -1e-4)
    li = a_im.astype(F32)
    mag = jnp.exp(lr * dt)
    abr = mag * jnp.cos(li * dt)
    abi = mag * jnp.sin(li * dt)
    nr, ni = abr - 1.0, abi
    den = lr * lr + li * li
    fr = (nr * lr + ni * li) / den
    fi = (ni * lr - nr * li) / den
    br, bi = b_re.astype(F32), b_im.astype(F32)
    bbr = fr[..., None] * br - fi[..., None] * bi
    bbi = fr[..., None] * bi + fi[..., None] * br
    gps = SSM_GROUPS // S5_SLABS
    eye = jnp.eye(gps, dtype=F32)

    def blockdiag_in(m):
        m = m.reshape(S5_SLABS, gps, SSM_STATE, SSM_GROUP)
        return jnp.einsum('sgpc,gh->sgchp', m, eye).reshape(S5_SLABS, LANES, S5_SLAB_STATES)

    def blockdiag_out(m):
        m = m.reshape(S5_SLABS, gps, SSM_GROUP, SSM_STATE)
        return jnp.einsum('sgcp,gh->sgphc', m, eye).reshape(S5_SLABS, S5_SLAB_STATES, LANES)

    bmat = jnp.concatenate([blockdiag_in(bbr), blockdiag_in(bbi)], axis=2).astype(BF16)
    cmat = jnp.concatenate([blockdiag_out(c_re.astype(F32)), -blockdiag_out(c_im.astype(F32))],
                           axis=1).astype(BF16)
    return bmat, abr.reshape(1, SSM_COMPLEX), abi.reshape(1, SSM_COMPLEX), cmat


def _memkv_kernel(mem_ref, g_ref, w_ref, kg_ref, mk_ref, mv_ref):
    x = mem_ref[...]
    h = x * lax.rsqrt(jnp.mean(x * x, axis=-1, keepdims=True) + NORM_EPS) * g_ref[...]
    kv = jnp.dot(h.astype(BF16), w_ref[...], preferred_element_type=F32)
    parts = []
    for hd in range(MEM_HEADS):
        kh = kv[:, hd * MEM_HEAD_DIM:(hd + 1) * MEM_HEAD_DIM]
        kh = kh * lax.rsqrt(jnp.mean(kh * kh, axis=-1, keepdims=True) + QK_EPS) * kg_ref[...]
        parts.append(kh)
    mk_ref[...] = jnp.concatenate(parts, axis=1).astype(BF16)
    mv_ref[...] = kv[:, MEM_WIDTH:].astype(BF16)


def _mem_kv(mem2, g, w_b, k_gain, bsz):
    return pl.pallas_call(
        _memkv_kernel,
        grid=(bsz,),
        in_specs=[
            pl.BlockSpec((MEM_LEN, D_MODEL), lambda b: (b, 0)),
            _const_spec((1, D_MODEL)),
            _const_spec((D_MODEL, 2 * MEM_WIDTH)),
            _const_spec((1, MEM_HEAD_DIM)),
        ],
        out_specs=[
            pl.BlockSpec((MEM_LEN, MEM_WIDTH), lambda b: (b, 0)),
            pl.BlockSpec((MEM_LEN, MEM_WIDTH), lambda b: (b, 0)),
        ],
        out_shape=[
            jax.ShapeDtypeStruct((bsz * MEM_LEN, MEM_WIDTH), BF16),
            jax.ShapeDtypeStruct((bsz * MEM_LEN, MEM_WIDTH), BF16),
        ],
        compiler_params=_cparams(("parallel",)),
    )(mem2, g, w_b, k_gain)


HEADS_PER_COL = LANES // HEAD_DIM
ROPE_CONST_ROWS = 16


def _split_bf16(a):
    hi = a.astype(BF16)
    lo = (a - hi.astype(F32)).astype(BF16)
    return hi, lo


def _rope_tables(pos_row, rc_ref, re_ref):
    tq = pos_row.shape[1]
    freq = jnp.concatenate([rc_ref[0:ROPE_HALF, :]] * (tq // LANES), axis=1)
    ang = freq * pos_row
    trig = jnp.concatenate([jnp.cos(ang), jnp.sin(ang),
                            jnp.zeros((LANES - 2 * ROPE_HALF, tq), F32)], axis=0)
    tab = jnp.dot(trig.T, re_ref[...], precision=lax.Precision.HIGHEST, preferred_element_type=F32)
    return tab[:, 0:LANES] + rc_ref[ROPE_HALF:ROPE_HALF + 1, :], tab[:, LANES:2 * LANES]


def _attn_kernel(sink_ref, q_ref, k_ref, v_ref, pos_ref, qm_ref, mk_ref, mv_ref,
                 rc_ref, re_ref, perm_ref, seg_ref, gmq_ref, oswa_ref, omem_ref, kbuf_ref, vbuf_ref, *, tq):
    nblk = tq // BLOCK
    first_tile = pl.program_id(1) == 0
    dot = lambda a, b: jnp.dot(a, b, preferred_element_type=F32)

    nver = kbuf_ref.shape[0] // 2
    mine = (pl.program_id(1) % 2) * nver
    other = nver - mine

    @pl.when(first_tile)
    def _():
        kbuf_ref[0:nver, 0:BLOCK, :] = jnp.zeros((nver, BLOCK, LANES), BF16)
        vbuf_ref[0:nver, 0:BLOCK, :] = jnp.zeros((nver, BLOCK, 2 * LANES), BF16)
        vbuf_ref[:, :, LANES:2 * LANES] = jnp.ones((2 * nver, BLOCK + tq, LANES), BF16)

    cos_t, sin_t = _rope_tables(pos_ref[0:1, :], rc_ref, re_ref)
    seg = seg_ref[...]
    perm = perm_ref[...]
    lo_half_pk = lax.broadcasted_iota(jnp.int32, (tq, LANES), 1) < HEAD_DIM
    lo_half_blk = lax.broadcasted_iota(jnp.int32, (BLOCK, LANES), 1) < HEAD_DIM

    def norm_rope(raw_bf, gain_cos, gain_sin):
        raw = raw_bf.astype(F32)
        hi, lo = _split_bf16(raw * raw)
        rs = lax.rsqrt((dot(hi, seg) + dot(lo, seg)) * (1.0 / HEAD_DIM) + QK_EPS)
        return (raw * gain_cos + dot(raw_bf, perm) * gain_sin) * rs

    q_cos = cos_t * rc_ref[ROPE_HALF + 1:ROPE_HALF + 2, :]
    q_sin = sin_t * rc_ref[ROPE_HALF + 2:ROPE_HALF + 3, :]
    k_cos = cos_t * rc_ref[ROPE_HALF + 3:ROPE_HALF + 4, :]
    k_sin = sin_t * rc_ref[ROPE_HALF + 4:ROPE_HALF + 5, :]

    for col in range(SWA_KV_WIDTH // LANES):
        cs = slice(col * LANES, (col + 1) * LANES)
        kc = norm_rope(k_ref[:, cs], k_cos, k_sin)
        versions = ((kbuf_ref, slice(None), kc.astype(BF16), pltpu.roll(kc, HEAD_DIM, 1).astype(BF16)),
                    (vbuf_ref, slice(0, LANES), v_ref[:, cs],
                     pltpu.roll(v_ref[:, cs].astype(F32), HEAD_DIM, 1).astype(BF16)))
        for buf, lanes, plain, rotated in versions:
            for ver, val in ((2 * col, plain), (2 * col + 1, rotated)):
                buf[mine + ver, BLOCK:BLOCK + tq, lanes] = val
                buf[other + ver, 0:BLOCK, lanes] = val[tq - BLOCK:tq]

    q_half = []
    for col in range(SWA_Q_WIDTH // LANES):
        cs = slice(col * LANES, (col + 1) * LANES)
        qc = norm_rope(q_ref[:, cs], q_cos, q_sin)
        q_half.append((jnp.where(lo_half_pk, qc, 0.0).astype(BF16), jnp.where(lo_half_pk, 0.0, qc).astype(BF16)))

    qi = lax.broadcasted_iota(jnp.int32, (2 * BLOCK, BLOCK), 0) & (BLOCK - 1)
    kj = lax.broadcasted_iota(jnp.int32, (2 * BLOCK, BLOCK), 1)
    own = kj <= qi
    no_prev = jnp.where(first_tile, 1, 0).astype(F32) * NEG_BIG

    for n in range(nblk):
        r0 = n * BLOCK
        for h in range(SWA_KV_HEADS):
            by_half = []
            for half in range(HEADS_PER_COL):
                ver = 2 * (h // HEADS_PER_COL) + (0 if half == h % HEADS_PER_COL else 1)
                qs = jnp.concatenate([q_half[2 * h][half][r0:r0 + BLOCK],
                                      q_half[2 * h + 1][half][r0:r0 + BLOCK]], axis=0)
                s2 = lax.dot_general(qs, kbuf_ref[mine + ver, r0:r0 + 2 * BLOCK, :], (((1,), (1,)), ((), ())),
                                     preferred_element_type=F32)
                s_prev = s2[:, :BLOCK] + no_prev if n == 0 else s2[:, :BLOCK]
                s = jnp.where(own, s2[:, BLOCK:], s_prev)
                sink = jnp.concatenate(
                    [jnp.full((BLOCK, BLOCK), sink_ref[h * SWA_GROUP + half], F32),
                     jnp.full((BLOCK, BLOCK), sink_ref[h * SWA_GROUP + half + HEADS_PER_COL], F32)], axis=0)
                m = jnp.maximum(jnp.broadcast_to(jnp.max(s, axis=-1, keepdims=True), s.shape), sink)
                p = jnp.exp(s - m)
                p2 = jnp.concatenate([jnp.where(own, 0.0, p).astype(BF16),
                                      jnp.where(own, p, 0.0).astype(BF16)], axis=1)
                o2 = dot(p2, vbuf_ref[mine + ver, r0:r0 + 2 * BLOCK, :])
                by_half.append(o2[:, :LANES] / (o2[:, LANES:] + jnp.exp(sink - m)))
            for sub in range(2):
                rs_ = slice(sub * BLOCK, (sub + 1) * BLOCK)
                colv = jnp.where(lo_half_blk, by_half[0][rs_], by_half[1][rs_])
                oswa_ref[r0:r0 + BLOCK, (2 * h + sub) * LANES:(2 * h + sub + 1) * LANES] = colv.astype(BF16)

    ones = jnp.ones((MEM_LEN, LANES), BF16)
    outs = []
    for hd in range(MEM_HEADS):
        sl = slice(hd * MEM_HEAD_DIM, (hd + 1) * MEM_HEAD_DIM)
        qh = qm_ref[:, sl].astype(F32)
        rs = lax.rsqrt(jnp.mean(qh * qh, axis=-1, keepdims=True) + QK_EPS)
        qh = qh * (rs * (MEM_HEAD_DIM ** -0.5)) * gmq_ref[...]
        s = lax.dot_general(qh.astype(BF16), mk_ref[:, sl], (((1,), (1,)), ((), ())),
                            preferred_element_type=F32)
        p = jnp.exp(s - jnp.max(s, axis=-1, keepdims=True)).astype(BF16)
        o2 = dot(p, jnp.concatenate([mv_ref[:, sl], ones], axis=1))
        outs.append(o2[:, :LANES] / o2[:, LANES:])
    omem_ref[...] = jnp.concatenate(outs, axis=1).astype(BF16)


def _rope_constants(q_gain, k_gain):
    inv_freq = ROPE_THETA ** (-jnp.arange(ROPE_HALF, dtype=F32) / ROPE_HALF)
    lane = jnp.arange(LANES) % HEAD_DIM
    j = jnp.arange(ROPE_HALF)[:, None]
    e_cos = ((lane[None, :] < ROPE_DIM) & (lane[None, :] % ROPE_HALF == j)).astype(F32)
    e_sin = (lane[None, :] == j + ROPE_HALF).astype(F32) - (lane[None, :] == j).astype(F32)
    ones = (lane >= ROPE_DIM).astype(F32)[None, :]
    src = jnp.where(lane < ROPE_HALF, jnp.arange(LANES) + ROPE_HALF, jnp.arange(LANES) - ROPE_HALF)
    rotary = lane < ROPE_DIM
    perm = ((jnp.arange(LANES)[:, None] == src[None, :]) & rotary[None, :]).astype(BF16)

    def gains(g, scale):
        col = jnp.tile(g.astype(F32).reshape(-1), HEADS_PER_COL) * scale
        return col[None, :], jnp.where(rotary, col[jnp.clip(src, 0, LANES - 1)], 0.0)[None, :]

    qg, qgp = gains(q_gain, HEAD_DIM ** -0.5)
    kg, kgp = gains(k_gain, 1.0)
    rows = jnp.concatenate([jnp.broadcast_to(inv_freq[:, None], (ROPE_HALF, LANES)), ones, qg, qgp, kg, kgp,
                            jnp.zeros((ROPE_CONST_ROWS - ROPE_HALF - 5, LANES), F32)], axis=0)
    zero8 = jnp.zeros((ROPE_HALF, LANES), F32)
    expand = jnp.concatenate([
        jnp.concatenate([e_cos, zero8], axis=1),
        jnp.concatenate([zero8, e_sin], axis=1),
        jnp.zeros((LANES - 2 * ROPE_HALF, 2 * LANES), F32)], axis=0)
    return rows, expand, perm


def _attention(sinks, q, k, v, pos_rows, qm, mk, mv, q_gain, k_gain, gmq, bsz, seq, tq):
    t = bsz * seq
    nj = seq // tq
    row = lambda b, j: (b * nj + j, 0)
    mem = lambda b, j: (b, 0)
    half = jnp.arange(LANES) // HEAD_DIM
    seg = (half[:, None] == half[None, :]).astype(BF16)
    kern = functools.partial(_attn_kernel, tq=tq)
    nver = 2 * (2 * SWA_KV_WIDTH // LANES)
    return pl.pallas_call(
        kern,
        grid=(bsz, nj),
        in_specs=[
            pl.BlockSpec(memory_space=pltpu.SMEM),
            pl.BlockSpec((tq, SWA_Q_WIDTH), row),
            pl.BlockSpec((tq, SWA_KV_WIDTH), row),
            pl.BlockSpec((tq, SWA_KV_WIDTH), row),
            pl.BlockSpec((SUBLANES, tq), row),
            pl.BlockSpec((tq, MEM_WIDTH), row),
            pl.BlockSpec((MEM_LEN, MEM_WIDTH), mem),
            pl.BlockSpec((MEM_LEN, MEM_WIDTH), mem),
            _const_spec((ROPE_CONST_ROWS, LANES)),
            _const_spec((LANES, 2 * LANES)),
            _const_spec((LANES, LANES)),
            _const_spec((LANES, LANES)),
            _const_spec((1, MEM_HEAD_DIM)),
        ],
        out_specs=[
            pl.BlockSpec((tq, SWA_Q_WIDTH), row),
            pl.BlockSpec((tq, MEM_WIDTH), row),
        ],
        out_shape=[
            jax.ShapeDtypeStruct((t, SWA_Q_WIDTH), BF16),
            jax.ShapeDtypeStruct((t, MEM_WIDTH), BF16),
        ],
        scratch_shapes=[
            pltpu.VMEM((nver, BLOCK + tq, LANES), BF16),
            pltpu.VMEM((nver, BLOCK + tq, 2 * LANES), BF16),
        ],
        compiler_params=_cparams(("parallel", "arbitrary")),
    )(sinks, q, k, v, pos_rows, qm, mk, mv, *_rope_constants(q_gain, k_gain), seg, gmq)


def _merge_kernel(x_ref, ys_ref, yw_ref, ym_ref, gate_ref, w0_ref, w1_ref, w2_ref, wo_ref, gf_ref,
                  wr_hi_ref, wr_lo_ref, br_ref, x1_ref, tsl_ref, idx_ref, wt_ref, *, tm):
    gates = gate_ref[...]
    dot = lambda a, b: jnp.dot(a, b, preferred_element_type=F32)
    merged = (gates[:, 0:D_MODEL].astype(F32) * dot(ys_ref[...], w0_ref[...])
              + gates[:, D_MODEL:2 * D_MODEL].astype(F32) * dot(yw_ref[...], w1_ref[...])
              + gates[:, 2 * D_MODEL:].astype(F32) * dot(ym_ref[...], w2_ref[...]))
    x1 = x_ref[...] + dot(merged.astype(BF16), wo_ref[...])
    x1_ref[...] = x1
    t = x1 * lax.rsqrt(jnp.mean(x1 * x1, axis=-1, keepdims=True) + NORM_EPS) * gf_ref[...]

    for j in range(ROW_SUBLANES):
        tsl_ref[pl.ds(j, tm, stride=ROW_SUBLANES), :] = t[:, j * LANES:(j + 1) * LANES]

    t_hi, t_lo = _split_bf16(t)
    logits = (dot(t_hi, wr_hi_ref[...]) + dot(t_hi, wr_lo_ref[...]) + dot(t_lo, wr_hi_ref[...])
              + br_ref[...])
    lane = lax.broadcasted_iota(jnp.int32, logits.shape, 1).astype(F32)
    work = logits
    vals, firsts = [], []
    for _ in range(TOP_K):
        m = jnp.max(work, axis=-1, keepdims=True)
        first = jnp.min(jnp.where(work == m, lane, float(LANES)), axis=-1, keepdims=True)
        work = jnp.where(lane == first, NEG_BIG * 2.0, work)
        vals.append(m)
        firsts.append(first)
    exps = [jnp.exp(v - vals[0]) for v in vals]
    den = exps[0] + exps[1] + exps[2] + exps[3]
    idx_out = jnp.zeros_like(logits)
    wt_out = jnp.zeros_like(logits)
    for k in range(TOP_K):
        idx_out = jnp.where(lane == float(k), firsts[k], idx_out)
        wt_out = jnp.where(lane == float(k), exps[k] / den, wt_out)
    idx_ref[...] = idx_out.astype(jnp.int32)
    wt_ref[...] = wt_out


def _merge(x2, y_ssm_tm, y_swa, y_mem, gates, w0, w1, w2, wo, gf, wr_hi, wr_lo, br, bsz, seq, tm):
    t = bsz * seq
    nj = seq // tm
    row = lambda b, j: (b * nj + j, 0)
    return pl.pallas_call(
        functools.partial(_merge_kernel, tm=tm),
        grid=(bsz, nj),
        in_specs=[
            pl.BlockSpec((tm, D_MODEL), row),
            pl.BlockSpec((tm, SSM_WIDTH), lambda b, j: (j, b)),
            pl.BlockSpec((tm, SWA_Q_WIDTH), row),
            pl.BlockSpec((tm, MEM_WIDTH), row),
            pl.BlockSpec((tm, GATE_WIDTH), row),
            _const_spec((SSM_WIDTH, D_MODEL)),
            _const_spec((SWA_Q_WIDTH, D_MODEL)),
            _const_spec((MEM_WIDTH, D_MODEL)),
            _const_spec((D_MODEL, D_MODEL)),
            _const_spec((1, D_MODEL)),
            _const_spec((D_MODEL, LANES)),
            _const_spec((D_MODEL, LANES)),
            _const_spec((1, LANES)),
        ],
        out_specs=[
            pl.BlockSpec((tm, D_MODEL), row),
            pl.BlockSpec((tm * ROW_SUBLANES, LANES), row),
            pl.BlockSpec((tm, LANES), row),
            pl.BlockSpec((tm, LANES), row),
        ],
        out_shape=[
            jax.ShapeDtypeStruct((t, D_MODEL), F32),
            jax.ShapeDtypeStruct((t * ROW_SUBLANES, LANES), F32),
            jax.ShapeDtypeStruct((t, LANES), jnp.int32),
            jax.ShapeDtypeStruct((t, LANES), F32),
        ],
        compiler_params=_cparams(("parallel", "parallel")),
    )(x2, y_ssm_tm, y_swa, y_mem, gates, w0, w1, w2, wo, gf, wr_hi, wr_lo, br)


DEINT_BLOCK = 2 * LANES


def _expert_prep_kernel(w1_ref, perm_ref, wg_ref, wl_ref):
    perm = perm_ref[...]
    for blk in range(2 * D_EXPERT // DEINT_BLOCK):
        cols = w1_ref[0, :, blk * DEINT_BLOCK:(blk + 1) * DEINT_BLOCK].astype(BF16)
        z = jnp.dot(cols, perm, preferred_element_type=F32).astype(BF16)
        wg_ref[0, :, blk * LANES:(blk + 1) * LANES] = z[:, :LANES]
        wl_ref[0, :, blk * LANES:(blk + 1) * LANES] = z[:, LANES:]


def _expert_prep(w1):
    src = jnp.arange(DEINT_BLOCK)
    dst = jnp.where(src % 2 == 0, src // 2, LANES + src // 2)
    perm = (dst[:, None] == jnp.arange(DEINT_BLOCK)[None, :]).astype(BF16)
    exp3 = lambda e: (e, 0, 0)
    return pl.pallas_call(
        _expert_prep_kernel,
        grid=(N_EXPERTS,),
        in_specs=[pl.BlockSpec((1, D_MODEL, 2 * D_EXPERT), exp3), _const_spec((DEINT_BLOCK, DEINT_BLOCK))],
        out_specs=[pl.BlockSpec((1, D_MODEL, D_EXPERT), exp3), pl.BlockSpec((1, D_MODEL, D_EXPERT), exp3)],
        out_shape=[jax.ShapeDtypeStruct((N_EXPERTS, D_MODEL, D_EXPERT), BF16)] * 2,
        compiler_params=_cparams(("parallel",)),
    )(w1, perm)


SCATTER_BATCH = 8


def _moe_kernel(cnt_ref, off_ref, tok_ref, wt_ref, src_ref, wg_ref, wl_ref, bg_ref, bl_ref, w2_ref,
                b2_ref, out_hbm, acc_ref, gbuf_ref, ybuf_ref, state_ref, sem, *, nb_tokens, tile):
    blk = pl.program_id(0)
    e = pl.program_id(1)
    acc_rows = nb_tokens * ROW_SUBLANES
    buf_rows = tile * ROW_SUBLANES

    def gather(base, half):
        row0 = half * buf_rows
        for i in range(tile):
            tok = tok_ref[0, 0, base + i]
            gbuf_ref[pl.ds(pl.multiple_of(row0 + i * ROW_SUBLANES, ROW_SUBLANES), ROW_SUBLANES), :] = (
                src_ref[pl.ds(pl.multiple_of(tok * ROW_SUBLANES, ROW_SUBLANES), ROW_SUBLANES), :])

    def scatter(base, nvalid, half):
        row0 = half * buf_rows
        for i0 in range(0, tile, SCATTER_BATCH):
            sums, dsts = [], []
            for i in range(i0, i0 + SCATTER_BATCH):
                live = i < nvalid
                tok = jnp.where(live, tok_ref[0, 0, base + i], nb_tokens)
                w = jnp.where(live, wt_ref[0, 0, base + i], 0.0)
                dst = pl.ds(pl.multiple_of(tok * ROW_SUBLANES, ROW_SUBLANES), ROW_SUBLANES)
                src = pl.ds(pl.multiple_of(row0 + i * ROW_SUBLANES, ROW_SUBLANES), ROW_SUBLANES)
                sums.append(acc_ref[dst, :] + w * ybuf_ref[src, :])
                dsts.append(dst)
            for dst, s in zip(dsts, sums):
                acc_ref[dst, :] = s

    @pl.when(e == 0)
    def _():
        acc_ref[...] = jnp.zeros_like(acc_ref)
        ybuf_ref[...] = jnp.zeros_like(ybuf_ref)
        state_ref[0] = 0
        state_ref[1] = 0
        state_ref[2] = 0
        gather(0, 0)

    n = cnt_ref[blk * N_EXPERTS + e]
    start = off_ref[blk * N_EXPERTS + e]
    ntiles = (n + tile - 1) // tile

    def tile_body(ti, carry):
        prev_base, prev_n, half = carry
        base = start + ti * tile
        other = 1 - half
        gather(jnp.where(ti + 1 < ntiles, base + tile, start + n), other)
        row0 = half * buf_rows
        x = jnp.concatenate(
            [gbuf_ref[pl.ds(row0 + j, tile, stride=ROW_SUBLANES), :] for j in range(ROW_SUBLANES)],
            axis=1).astype(BF16)
        hg = jnp.dot(x, wg_ref[0], preferred_element_type=F32) + bg_ref[0]
        hl = jnp.dot(x, wl_ref[0], preferred_element_type=F32) + bl_ref[0]
        xg = jnp.minimum(hg, SWIGLU_LIMIT)
        xl = jnp.clip(hl, -SWIGLU_LIMIT, SWIGLU_LIMIT)
        act = xg * _sigmoid(SWIGLU_ALPHA * xg) * (xl + 1.0)
        y = jnp.dot(act.astype(BF16), w2_ref[0], preferred_element_type=F32) + b2_ref[0]
        scatter(prev_base, prev_n, other)
        for j in range(ROW_SUBLANES):
            ybuf_ref[pl.ds(row0 + j, tile, stride=ROW_SUBLANES), :] = y[:, j * LANES:(j + 1) * LANES]
        return base, n - ti * tile, other

    prev_base, prev_n, half = lax.fori_loop(0, ntiles, tile_body,
                                            (state_ref[0], state_ref[1], state_ref[2]))
    state_ref[0] = prev_base
    state_ref[1] = prev_n
    state_ref[2] = half

    @pl.when(e == N_EXPERTS - 1)
    def _():
        scatter(prev_base, prev_n, 1 - half)
        cp = pltpu.make_async_copy(acc_ref.at[pl.ds(0, acc_rows), :],
                                   out_hbm.at[pl.ds(blk * acc_rows, acc_rows), :], sem)
        cp.start()
        cp.wait()


def _moe(counts, offsets, tok_sorted, wt_sorted, tsl, wg, wl, bg, bl, w2, b2, t, nb_tokens, tile):
    nblk = t // nb_tokens
    slots = tok_sorted.shape[-1]
    exp3 = lambda b, e, *_: (e, 0, 0)
    blk3 = lambda b, e, *_: (b, 0, 0)
    grid_spec = pltpu.PrefetchScalarGridSpec(
        num_scalar_prefetch=2,
        grid=(nblk, N_EXPERTS),
        in_specs=[
            pl.BlockSpec((1, 1, slots), blk3, memory_space=pltpu.SMEM),
            pl.BlockSpec((1, 1, slots), blk3, memory_space=pltpu.SMEM),
            pl.BlockSpec((nb_tokens * ROW_SUBLANES, LANES), lambda b, e, *_: (b, 0),
                         pipeline_mode=pl.Buffered(1)),
            pl.BlockSpec((1, D_MODEL, D_EXPERT), exp3),
            pl.BlockSpec((1, D_MODEL, D_EXPERT), exp3),
            pl.BlockSpec((1, 1, D_EXPERT), exp3),
            pl.BlockSpec((1, 1, D_EXPERT), exp3),
            pl.BlockSpec((1, D_EXPERT, D_MODEL), exp3),
            pl.BlockSpec((1, 1, D_MODEL), exp3),
        ],
        out_specs=pl.BlockSpec(memory_space=pl.ANY),
        scratch_shapes=[
            pltpu.VMEM(((nb_tokens + 1) * ROW_SUBLANES, LANES), F32),
            pltpu.VMEM((2 * tile * ROW_SUBLANES, LANES), F32),
            pltpu.VMEM((2 * tile * ROW_SUBLANES, LANES), F32),
            pltpu.SMEM((3,), jnp.int32),
            pltpu.SemaphoreType.DMA,
        ],
    )
    return pl.pallas_call(
        functools.partial(_moe_kernel, nb_tokens=nb_tokens, tile=tile),
        grid_spec=grid_spec,
        out_shape=jax.ShapeDtypeStruct((t * ROW_SUBLANES, LANES), F32),
        compiler_params=_cparams(("arbitrary", "arbitrary")),
    )(counts, offsets, tok_sorted, wt_sorted, tsl, wg, wl, bg, bl, w2, b2)


def _route(idx, wts, t, nb_tokens, tile):
    nblk = t // nb_tokens
    eid = idx[:, :TOP_K].reshape(nblk, nb_tokens * TOP_K)
    w = wts[:, :TOP_K].reshape(nblk, nb_tokens * TOP_K)
    order = jnp.argsort(eid, axis=1, stable=True)
    tok_sorted = (order // TOP_K).astype(jnp.int32)
    wt_sorted = jnp.take_along_axis(w, order, axis=1)
    counts = jnp.sum((eid[:, :, None] == jnp.arange(N_EXPERTS)[None, None, :]).astype(jnp.int32), axis=1)
    offsets = jnp.cumsum(counts, axis=1) - counts
    pad = ((0, 0), (0, tile))
    tok_sorted = jnp.pad(tok_sorted, pad)[:, None, :]
    wt_sorted = jnp.pad(wt_sorted, pad)[:, None, :]
    return counts.reshape(-1), offsets.reshape(-1).astype(jnp.int32), tok_sorted, wt_sorted


def _residual_kernel(x1_ref, moe_ref, o_ref, *, tm):
    parts = [moe_ref[pl.ds(j, tm, stride=ROW_SUBLANES), :] for j in range(ROW_SUBLANES)]
    o_ref[...] = x1_ref[...] + jnp.concatenate(parts, axis=1)


def _residual(x1, moe, tm):
    t = x1.shape[0]
    return pl.pallas_call(
        functools.partial(_residual_kernel, tm=tm),
        grid=(t // tm,),
        in_specs=[pl.BlockSpec((tm, D_MODEL), lambda i: (i, 0)),
                  pl.BlockSpec((tm * ROW_SUBLANES, LANES), lambda i: (i, 0))],
        out_specs=pl.BlockSpec((tm, D_MODEL), lambda i: (i, 0)),
        out_shape=jax.ShapeDtypeStruct((t, D_MODEL), F32),
        compiler_params=_cparams(("parallel",)),
    )(x1, moe)


def kernel(x, mem, positions, attn_norm_g, mem_norm_g, w_in, b_gate, ssm_log_dt, ssm_a_re, ssm_a_im,
           ssm_b_re, ssm_b_im, ssm_c_re, ssm_c_im, ssm_d, w_glu_v, w_glu_g, swa_q_gain, swa_k_gain,
           swa_sinks, w_mem_kv, mem_q_gain, mem_k_gain, w_br_ssm, w_br_swa, w_br_mem, w_out,
           ffn_norm_g, w_router, b_router, w_mlp1, b_mlp1, w_mlp2, b_mlp2):
    bsz, seq, _ = x.shape
    depth = w_in.shape[0]
    t = bsz * seq
    tm = min(512, seq)
    tq = min(512, seq)
    steps = min(32, seq)

    pos_rows = jnp.broadcast_to(positions.astype(F32).reshape(t // tq, 1, tq),
                                (t // tq, SUBLANES, tq)).reshape(t // tq * SUBLANES, tq)
    mem2 = mem.reshape(bsz * MEM_LEN, D_MODEL)

    x2 = x.reshape(t, D_MODEL)
    for l in range(depth):
        row = lambda a: a[l].reshape(1, -1).astype(F32)
        u_tm, q, k, v, qm, gates = _in_projection(
            x2, row(attn_norm_g), w_in[l].astype(BF16), row(b_gate), bsz, seq, tm)

        bmat, abr, abi, cmat = _s5_discretize(ssm_log_dt[l], ssm_a_re[l], ssm_a_im[l], ssm_b_re[l],
                                              ssm_b_im[l], ssm_c_re[l], ssm_c_im[l])
        y_ssm_tm = _s5_branch(u_tm.reshape(seq * bsz, SSM_WIDTH), bmat, abr, abi, cmat, row(ssm_d),
                              w_glu_v[l].astype(BF16), w_glu_g[l].astype(BF16), bsz, seq, steps)
        y_ssm_tm = y_ssm_tm.reshape(seq, bsz * SSM_WIDTH)

        mk, mv = _mem_kv(mem2, row(mem_norm_g), w_mem_kv[l].astype(BF16), row(mem_k_gain), bsz)
        y_swa, y_mem = _attention(
            swa_sinks[l].astype(F32), q, k, v, pos_rows, qm, mk, mv, swa_q_gain[l], swa_k_gain[l],
            row(mem_q_gain), bsz, seq, tq)

        wr = jnp.pad(w_router[l].astype(F32), ((0, 0), (0, LANES - N_EXPERTS)))
        wr_hi = wr.astype(BF16)
        wr_lo = (wr - wr_hi.astype(F32)).astype(BF16)
        br = jnp.pad(b_router[l].astype(F32), (0, LANES - N_EXPERTS), constant_values=NEG_BIG).reshape(1, LANES)
        x1, tsl, idx, wts = _merge(x2, y_ssm_tm, y_swa, y_mem, gates, w_br_ssm[l].astype(BF16),
                                   w_br_swa[l].astype(BF16), w_br_mem[l].astype(BF16),
                                   w_out[l].astype(BF16), row(ffn_norm_g), wr_hi, wr_lo, br, bsz, seq, tm)

        nb_tokens = min(MOE_TOKEN_BLOCK, t)
        idx, w1_l, w2_l = lax.optimization_barrier((idx, w_mlp1[l], w_mlp2[l]))
        counts, offsets, tok_sorted, wt_sorted = _route(idx, wts, t, nb_tokens, MOE_TILE)
        wg, wl = _expert_prep(w1_l)
        b1 = b_mlp1[l].astype(F32)
        moe = _moe(counts, offsets, tok_sorted, wt_sorted, tsl, wg, wl, b1[:, None, 0::2],
                   b1[:, None, 1::2], w2_l.astype(BF16), b_mlp2[l].astype(F32)[:, None, :],
                   t, nb_tokens, MOE_TILE)
        x2 = _residual(x1, moe, tm)
    return x2.reshape(bsz, seq, D_MODEL)
```

```python
import functools
import math

import jax
import jax.numpy as jnp
from jax import lax
from jax.experimental import pallas as pl
from jax.experimental.pallas import tpu as pltpu

F32 = jnp.float32
BF16 = jnp.bfloat16

D_MODEL = 1024
MEM_LEN = 256
NORM_EPS = 1e-5
QK_EPS = 1e-6

SSM_WIDTH = 512
SSM_GROUP = 16
SSM_GROUPS = 32
SSM_STATE = 64
SSM_COMPLEX = SSM_GROUPS * SSM_STATE
S5_SLABS = SSM_WIDTH // 128
S5_SLAB_STATES = SSM_COMPLEX // S5_SLABS

HEAD_DIM = 64
SWA_Q_HEADS = 16
SWA_KV_HEADS = 4
SWA_GROUP = 4
BLOCK = 128
ROPE_THETA = 500000.0
ROPE_DIM = 16
ROPE_HALF = 8

MEM_HEADS = 4
MEM_HEAD_DIM = 128

SWA_Q_WIDTH = 1024
SWA_KV_WIDTH = 256
MEM_WIDTH = 512
N_BRANCH = 3
GATE_WIDTH = N_BRANCH * D_MODEL
OFF_U, OFF_Q, OFF_K, OFF_V, OFF_QM, OFF_G, OFF_END = 0, 512, 1536, 1792, 2048, 2560, 5632

N_EXPERTS = 32
TOP_K = 4
D_EXPERT = 1024
SWIGLU_ALPHA = 1.702
SWIGLU_LIMIT = 7.0

LANES = 128
SUBLANES = 8
VMEM_LIMIT = 56 * 1024 * 1024
NEG_BIG = -1e30

ROW_SUBLANES = D_MODEL // LANES
MOE_TOKEN_BLOCK = 4096
MOE_TILE = 272


def _sigmoid(x):
    return 1.0 / (1.0 + jnp.exp(-x))


def _cparams(sem):
    return pltpu.CompilerParams(dimension_semantics=sem, vmem_limit_bytes=VMEM_LIMIT)


def _const_spec(shape):
    nd = len(shape)
    return pl.BlockSpec(shape, lambda *_: (0,) * nd)


def _inproj_kernel(x_ref, g_ref, w_ref, bg_ref, u_ref, q_ref, k_ref, v_ref, qm_ref, gate_ref):
    x = x_ref[...]
    h = x * lax.rsqrt(jnp.mean(x * x, axis=-1, keepdims=True) + NORM_EPS) * g_ref[...]
    hb = h.astype(BF16)

    def proj(lo, hi):
        return jnp.dot(hb, w_ref[:, lo:hi], preferred_element_type=F32)

    u_ref[...] = proj(OFF_U, OFF_Q).astype(BF16)
    q_ref[...] = proj(OFF_Q, OFF_K).astype(BF16)
    k_ref[...] = proj(OFF_K, OFF_V).astype(BF16)
    v_ref[...] = proj(OFF_V, OFF_QM).astype(BF16)
    qm_ref[...] = proj(OFF_QM, OFF_G).astype(BF16)
    gate_ref[...] = _sigmoid(proj(OFF_G, OFF_END) + bg_ref[...]).astype(BF16)


def _in_projection(x2, g, w_in_b, b_gate, bsz, seq, tm):
    t = bsz * seq
    nj = seq // tm
    row = lambda b, j: (b * nj + j, 0)
    return pl.pallas_call(
        _inproj_kernel,
        grid=(bsz, nj),
        in_specs=[
            pl.BlockSpec((tm, D_MODEL), row),
            _const_spec((1, D_MODEL)),
            _const_spec((D_MODEL, OFF_END)),
            _const_spec((1, GATE_WIDTH)),
        ],
        out_specs=[
            pl.BlockSpec((tm, SSM_WIDTH), lambda b, j: (j, b)),
            pl.BlockSpec((tm, SWA_Q_WIDTH), row),
            pl.BlockSpec((tm, SWA_KV_WIDTH), row),
            pl.BlockSpec((tm, SWA_KV_WIDTH), row),
            pl.BlockSpec((tm, MEM_WIDTH), row),
            pl.BlockSpec((tm, GATE_WIDTH), row),
        ],
        out_shape=[
            jax.ShapeDtypeStruct((seq, bsz * SSM_WIDTH), BF16),
            jax.ShapeDtypeStruct((t, SWA_Q_WIDTH), BF16),
            jax.ShapeDtypeStruct((t, SWA_KV_WIDTH), BF16),
            jax.ShapeDtypeStruct((t, SWA_KV_WIDTH), BF16),
            jax.ShapeDtypeStruct((t, MEM_WIDTH), BF16),
            jax.ShapeDtypeStruct((t, GATE_WIDTH), BF16),
        ],
        compiler_params=_cparams(("parallel", "parallel")),
    )(x2, g, w_in_b, b_gate)


def _gelu_tanh(x):
    c = math.sqrt(2.0 / math.pi)
    return 0.5 * x * (1.0 + jnp.tanh(c * (x + 0.044715 * (x * x * x))))


def _s5_kernel(u_ref, perm_ref, permt_ref, bmat_ref, are_ref, aim_ref, cmat_ref, d_ref, wv_ref, wg_ref,
               o_ref, st_ref, bu_ref, *, bsz, steps, col_chunk):
    nc = SSM_COMPLEX

    @pl.when(pl.program_id(0) == 0)
    def _():
        st_ref[...] = jnp.zeros_like(st_ref)

    u_bt = jnp.concatenate([u_ref[:, b * SSM_WIDTH:(b + 1) * SSM_WIDTH] for b in range(bsz)], axis=0)
    u = jnp.dot(perm_ref[...], u_bt, preferred_element_type=F32).astype(BF16)
    for m in range(S5_SLABS):
        um = u[:, m * LANES:(m + 1) * LANES]
        re = slice(m * S5_SLAB_STATES, (m + 1) * S5_SLAB_STATES)
        im = slice(nc + m * S5_SLAB_STATES, nc + (m + 1) * S5_SLAB_STATES)
        bu_ref[:, re] = jnp.dot(um, bmat_ref[m, :, :S5_SLAB_STATES], preferred_element_type=F32)
        bu_ref[:, im] = jnp.dot(um, bmat_ref[m, :, S5_SLAB_STATES:], preferred_element_type=F32)

    for lo in range(0, nc, col_chunk):
        re = slice(lo, lo + col_chunk)
        im = slice(nc + lo, nc + lo + col_chunk)
        ar = jnp.broadcast_to(are_ref[:, re], (bsz, col_chunk))
        ai = jnp.broadcast_to(aim_ref[:, re], (bsz, col_chunk))

        def step(t, carry):
            sr, si = carry
            rows = pl.ds(pl.multiple_of(t * bsz, bsz), bsz)
            nr = ar * sr - ai * si + bu_ref[rows, re]
            ni = ar * si + ai * sr + bu_ref[rows, im]
            bu_ref[rows, re] = nr
            bu_ref[rows, im] = ni
            return nr, ni

        sr, si = lax.fori_loop(0, steps, step, (st_ref[:, re], st_ref[:, im]))
        st_ref[:, re] = sr
        st_ref[:, im] = si

    ys = []
    for m in range(S5_SLABS):
        re = slice(m * S5_SLAB_STATES, (m + 1) * S5_SLAB_STATES)
        im = slice(nc + m * S5_SLAB_STATES, nc + (m + 1) * S5_SLAB_STATES)
        ys.append(jnp.dot(bu_ref[:, re].astype(BF16), cmat_ref[m, :S5_SLAB_STATES, :],
                          preferred_element_type=F32)
                  + jnp.dot(bu_ref[:, im].astype(BF16), cmat_ref[m, S5_SLAB_STATES:, :],
                            preferred_element_type=F32))
    y = jnp.concatenate(ys, axis=1) + d_ref[...] * u.astype(F32)
    yb = _gelu_tanh(y).astype(BF16)
    val = jnp.dot(yb, wv_ref[...], preferred_element_type=F32)
    gate = jnp.dot(yb, wg_ref[...], preferred_element_type=F32)
    out = (val * _sigmoid(gate)).astype(BF16)
    out_bt = jnp.dot(permt_ref[...], out, preferred_element_type=F32).astype(BF16)
    for b in range(bsz):
        o_ref[:, b * SSM_WIDTH:(b + 1) * SSM_WIDTH] = out_bt[b * steps:(b + 1) * steps]


def _s5_branch(u_tm, bmat, a_re, a_im, cmat, d_skip, wv, wg, bsz, seq, steps):
    rows = steps * bsz
    kern = functools.partial(_s5_kernel, bsz=bsz, steps=steps, col_chunk=512)
    r = jnp.arange(rows)
    perm = (((r % bsz) * steps + r // bsz)[:, None] == r[None, :]).astype(BF16)
    return pl.pallas_call(
        kern,
        grid=(seq // steps,),
        in_specs=[
            pl.BlockSpec((steps, bsz * SSM_WIDTH), lambda i: (i, 0)),
            _const_spec((rows, rows)),
            _const_spec((rows, rows)),
            _const_spec((S5_SLABS, LANES, 2 * S5_SLAB_STATES)),
            _const_spec((1, SSM_COMPLEX)),
            _const_spec((1, SSM_COMPLEX)),
            _const_spec((S5_SLABS, 2 * S5_SLAB_STATES, LANES)),
            _const_spec((1, SSM_WIDTH)),
            _const_spec((SSM_WIDTH, SSM_WIDTH)),
            _const_spec((SSM_WIDTH, SSM_WIDTH)),
        ],
        out_specs=pl.BlockSpec((steps, bsz * SSM_WIDTH), lambda i: (i, 0)),
        out_shape=jax.ShapeDtypeStruct((seq, bsz * SSM_WIDTH), BF16),
        scratch_shapes=[
            pltpu.VMEM((bsz, 2 * SSM_COMPLEX), F32),
            pltpu.VMEM((rows, 2 * SSM_COMPLEX), F32),
        ],
        compiler_params=_cparams(("arbitrary",)),
    )(u_tm, perm, perm.T, bmat, a_re, a_im, cmat, d_skip, wv, wg)


def _s5_discretize(log_dt, a_re, a_im, b_re, b_im, c_re, c_im):
    dt = jnp.exp(log_dt.astype(F32))[:, None]
    lr = jnp.minimum(a_re.astype(F32), -1e-4)
    li = a_im.astype(F32)
    mag = jnp.exp(lr * dt)
    abr = mag * jnp.cos(li * dt)
    abi = mag * jnp.sin(li * dt)
    nr, ni = abr - 1.0, abi
    den = lr * lr + li * li
    fr = (nr * lr + ni * li) / den
    fi = (ni * lr - nr * li) / den
    br, bi = b_re.astype(F32), b_im.astype(F32)
    bbr = fr[..., None] * br - fi[..., None] * bi
    bbi = fr[..., None] * bi + fi[..., None] * br
    gps = SSM_GROUPS // S5_SLABS
    eye = jnp.eye(gps, dtype=F32)

    def blockdiag_in(m):
        m = m.reshape(S5_SLABS, gps, SSM_STATE, SSM_GROUP)
        return jnp.einsum('sgpc,gh->sgchp', m, eye).reshape(S5_SLABS, LANES, S5_SLAB_STATES)

    def blockdiag_out(m):
        m = m.reshape(S5_SLABS, gps, SSM_GROUP, SSM_STATE)
        return jnp.einsum('sgcp,gh->sgphc', m, eye).reshape(S5_SLABS, S5_SLAB_STATES, LANES)

    bmat = jnp.concatenate([blockdiag_in(bbr), blockdiag_in(bbi)], axis=2).astype(BF16)
    cmat = jnp.concatenate([blockdiag_out(c_re.astype(F32)), -blockdiag_out(c_im.astype(F32))],
                           axis=1).astype(BF16)
    return bmat, abr.reshape(1, SSM_COMPLEX), abi.reshape(1, SSM_COMPLEX), cmat


def _memkv_kernel(mem_ref, g_ref, w_ref, kg_ref, mk_ref, mv_ref):
    x = mem_ref[...]
    h = x * lax.rsqrt(jnp.mean(x * x, axis=-1, keepdims=True) + NORM_EPS) * g_ref[...]
    kv = jnp.dot(h.astype(BF16), w_ref[...], preferred_element_type=F32)
    parts = []
    for hd in range(MEM_HEADS):
        kh = kv[:, hd * MEM_HEAD_DIM:(hd + 1) * MEM_HEAD_DIM]
        kh = kh * lax.rsqrt(jnp.mean(kh * kh, axis=-1, keepdims=True) + QK_EPS) * kg_ref[...]
        parts.append(kh)
    mk_ref[...] = jnp.concatenate(parts, axis=1).astype(BF16)
    mv_ref[...] = kv[:, MEM_WIDTH:].astype(BF16)


def _mem_kv(mem2, g, w_b, k_gain, bsz):
    return pl.pallas_call(
        _memkv_kernel,
        grid=(bsz,),
        in_specs=[
            pl.BlockSpec((MEM_LEN, D_MODEL), lambda b: (b, 0)),
            _const_spec((1, D_MODEL)),
            _const_spec((D_MODEL, 2 * MEM_WIDTH)),
            _const_spec((1, MEM_HEAD_DIM)),
        ],
        out_specs=[
            pl.BlockSpec((MEM_LEN, MEM_WIDTH), lambda b: (b, 0)),
            pl.BlockSpec((MEM_LEN, MEM_WIDTH), lambda b: (b, 0)),
        ],
        out_shape=[
            jax.ShapeDtypeStruct((bsz * MEM_LEN, MEM_WIDTH), BF16),
            jax.ShapeDtypeStruct((bsz * MEM_LEN, MEM_WIDTH), BF16),
        ],
        compiler_params=_cparams(("parallel",)),
    )(mem2, g, w_b, k_gain)


HEADS_PER_COL = LANES // HEAD_DIM
ROPE_CONST_ROWS = 16


def _split_bf16(a):
    hi = a.astype(BF16)
    lo = (a - hi.astype(F32)).astype(BF16)
    return hi, lo


def _rope_tables(pos_row, rc_ref, re_ref):
    tq = pos_row.shape[1]
    freq = jnp.concatenate([rc_ref[0:ROPE_HALF, :]] * (tq // LANES), axis=1)
    ang = freq * pos_row
    trig = jnp.concatenate([jnp.cos(ang), jnp.sin(ang),
                            jnp.zeros((LANES - 2 * ROPE_HALF, tq), F32)], axis=0)
    tab = jnp.dot(trig.T, re_ref[...], precision=lax.Precision.HIGHEST, preferred_element_type=F32)
    return tab[:, 0:LANES] + rc_ref[ROPE_HALF:ROPE_HALF + 1, :], tab[:, LANES:2 * LANES]


def _attn_kernel(sink_ref, q_ref, k_ref, v_ref, pos_ref, qm_ref, mk_ref, mv_ref,
                 rc_ref, re_ref, perm_ref, seg_ref, gmq_ref, oswa_ref, omem_ref, kbuf_ref, vbuf_ref, *, tq):
    nblk = tq // BLOCK
    first_tile = pl.program_id(1) == 0
    dot = lambda a, b: jnp.dot(a, b, preferred_element_type=F32)

    nver = kbuf_ref.shape[0] // 2
    mine = (pl.program_id(1) % 2) * nver
    other = nver - mine

    @pl.when(first_tile)
    def _():
        kbuf_ref[0:nver, 0:BLOCK, :] = jnp.zeros((nver, BLOCK, LANES), BF16)
        vbuf_ref[0:nver, 0:BLOCK, :] = jnp.zeros((nver, BLOCK, 2 * LANES), BF16)
        vbuf_ref[:, :, LANES:2 * LANES] = jnp.ones((2 * nver, BLOCK + tq, LANES), BF16)

    cos_t, sin_t = _rope_tables(pos_ref[0:1, :], rc_ref, re_ref)
    seg = seg_ref[...]
    perm = perm_ref[...]
    lo_half_pk = lax.broadcasted_iota(jnp.int32, (tq, LANES), 1) < HEAD_DIM
    lo_half_blk = lax.broadcasted_iota(jnp.int32, (BLOCK, LANES), 1) < HEAD_DIM

    def norm_rope(raw_bf, gain_cos, gain_sin):
        raw = raw_bf.astype(F32)
        hi, lo = _split_bf16(raw * raw)
        rs = lax.rsqrt((dot(hi, seg) + dot(lo, seg)) * (1.0 / HEAD_DIM) + QK_EPS)
        return (raw * gain_cos + dot(raw_bf, perm) * gain_sin) * rs

    q_cos = cos_t * rc_ref[ROPE_HALF + 1:ROPE_HALF + 2, :]
    q_sin = sin_t * rc_ref[ROPE_HALF + 2:ROPE_HALF + 3, :]
    k_cos = cos_t * rc_ref[ROPE_HALF + 3:ROPE_HALF + 4, :]
    k_sin = sin_t * rc_ref[ROPE_HALF + 4:ROPE_HALF + 5, :]

    for col in range(SWA_KV_WIDTH // LANES):
        cs = slice(col * LANES, (col + 1) * LANES)
        kc = norm_rope(k_ref[:, cs], k_cos, k_sin)
        versions = ((kbuf_ref, slice(None), kc.astype(BF16), pltpu.roll(kc, HEAD_DIM, 1).astype(BF16)),
                    (vbuf_ref, slice(0, LANES), v_ref[:, cs],
                     pltpu.roll(v_ref[:, cs].astype(F32), HEAD_DIM, 1).astype(BF16)))
        for buf, lanes, plain, rotated in versions:
            for ver, val in ((2 * col, plain), (2 * col + 1, rotated)):
                buf[mine + ver, BLOCK:BLOCK + tq, lanes] = val
                buf[other + ver, 0:BLOCK, lanes] = val[tq - BLOCK:tq]

    q_half = []
    for col in range(SWA_Q_WIDTH // LANES):
        cs = slice(col * LANES, (col + 1) * LANES)
        qc = norm_rope(q_ref[:, cs], q_cos, q_sin)
        q_half.append((jnp.where(lo_half_pk, qc, 0.0).astype(BF16), jnp.where(lo_half_pk, 0.0, qc).astype(BF16)))

    qi = lax.broadcasted_iota(jnp.int32, (2 * BLOCK, BLOCK), 0) & (BLOCK - 1)
    kj = lax.broadcasted_iota(jnp.int32, (2 * BLOCK, BLOCK), 1)
    own = kj <= qi
    no_prev = jnp.where(first_tile, 1, 0).astype(F32) * NEG_BIG

    for n in range(nblk):
        r0 = n * BLOCK
        for h in range(SWA_KV_HEADS):
            by_half = []
            for half in range(HEADS_PER_COL):
                ver = 2 * (h // HEADS_PER_COL) + (0 if half == h % HEADS_PER_COL else 1)
                qs = jnp.concatenate([q_half[2 * h][half][r0:r0 + BLOCK],
                                      q_half[2 * h + 1][half][r0:r0 + BLOCK]], axis=0)
                s2 = lax.dot_general(qs, kbuf_ref[mine + ver, r0:r0 + 2 * BLOCK, :], (((1,), (1,)), ((), ())),
                                     preferred_element_type=F32)
                s_prev = s2[:, :BLOCK] + no_prev if n == 0 else s2[:, :BLOCK]
                s = jnp.where(own, s2[:, BLOCK:], s_prev)
                sink = jnp.concatenate(
                    [jnp.full((BLOCK, BLOCK), sink_ref[h * SWA_GROUP + half], F32),
                     jnp.full((BLOCK, BLOCK), sink_ref[h * SWA_GROUP + half + HEADS_PER_COL], F32)], axis=0)
                m = jnp.maximum(jnp.broadcast_to(jnp.max(s, axis=-1, keepdims=True), s.shape), sink)
                p = jnp.exp(s - m)
                p2 = jnp.concatenate([jnp.where(own, 0.0, p).astype(BF16),
                                      jnp.where(own, p, 0.0).astype(BF16)], axis=1)
                o2 = dot(p2, vbuf_ref[mine + ver, r0:r0 + 2 * BLOCK, :])
                by_half.append(o2[:, :LANES] / (o2[:, LANES:] + jnp.exp(sink - m)))
            for sub in range(2):
                rs_ = slice(sub * BLOCK, (sub + 1) * BLOCK)
                colv = jnp.where(lo_half_blk, by_half[0][rs_], by_half[1][rs_])
                oswa_ref[r0:r0 + BLOCK, (2 * h + sub) * LANES:(2 * h + sub + 1) * LANES] = colv.astype(BF16)

    ones = jnp.ones((MEM_LEN, LANES), BF16)
    outs = []
    for hd in range(MEM_HEADS):
        sl = slice(hd * MEM_HEAD_DIM, (hd + 1) * MEM_HEAD_DIM)
        qh = qm_ref[:, sl].astype(F32)
        rs = lax.rsqrt(jnp.mean(qh * qh, axis=-1, keepdims=True) + QK_EPS)
        qh = qh * (rs * (MEM_HEAD_DIM ** -0.5)) * gmq_ref[...]
        s = lax.dot_general(qh.astype(BF16), mk_ref[:, sl], (((1,), (1,)), ((), ())),
                            preferred_element_type=F32)
        p = jnp.exp(s - jnp.max(s, axis=-1, keepdims=True)).astype(BF16)
        o2 = dot(p, jnp.concatenate([mv_ref[:, sl], ones], axis=1))
        outs.append(o2[:, :LANES] / o2[:, LANES:])
    omem_ref[...] = jnp.concatenate(outs, axis=1).astype(BF16)


def _rope_constants(q_gain, k_gain):
    inv_freq = ROPE_THETA ** (-jnp.arange(ROPE_HALF, dtype=F32) / ROPE_HALF)
    lane = jnp.arange(LANES) % HEAD_DIM
    j = jnp.arange(ROPE_HALF)[:, None]
    e_cos = ((lane[None, :] < ROPE_DIM) & (lane[None, :] % ROPE_HALF == j)).astype(F32)
    e_sin = (lane[None, :] == j + ROPE_HALF).astype(F32) - (lane[None, :] == j).astype(F32)
    ones = (lane >= ROPE_DIM).astype(F32)[None, :]
    src = jnp.where(lane < ROPE_HALF, jnp.arange(LANES) + ROPE_HALF, jnp.arange(LANES) - ROPE_HALF)
    rotary = lane < ROPE_DIM
    perm = ((jnp.arange(LANES)[:, None] == src[None, :]) & rotary[None, :]).astype(BF16)

    def gains(g, scale):
        col = jnp.tile(g.astype(F32).reshape(-1), HEADS_PER_COL) * scale
        return col[None, :], jnp.where(rotary, col[jnp.clip(src, 0, LANES - 1)], 0.0)[None, :]

    qg, qgp = gains(q_gain, HEAD_DIM ** -0.5)
    kg, kgp = gains(k_gain, 1.0)
    rows = jnp.concatenate([jnp.broadcast_to(inv_freq[:, None], (ROPE_HALF, LANES)), ones, qg, qgp, kg, kgp,
                            jnp.zeros((ROPE_CONST_ROWS - ROPE_HALF - 5, LANES), F32)], axis=0)
    zero8 = jnp.zeros((ROPE_HALF, LANES), F32)
    expand = jnp.concatenate([
        jnp.concatenate([e_cos, zero8], axis=1),
        jnp.concatenate([zero8, e_sin], axis=1),
        jnp.zeros((LANES - 2 * ROPE_HALF, 2 * LANES), F32)], axis=0)
    return rows, expand, perm


def _attention(sinks, q, k, v, pos_rows, qm, mk, mv, q_gain, k_gain, gmq, bsz, seq, tq):
    t = bsz * seq
    nj = seq // tq
    row = lambda b, j: (b * nj + j, 0)
    mem = lambda b, j: (b, 0)
    half = jnp.arange(LANES) // HEAD_DIM
    seg = (half[:, None] == half[None, :]).astype(BF16)
    kern = functools.partial(_attn_kernel, tq=tq)
    nver = 2 * (2 * SWA_KV_WIDTH // LANES)
    return pl.pallas_call(
        kern,
        grid=(bsz, nj),
        in_specs=[
            pl.BlockSpec(memory_space=pltpu.SMEM),
            pl.BlockSpec((tq, SWA_Q_WIDTH), row),
            pl.BlockSpec((tq, SWA_KV_WIDTH), row),
            pl.BlockSpec((tq, SWA_KV_WIDTH), row),
            pl.BlockSpec((SUBLANES, tq), row),
            pl.BlockSpec((tq, MEM_WIDTH), row),
            pl.BlockSpec((MEM_LEN, MEM_WIDTH), mem),
            pl.BlockSpec((MEM_LEN, MEM_WIDTH), mem),
            _const_spec((ROPE_CONST_ROWS, LANES)),
            _const_spec((LANES, 2 * LANES)),
            _const_spec((LANES, LANES)),
            _const_spec((LANES, LANES)),
            _const_spec((1, MEM_HEAD_DIM)),
        ],
        out_specs=[
            pl.BlockSpec((tq, SWA_Q_WIDTH), row),
            pl.BlockSpec((tq, MEM_WIDTH), row),
        ],
        out_shape=[
            jax.ShapeDtypeStruct((t, SWA_Q_WIDTH), BF16),
            jax.ShapeDtypeStruct((t, MEM_WIDTH), BF16),
        ],
        scratch_shapes=[
            pltpu.VMEM((nver, BLOCK + tq, LANES), BF16),
            pltpu.VMEM((nver, BLOCK + tq, 2 * LANES), BF16),
        ],
        compiler_params=_cparams(("parallel", "arbitrary")),
    )(sinks, q, k, v, pos_rows, qm, mk, mv, *_rope_constants(q_gain, k_gain), seg, gmq)


def _merge_kernel(x_ref, ys_ref, yw_ref, ym_ref, gate_ref, w0_ref, w1_ref, w2_ref, wo_ref, gf_ref,
                  wr_cat_ref, br_ref, x1_ref, tsl_ref, idx_ref, wt_ref, *, tm):
    gates = gate_ref[...]
    dot = lambda a, b: jnp.dot(a, b, preferred_element_type=F32)
    merged = (gates[:, 0:D_MODEL].astype(F32) * dot(ys_ref[...], w0_ref[...])
              + gates[:, D_MODEL:2 * D_MODEL].astype(F32) * dot(yw_ref[...], w1_ref[...])
              + gates[:, 2 * D_MODEL:].astype(F32) * dot(ym_ref[...], w2_ref[...]))
    x1 = x_ref[...] + dot(merged.astype(BF16), wo_ref[...])
    x1_ref[...] = x1
    t = x1 * lax.rsqrt(jnp.mean(x1 * x1, axis=-1, keepdims=True) + NORM_EPS) * gf_ref[...]

    for j in range(ROW_SUBLANES):
        tsl_ref[pl.ds(j, tm, stride=ROW_SUBLANES), :] = t[:, j * LANES:(j + 1) * LANES]

    t_hi, t_lo = _split_bf16(t)
    both = dot(t_hi, wr_cat_ref[...])
    logits = both[:, :LANES] + both[:, LANES:] + dot(t_lo, wr_cat_ref[:, :LANES]) + br_ref[...]
    lane = lax.broadcasted_iota(jnp.int32, logits.shape, 1).astype(F32)
    work = logits
    vals, firsts = [], []
    for _ in range(TOP_K):
        m = jnp.max(work, axis=-1, keepdims=True)
        first = jnp.min(jnp.where(work == m, lane, float(LANES)), axis=-1, keepdims=True)
        work = jnp.where(lane == first, NEG_BIG * 2.0, work)
        vals.append(m)
        firsts.append(first)
    exps = [jnp.exp(v - vals[0]) for v in vals]
    den = exps[0] + exps[1] + exps[2] + exps[3]
    idx_out = jnp.zeros_like(logits)
    wt_out = jnp.zeros_like(logits)
    for k in range(TOP_K):
        idx_out = jnp.where(lane == float(k), firsts[k], idx_out)
        wt_out = jnp.where(lane == float(k), exps[k] / den, wt_out)
    idx_ref[...] = idx_out.astype(jnp.int32)
    wt_ref[...] = wt_out


def _merge(x2, y_ssm_tm, y_swa, y_mem, gates, w0, w1, w2, wo, gf, wr_cat, br, bsz, seq, tm):
    t = bsz * seq
    nj = seq // tm
    row = lambda b, j: (b * nj + j, 0)
    return pl.pallas_call(
        functools.partial(_merge_kernel, tm=tm),
        grid=(bsz, nj),
        in_specs=[
            pl.BlockSpec((tm, D_MODEL), row),
            pl.BlockSpec((tm, SSM_WIDTH), lambda b, j: (j, b)),
            pl.BlockSpec((tm, SWA_Q_WIDTH), row),
            pl.BlockSpec((tm, MEM_WIDTH), row),
            pl.BlockSpec((tm, GATE_WIDTH), row),
            _const_spec((SSM_WIDTH, D_MODEL)),
            _const_spec((SWA_Q_WIDTH, D_MODEL)),
            _const_spec((MEM_WIDTH, D_MODEL)),
            _const_spec((D_MODEL, D_MODEL)),
            _const_spec((1, D_MODEL)),
            _const_spec((D_MODEL, 2 * LANES)),
            _const_spec((1, LANES)),
        ],
        out_specs=[
            pl.BlockSpec((tm, D_MODEL), row),
            pl.BlockSpec((tm * ROW_SUBLANES, LANES), row),
            pl.BlockSpec((tm, LANES), row),
            pl.BlockSpec((tm, LANES), row),
        ],
        out_shape=[
            jax.ShapeDtypeStruct((t, D_MODEL), F32),
            jax.ShapeDtypeStruct((t * ROW_SUBLANES, LANES), F32),
            jax.ShapeDtypeStruct((t, LANES), jnp.int32),
            jax.ShapeDtypeStruct((t, LANES), F32),
        ],
        compiler_params=_cparams(("parallel", "parallel")),
    )(x2, y_ssm_tm, y_swa, y_mem, gates, w0, w1, w2, wo, gf, wr_cat, br)


DEINT_BLOCK = 2 * LANES


def _expert_prep_kernel(w1_ref, perm_ref, w1d_ref):
    perm = perm_ref[...]
    for blk in range(2 * D_EXPERT // DEINT_BLOCK):
        cols = w1_ref[0, :, blk * DEINT_BLOCK:(blk + 1) * DEINT_BLOCK].astype(BF16)
        z = jnp.dot(cols, perm, preferred_element_type=F32).astype(BF16)
        w1d_ref[0, :, blk * LANES:(blk + 1) * LANES] = z[:, :LANES]
        w1d_ref[0, :, D_EXPERT + blk * LANES:D_EXPERT + (blk + 1) * LANES] = z[:, LANES:]


def _expert_prep(w1):
    src = jnp.arange(DEINT_BLOCK)
    dst = jnp.where(src % 2 == 0, src // 2, LANES + src // 2)
    perm = (dst[:, None] == jnp.arange(DEINT_BLOCK)[None, :]).astype(BF16)
    exp3 = lambda e: (e, 0, 0)
    return pl.pallas_call(
        _expert_prep_kernel,
        grid=(N_EXPERTS,),
        in_specs=[pl.BlockSpec((1, D_MODEL, 2 * D_EXPERT), exp3), _const_spec((DEINT_BLOCK, DEINT_BLOCK))],
        out_specs=pl.BlockSpec((1, D_MODEL, 2 * D_EXPERT), exp3),
        out_shape=jax.ShapeDtypeStruct((N_EXPERTS, D_MODEL, 2 * D_EXPERT), BF16),
        compiler_params=_cparams(("parallel",)),
    )(w1, perm)


SCATTER_BATCH = 8


def _moe_kernel(cnt_ref, off_ref, tok_ref, wt_ref, src_ref, w1_ref, w2_ref, bias_ref,
                out_hbm, acc_ref, gbuf_ref, ybuf_ref, state_ref, sem, *, nb_tokens, tile):
    blk = pl.program_id(0)
    e = pl.program_id(1)
    acc_rows = nb_tokens * ROW_SUBLANES
    buf_rows = tile * ROW_SUBLANES

    def gather(base, half):
        row0 = half * buf_rows
        for i in range(tile):
            tok = tok_ref[0, 0, base + i]
            gbuf_ref[pl.ds(pl.multiple_of(row0 + i * ROW_SUBLANES, ROW_SUBLANES), ROW_SUBLANES), :] = (
                src_ref[pl.ds(pl.multiple_of(tok * ROW_SUBLANES, ROW_SUBLANES), ROW_SUBLANES), :])

    def scatter(base, nvalid, half):
        row0 = half * buf_rows
        for i0 in range(0, tile, SCATTER_BATCH):
            sums, dsts = [], []
            for i in range(i0, i0 + SCATTER_BATCH):
                live = i < nvalid
                tok = jnp.where(live, tok_ref[0, 0, base + i], nb_tokens)
                w = jnp.where(live, wt_ref[0, 0, base + i], 0.0)
                dst = pl.ds(pl.multiple_of(tok * ROW_SUBLANES, ROW_SUBLANES), ROW_SUBLANES)
                src = pl.ds(pl.multiple_of(row0 + i * ROW_SUBLANES, ROW_SUBLANES), ROW_SUBLANES)
                sums.append(acc_ref[dst, :] + w * ybuf_ref[src, :])
                dsts.append(dst)
            for dst, s in zip(dsts, sums):
                acc_ref[dst, :] = s

    def writeback(block):
        return pltpu.make_async_copy(acc_ref.at[pl.ds(0, acc_rows), :],
                                     out_hbm.at[pl.ds(block * acc_rows, acc_rows), :], sem)

    @pl.when(e == 0)
    def _():
        @pl.when(blk > 0)
        def _():
            writeback(blk - 1).wait()

        acc_ref[...] = jnp.zeros_like(acc_ref)
        ybuf_ref[...] = jnp.zeros_like(ybuf_ref)
        state_ref[0] = 0
        state_ref[1] = 0
        state_ref[2] = 0
        gather(0, 0)

    n = cnt_ref[blk * N_EXPERTS + e]
    start = off_ref[blk * N_EXPERTS + e]
    ntiles = (n + tile - 1) // tile

    def tile_body(ti, carry):
        prev_base, prev_n, half = carry
        base = start + ti * tile
        other = 1 - half
        gather(jnp.where(ti + 1 < ntiles, base + tile, start + n), other)
        row0 = half * buf_rows
        x = jnp.concatenate(
            [gbuf_ref[pl.ds(row0 + j, tile, stride=ROW_SUBLANES), :] for j in range(ROW_SUBLANES)],
            axis=1).astype(BF16)
        hg = jnp.dot(x, w1_ref[0, :, :D_EXPERT], preferred_element_type=F32) + bias_ref[0, :, :D_EXPERT]
        hl = (jnp.dot(x, w1_ref[0, :, D_EXPERT:], preferred_element_type=F32)
              + bias_ref[0, :, D_EXPERT:2 * D_EXPERT])
        xg = jnp.minimum(hg, SWIGLU_LIMIT)
        xl = jnp.clip(hl, -SWIGLU_LIMIT, SWIGLU_LIMIT)
        act = xg * _sigmoid(SWIGLU_ALPHA * xg) * (xl + 1.0)
        y = jnp.dot(act.astype(BF16), w2_ref[0], preferred_element_type=F32) + bias_ref[0, :, 2 * D_EXPERT:]
        scatter(prev_base, prev_n, other)
        for j in range(ROW_SUBLANES):
            ybuf_ref[pl.ds(row0 + j, tile, stride=ROW_SUBLANES), :] = y[:, j * LANES:(j + 1) * LANES]
        return base, n - ti * tile, other

    prev_base, prev_n, half = lax.fori_loop(0, ntiles, tile_body,
                                            (state_ref[0], state_ref[1], state_ref[2]))
    state_ref[0] = prev_base
    state_ref[1] = prev_n
    state_ref[2] = half

    @pl.when(e == N_EXPERTS - 1)
    def _():
        scatter(prev_base, prev_n, 1 - half)
        writeback(blk).start()

        @pl.when(blk == pl.num_programs(0) - 1)
        def _():
            writeback(blk).wait()


def _moe(counts, offsets, tok_sorted, wt_sorted, tsl, w1d, w2, bias, t, nb_tokens, tile):
    nblk = t // nb_tokens
    slots = tok_sorted.shape[-1]
    exp3 = lambda b, e, *_: (e, 0, 0)
    blk3 = lambda b, e, *_: (b, 0, 0)
    grid_spec = pltpu.PrefetchScalarGridSpec(
        num_scalar_prefetch=2,
        grid=(nblk, N_EXPERTS),
        in_specs=[
            pl.BlockSpec((1, 1, slots), blk3, memory_space=pltpu.SMEM),
            pl.BlockSpec((1, 1, slots), blk3, memory_space=pltpu.SMEM),
            pl.BlockSpec((nb_tokens * ROW_SUBLANES, LANES), lambda b, e, *_: (b, 0),
                         pipeline_mode=pl.Buffered(1)),
            pl.BlockSpec((1, D_MODEL, 2 * D_EXPERT), exp3),
            pl.BlockSpec((1, D_EXPERT, D_MODEL), exp3),
            pl.BlockSpec((1, 1, 2 * D_EXPERT + D_MODEL), exp3),
        ],
        out_specs=pl.BlockSpec(memory_space=pl.ANY),
        scratch_shapes=[
            pltpu.VMEM(((nb_tokens + 1) * ROW_SUBLANES, LANES), F32),
            pltpu.VMEM((2 * tile * ROW_SUBLANES, LANES), F32),
            pltpu.VMEM((2 * tile * ROW_SUBLANES, LANES), F32),
            pltpu.SMEM((3,), jnp.int32),
            pltpu.SemaphoreType.DMA,
        ],
    )
    return pl.pallas_call(
        functools.partial(_moe_kernel, nb_tokens=nb_tokens, tile=tile),
        grid_spec=grid_spec,
        out_shape=jax.ShapeDtypeStruct((t * ROW_SUBLANES, LANES), F32),
        compiler_params=_cparams(("arbitrary", "arbitrary")),
    )(counts, offsets, tok_sorted, wt_sorted, tsl, w1d, w2, bias)


def _route(idx, wts, t, nb_tokens, tile):
    nblk = t // nb_tokens
    eid = idx[:, :TOP_K].reshape(nblk, nb_tokens * TOP_K)
    w = wts[:, :TOP_K].reshape(nblk, nb_tokens * TOP_K)
    order = jnp.argsort(eid, axis=1, stable=True)
    tok_sorted = (order // TOP_K).astype(jnp.int32)
    wt_sorted = jnp.take_along_axis(w, order, axis=1)
    counts = jnp.sum((eid[:, :, None] == jnp.arange(N_EXPERTS)[None, None, :]).astype(jnp.int32), axis=1)
    offsets = jnp.cumsum(counts, axis=1) - counts
    pad = ((0, 0), (0, tile))
    tok_sorted = jnp.pad(tok_sorted, pad)[:, None, :]
    wt_sorted = jnp.pad(wt_sorted, pad)[:, None, :]
    return counts.reshape(-1), offsets.reshape(-1).astype(jnp.int32), tok_sorted, wt_sorted


def _residual_kernel(x1_ref, moe_ref, o_ref, *, tm):
    parts = [moe_ref[pl.ds(j, tm, stride=ROW_SUBLANES), :] for j in range(ROW_SUBLANES)]
    o_ref[...] = x1_ref[...] + jnp.concatenate(parts, axis=1)


def _residual(x1, moe, tm):
    t = x1.shape[0]
    return pl.pallas_call(
        functools.partial(_residual_kernel, tm=tm),
        grid=(t // tm,),
        in_specs=[pl.BlockSpec((tm, D_MODEL), lambda i: (i, 0)),
                  pl.BlockSpec((tm * ROW_SUBLANES, LANES), lambda i: (i, 0))],
        out_specs=pl.BlockSpec((tm, D_MODEL), lambda i: (i, 0)),
        out_shape=jax.ShapeDtypeStruct((t, D_MODEL), F32),
        compiler_params=_cparams(("parallel",)),
    )(x1, moe)


def kernel(x, mem, positions, attn_norm_g, mem_norm_g, w_in, b_gate, ssm_log_dt, ssm_a_re, ssm_a_im,
           ssm_b_re, ssm_b_im, ssm_c_re, ssm_c_im, ssm_d, w_glu_v, w_glu_g, swa_q_gain, swa_k_gain,
           swa_sinks, w_mem_kv, mem_q_gain, mem_k_gain, w_br_ssm, w_br_swa, w_br_mem, w_out,
           ffn_norm_g, w_router, b_router, w_mlp1, b_mlp1, w_mlp2, b_mlp2):
    bsz, seq, _ = x.shape
    depth = w_in.shape[0]
    t = bsz * seq
    tm = min(512, seq)
    tq = min(512, seq)
    steps = min(32, seq)

    pos_rows = jnp.broadcast_to(positions.astype(F32).reshape(t // tq, 1, tq),
                                (t // tq, SUBLANES, tq)).reshape(t // tq * SUBLANES, tq)
    mem2 = mem.reshape(bsz * MEM_LEN, D_MODEL)

    x2 = x.reshape(t, D_MODEL)
    for l in range(depth):
        row = lambda a: a[l].reshape(1, -1).astype(F32)
        u_tm, q, k, v, qm, gates = _in_projection(
            x2, row(attn_norm_g), w_in[l].astype(BF16), row(b_gate), bsz, seq, tm)

        bmat, abr, abi, cmat = _s5_discretize(ssm_log_dt[l], ssm_a_re[l], ssm_a_im[l], ssm_b_re[l],
                                              ssm_b_im[l], ssm_c_re[l], ssm_c_im[l])
        y_ssm_tm = _s5_branch(u_tm, bmat, abr, abi, cmat, row(ssm_d),
                              w_glu_v[l].astype(BF16), w_glu_g[l].astype(BF16), bsz, seq, steps)

        mk, mv = _mem_kv(mem2, row(mem_norm_g), w_mem_kv[l].astype(BF16), row(mem_k_gain), bsz)
        y_swa, y_mem = _attention(
            swa_sinks[l].astype(F32), q, k, v, pos_rows, qm, mk, mv, swa_q_gain[l], swa_k_gain[l],
            row(mem_q_gain), bsz, seq, tq)

        wr = jnp.pad(w_router[l].astype(F32), ((0, 0), (0, LANES - N_EXPERTS)))
        wr_hi = wr.astype(BF16)
        wr_lo = (wr - wr_hi.astype(F32)).astype(BF16)
        br = jnp.pad(b_router[l].astype(F32), (0, LANES - N_EXPERTS), constant_values=NEG_BIG).reshape(1, LANES)
        x1, tsl, idx, wts = _merge(x2, y_ssm_tm, y_swa, y_mem, gates, w_br_ssm[l].astype(BF16),
                                   w_br_swa[l].astype(BF16), w_br_mem[l].astype(BF16),
                                   w_out[l].astype(BF16), row(ffn_norm_g), jnp.concatenate([wr_hi, wr_lo], axis=1), br, bsz, seq, tm)

        nb_tokens = min(MOE_TOKEN_BLOCK, t)
        idx, w1_l, w2_l = lax.optimization_barrier((idx, w_mlp1[l], w_mlp2[l]))
        counts, offsets, tok_sorted, wt_sorted = _route(idx, wts, t, nb_tokens, MOE_TILE)
        b1 = b_mlp1[l].astype(F32)
        bias = jnp.concatenate([b1[:, 0::2], b1[:, 1::2], b_mlp2[l].astype(F32)], axis=1)[:, None, :]
        moe = _moe(counts, offsets, tok_sorted, wt_sorted, tsl, _expert_prep(w1_l), w2_l.astype(BF16), bias,
                   t, nb_tokens, MOE_TILE)
        x2 = _residual(x1, moe, tm)
    return x2.reshape(bsz, seq, D_MODEL)
```

```python
import functools
import math

import jax
import jax.numpy as jnp
from jax import lax
from jax.experimental import pallas as pl
from jax.experimental.pallas import tpu as pltpu

F32 = jnp.float32
BF16 = jnp.bfloat16

D_MODEL = 1024
MEM_LEN = 256
NORM_EPS = 1e-5
QK_EPS = 1e-6

SSM_WIDTH = 512
SSM_GROUP = 16
SSM_GROUPS = 32
SSM_STATE = 64
SSM_COMPLEX = SSM_GROUPS * SSM_STATE
S5_SLABS = SSM_WIDTH // 128
S5_SLAB_STATES = SSM_COMPLEX // S5_SLABS

HEAD_DIM = 64
SWA_KV_HEADS = 4
SWA_GROUP = 4
BLOCK = 128
ROPE_THETA = 500000.0
ROPE_DIM = 16
ROPE_HALF = 8

MEM_HEADS = 4
MEM_HEAD_DIM = 128

SWA_Q_WIDTH = 1024
SWA_KV_WIDTH = 256
MEM_WIDTH = 512
N_BRANCH = 3
GATE_WIDTH = N_BRANCH * D_MODEL
OFF_U, OFF_Q, OFF_K, OFF_V, OFF_QM, OFF_G, OFF_END = 0, 512, 1536, 1792, 2048, 2560, 5632

N_EXPERTS = 32
TOP_K = 4
D_EXPERT = 1024
SWIGLU_ALPHA = 1.702
SWIGLU_LIMIT = 7.0

LANES = 128
SUBLANES = 8
VMEM_LIMIT = 56 * 1024 * 1024
NEG_BIG = -1e30

ROW_SUBLANES = D_MODEL // LANES
MOE_TOKEN_BLOCK = 4096
MOE_TILE = 272
MOE_SMALL_TILE = 144


def _sigmoid(x):
    return 1.0 / (1.0 + jnp.exp(-x))


def _cparams(sem):
    return pltpu.CompilerParams(dimension_semantics=sem, vmem_limit_bytes=VMEM_LIMIT)


def _const_spec(shape):
    nd = len(shape)
    return pl.BlockSpec(shape, lambda *_: (0,) * nd)


def _inproj_kernel(x_ref, g_ref, w_ref, bg_ref, u_ref, q_ref, k_ref, v_ref, qm_ref, gate_ref):
    x = x_ref[...]
    h = x * lax.rsqrt(jnp.mean(x * x, axis=-1, keepdims=True) + NORM_EPS) * g_ref[...]
    hb = h.astype(BF16)

    def proj(lo, hi):
        return jnp.dot(hb, w_ref[:, lo:hi], preferred_element_type=F32)

    u_ref[...] = proj(OFF_U, OFF_Q).astype(BF16)
    q_ref[...] = proj(OFF_Q, OFF_K).astype(BF16)
    k_ref[...] = proj(OFF_K, OFF_V).astype(BF16)
    v_ref[...] = proj(OFF_V, OFF_QM).astype(BF16)
    qm_ref[...] = proj(OFF_QM, OFF_G).astype(BF16)
    gate_ref[...] = _sigmoid(proj(OFF_G, OFF_END) + bg_ref[...]).astype(BF16)


def _in_projection(x2, g, w_in_b, b_gate, bsz, seq, tm):
    t = bsz * seq
    nj = seq // tm
    row = lambda b, j: (b * nj + j, 0)
    return pl.pallas_call(
        _inproj_kernel,
        grid=(bsz, nj),
        in_specs=[
            pl.BlockSpec((tm, D_MODEL), row),
            _const_spec((1, D_MODEL)),
            _const_spec((D_MODEL, OFF_END)),
            _const_spec((1, GATE_WIDTH)),
        ],
        out_specs=[
            pl.BlockSpec((tm, SSM_WIDTH), lambda b, j: (j, b)),
            pl.BlockSpec((tm, SWA_Q_WIDTH), row),
            pl.BlockSpec((tm, SWA_KV_WIDTH), row),
            pl.BlockSpec((tm, SWA_KV_WIDTH), row),
            pl.BlockSpec((tm, MEM_WIDTH), row),
            pl.BlockSpec((tm, GATE_WIDTH), row),
        ],
        out_shape=[
            jax.ShapeDtypeStruct((seq, bsz * SSM_WIDTH), BF16),
            jax.ShapeDtypeStruct((t, SWA_Q_WIDTH), BF16),
            jax.ShapeDtypeStruct((t, SWA_KV_WIDTH), BF16),
            jax.ShapeDtypeStruct((t, SWA_KV_WIDTH), BF16),
            jax.ShapeDtypeStruct((t, MEM_WIDTH), BF16),
            jax.ShapeDtypeStruct((t, GATE_WIDTH), BF16),
        ],
        compiler_params=_cparams(("parallel", "parallel")),
    )(x2, g, w_in_b, b_gate)


def _gelu_tanh(x):
    c = math.sqrt(2.0 / math.pi)
    return 0.5 * x * (1.0 + jnp.tanh(c * (x + 0.044715 * (x * x * x))))


def _s5_kernel(u_ref, perm_ref, permt_ref, bmat_ref, are_ref, aim_ref, cmat_ref, d_ref, wv_ref, wg_ref,
               o_ref, st_ref, bu_ref, *, bsz, steps, col_chunk):
    nc = SSM_COMPLEX

    @pl.when(pl.program_id(0) == 0)
    def _():
        st_ref[...] = jnp.zeros_like(st_ref)

    u_bt = jnp.concatenate([u_ref[:, b * SSM_WIDTH:(b + 1) * SSM_WIDTH] for b in range(bsz)], axis=0)
    u = jnp.dot(perm_ref[...], u_bt, preferred_element_type=F32).astype(BF16)
    for m in range(S5_SLABS):
        um = u[:, m * LANES:(m + 1) * LANES]
        re = slice(m * S5_SLAB_STATES, (m + 1) * S5_SLAB_STATES)
        im = slice(nc + m * S5_SLAB_STATES, nc + (m + 1) * S5_SLAB_STATES)
        bu_ref[:, re] = jnp.dot(um, bmat_ref[m, :, :S5_SLAB_STATES], preferred_element_type=F32)
        bu_ref[:, im] = jnp.dot(um, bmat_ref[m, :, S5_SLAB_STATES:], preferred_element_type=F32)

    for lo in range(0, nc, col_chunk):
        re = slice(lo, lo + col_chunk)
        im = slice(nc + lo, nc + lo + col_chunk)
        ar = jnp.broadcast_to(are_ref[:, re], (bsz, col_chunk))
        ai = jnp.broadcast_to(aim_ref[:, re], (bsz, col_chunk))

        def step(t, carry):
            sr, si = carry
            rows = pl.ds(pl.multiple_of(t * bsz, bsz), bsz)
            nr = ar * sr - ai * si + bu_ref[rows, re]
            ni = ar * si + ai * sr + bu_ref[rows, im]
            bu_ref[rows, re] = nr
            bu_ref[rows, im] = ni
            return nr, ni

        sr, si = lax.fori_loop(0, steps, step, (st_ref[:, re], st_ref[:, im]))
        st_ref[:, re] = sr
        st_ref[:, im] = si

    ys = []
    for m in range(S5_SLABS):
        re = slice(m * S5_SLAB_STATES, (m + 1) * S5_SLAB_STATES)
        im = slice(nc + m * S5_SLAB_STATES, nc + (m + 1) * S5_SLAB_STATES)
        ys.append(jnp.dot(bu_ref[:, re].astype(BF16), cmat_ref[m, :S5_SLAB_STATES, :],
                          preferred_element_type=F32)
                  + jnp.dot(bu_ref[:, im].astype(BF16), cmat_ref[m, S5_SLAB_STATES:, :],
                            preferred_element_type=F32))
    y = jnp.concatenate(ys, axis=1) + d_ref[...] * u.astype(F32)
    yb = _gelu_tanh(y).astype(BF16)
    val = jnp.dot(yb, wv_ref[...], preferred_element_type=F32)
    gate = jnp.dot(yb, wg_ref[...], preferred_element_type=F32)
    out = (val * _sigmoid(gate)).astype(BF16)
    out_bt = jnp.dot(permt_ref[...], out, preferred_element_type=F32).astype(BF16)
    for b in range(bsz):
        o_ref[:, b * SSM_WIDTH:(b + 1) * SSM_WIDTH] = out_bt[b * steps:(b + 1) * steps]


def _s5_branch(u_tm, bmat, a_re, a_im, cmat, d_skip, wv, wg, bsz, seq, steps):
    rows = steps * bsz
    kern = functools.partial(_s5_kernel, bsz=bsz, steps=steps, col_chunk=512)
    r = jnp.arange(rows)
    perm = (((r % bsz) * steps + r // bsz)[:, None] == r[None, :]).astype(BF16)
    return pl.pallas_call(
        kern,
        grid=(seq // steps,),
        in_specs=[
            pl.BlockSpec((steps, bsz * SSM_WIDTH), lambda i: (i, 0)),
            _const_spec((rows, rows)),
            _const_spec((rows, rows)),
            _const_spec((S5_SLABS, LANES, 2 * S5_SLAB_STATES)),
            _const_spec((1, SSM_COMPLEX)),
            _const_spec((1, SSM_COMPLEX)),
            _const_spec((S5_SLABS, 2 * S5_SLAB_STATES, LANES)),
            _const_spec((1, SSM_WIDTH)),
            _const_spec((SSM_WIDTH, SSM_WIDTH)),
            _const_spec((SSM_WIDTH, SSM_WIDTH)),
        ],
        out_specs=pl.BlockSpec((steps, bsz * SSM_WIDTH), lambda i: (i, 0)),
        out_shape=jax.ShapeDtypeStruct((seq, bsz * SSM_WIDTH), BF16),
        scratch_shapes=[
            pltpu.VMEM((bsz, 2 * SSM_COMPLEX), F32),
            pltpu.VMEM((rows, 2 * SSM_COMPLEX), F32),
        ],
        compiler_params=_cparams(("arbitrary",)),
    )(u_tm, perm, perm.T, bmat, a_re, a_im, cmat, d_skip, wv, wg)


def _s5_discretize(log_dt, a_re, a_im, b_re, b_im, c_re, c_im):
    dt = jnp.exp(log_dt.astype(F32))[:, None]
    lr = jnp.minimum(a_re.astype(F32), -1e-4)
    li = a_im.astype(F32)
    mag = jnp.exp(lr * dt)
    abr = mag * jnp.cos(li * dt)
    abi = mag * jnp.sin(li * dt)
    nr, ni = abr - 1.0, abi
    den = lr * lr + li * li
    fr = (nr * lr + ni * li) / den
    fi = (ni * lr - nr * li) / den
    br, bi = b_re.astype(F32), b_im.astype(F32)
    bbr = fr[..., None] * br - fi[..., None] * bi
    bbi = fr[..., None] * bi + fi[..., None] * br
    gps = SSM_GROUPS // S5_SLABS
    eye = jnp.eye(gps, dtype=F32)

    def blockdiag_in(m):
        m = m.reshape(S5_SLABS, gps, SSM_STATE, SSM_GROUP)
        return jnp.einsum('sgpc,gh->sgchp', m, eye).reshape(S5_SLABS, LANES, S5_SLAB_STATES)

    def blockdiag_out(m):
        m = m.reshape(S5_SLABS, gps, SSM_GROUP, SSM_STATE)
        return jnp.einsum('sgcp,gh->sgphc', m, eye).reshape(S5_SLABS, S5_SLAB_STATES, LANES)

    bmat = jnp.concatenate([blockdiag_in(bbr), blockdiag_in(bbi)], axis=2).astype(BF16)
    cmat = jnp.concatenate([blockdiag_out(c_re.astype(F32)), -blockdiag_out(c_im.astype(F32))],
                           axis=1).astype(BF16)
    return bmat, abr.reshape(1, SSM_COMPLEX), abi.reshape(1, SSM_COMPLEX), cmat


def _memkv_kernel(mem_ref, g_ref, w_ref, kg_ref, mk_ref, mv_ref):
    x = mem_ref[...]
    h = x * lax.rsqrt(jnp.mean(x * x, axis=-1, keepdims=True) + NORM_EPS) * g_ref[...]
    kv = jnp.dot(h.astype(BF16), w_ref[...], preferred_element_type=F32)
    parts = []
    for hd in range(MEM_HEADS):
        kh = kv[:, hd * MEM_HEAD_DIM:(hd + 1) * MEM_HEAD_DIM]
        kh = kh * lax.rsqrt(jnp.mean(kh * kh, axis=-1, keepdims=True) + QK_EPS) * kg_ref[...]
        parts.append(kh)
    mk_ref[...] = jnp.concatenate(parts, axis=1).astype(BF16)
    mv_ref[...] = kv[:, MEM_WIDTH:].astype(BF16)


def _mem_kv(mem2, g, w_b, k_gain, bsz):
    return pl.pallas_call(
        _memkv_kernel,
        grid=(bsz,),
        in_specs=[
            pl.BlockSpec((MEM_LEN, D_MODEL), lambda b: (b, 0)),
            _const_spec((1, D_MODEL)),
            _const_spec((D_MODEL, 2 * MEM_WIDTH)),
            _const_spec((1, MEM_HEAD_DIM)),
        ],
        out_specs=[
            pl.BlockSpec((MEM_LEN, MEM_WIDTH), lambda b: (b, 0)),
            pl.BlockSpec((MEM_LEN, MEM_WIDTH), lambda b: (b, 0)),
        ],
        out_shape=[
            jax.ShapeDtypeStruct((bsz * MEM_LEN, MEM_WIDTH), BF16),
            jax.ShapeDtypeStruct((bsz * MEM_LEN, MEM_WIDTH), BF16),
        ],
        compiler_params=_cparams(("parallel",)),
    )(mem2, g, w_b, k_gain)


HEADS_PER_COL = LANES // HEAD_DIM
ROPE_CONST_ROWS = 16


def _split_bf16(a):
    hi = a.astype(BF16)
    lo = (a - hi.astype(F32)).astype(BF16)
    return hi, lo


def _rope_tables(pos_row, rc_ref, re_ref):
    tq = pos_row.shape[1]
    freq = jnp.concatenate([rc_ref[0:ROPE_HALF, :]] * (tq // LANES), axis=1)
    ang = freq * pos_row
    trig = jnp.concatenate([jnp.cos(ang), jnp.sin(ang),
                            jnp.zeros((LANES - 2 * ROPE_HALF, tq), F32)], axis=0)
    tab = jnp.dot(trig.T, re_ref[...], precision=lax.Precision.HIGHEST, preferred_element_type=F32)
    return tab[:, 0:LANES] + rc_ref[ROPE_HALF:ROPE_HALF + 1, :], tab[:, LANES:2 * LANES]


def _attn_kernel(sink_ref, q_ref, k_ref, v_ref, pos_ref, qm_ref, mk_ref, mv_ref,
                 rc_ref, re_ref, perm_ref, seg_ref, gmq_ref, oswa_ref, omem_ref, kbuf_ref, vbuf_ref, *, tq):
    nblk = tq // BLOCK
    first_tile = pl.program_id(1) == 0
    dot = lambda a, b: jnp.dot(a, b, preferred_element_type=F32)

    nver = kbuf_ref.shape[0] // 2
    mine = (pl.program_id(1) % 2) * nver
    other = nver - mine

    @pl.when(first_tile)
    def _():
        kbuf_ref[0:nver, 0:BLOCK, :] = jnp.zeros((nver, BLOCK, LANES), BF16)
        vbuf_ref[0:nver, 0:BLOCK, :] = jnp.zeros((nver, BLOCK, 2 * LANES), BF16)
        vbuf_ref[:, :, LANES:2 * LANES] = jnp.ones((2 * nver, BLOCK + tq, LANES), BF16)

    cos_t, sin_t = _rope_tables(pos_ref[0:1, :], rc_ref, re_ref)
    seg = seg_ref[...]
    perm = perm_ref[...]
    lo_half_pk = lax.broadcasted_iota(jnp.int32, (tq, LANES), 1) < HEAD_DIM
    lo_half_blk = lax.broadcasted_iota(jnp.int32, (BLOCK, LANES), 1) < HEAD_DIM

    def norm_rope(raw_bf, gain_cos, gain_sin):
        raw = raw_bf.astype(F32)
        hi, lo = _split_bf16(raw * raw)
        rs = lax.rsqrt((dot(hi, seg) + dot(lo, seg)) * (1.0 / HEAD_DIM) + QK_EPS)
        return (raw * gain_cos + dot(raw_bf, perm) * gain_sin) * rs

    q_cos = cos_t * rc_ref[ROPE_HALF + 1:ROPE_HALF + 2, :]
    q_sin = sin_t * rc_ref[ROPE_HALF + 2:ROPE_HALF + 3, :]
    k_cos = cos_t * rc_ref[ROPE_HALF + 3:ROPE_HALF + 4, :]
    k_sin = sin_t * rc_ref[ROPE_HALF + 4:ROPE_HALF + 5, :]

    for col in range(SWA_KV_WIDTH // LANES):
        cs = slice(col * LANES, (col + 1) * LANES)
        kc = norm_rope(k_ref[:, cs], k_cos, k_sin)
        versions = ((kbuf_ref, slice(None), kc.astype(BF16), pltpu.roll(kc, HEAD_DIM, 1).astype(BF16)),
                    (vbuf_ref, slice(0, LANES), v_ref[:, cs],
                     pltpu.roll(v_ref[:, cs].astype(F32), HEAD_DIM, 1).astype(BF16)))
        for buf, lanes, plain, rotated in versions:
            for ver, val in ((2 * col, plain), (2 * col + 1, rotated)):
                buf[mine + ver, BLOCK:BLOCK + tq, lanes] = val
                buf[other + ver, 0:BLOCK, lanes] = val[tq - BLOCK:tq]

    q_half = []
    for col in range(SWA_Q_WIDTH // LANES):
        cs = slice(col * LANES, (col + 1) * LANES)
        qc = norm_rope(q_ref[:, cs], q_cos, q_sin)
        q_half.append((jnp.where(lo_half_pk, qc, 0.0).astype(BF16), jnp.where(lo_half_pk, 0.0, qc).astype(BF16)))

    qi = lax.broadcasted_iota(jnp.int32, (2 * BLOCK, BLOCK), 0) & (BLOCK - 1)
    kj = lax.broadcasted_iota(jnp.int32, (2 * BLOCK, BLOCK), 1)
    own = kj <= qi
    no_prev = jnp.where(first_tile, 1, 0).astype(F32) * NEG_BIG

    for n in range(nblk):
        r0 = n * BLOCK
        for h in range(SWA_KV_HEADS):
            by_half = []
            for half in range(HEADS_PER_COL):
                ver = 2 * (h // HEADS_PER_COL) + (0 if half == h % HEADS_PER_COL else 1)
                qs = jnp.concatenate([q_half[2 * h][half][r0:r0 + BLOCK],
                                      q_half[2 * h + 1][half][r0:r0 + BLOCK]], axis=0)
                s2 = lax.dot_general(qs, kbuf_ref[mine + ver, r0:r0 + 2 * BLOCK, :], (((1,), (1,)), ((), ())),
                                     preferred_element_type=F32)
                s_prev = s2[:, :BLOCK] + no_prev if n == 0 else s2[:, :BLOCK]
                s = jnp.where(own, s2[:, BLOCK:], s_prev)
                sink = jnp.concatenate(
                    [jnp.full((BLOCK, BLOCK), sink_ref[h * SWA_GROUP + half], F32),
                     jnp.full((BLOCK, BLOCK), sink_ref[h * SWA_GROUP + half + HEADS_PER_COL], F32)], axis=0)
                m = jnp.maximum(jnp.broadcast_to(jnp.max(s, axis=-1, keepdims=True), s.shape), sink)
                p = jnp.exp(s - m)
                p2 = jnp.concatenate([jnp.where(own, 0.0, p).astype(BF16),
                                      jnp.where(own, p, 0.0).astype(BF16)], axis=1)
                o2 = dot(p2, vbuf_ref[mine + ver, r0:r0 + 2 * BLOCK, :])
                by_half.append(o2[:, :LANES] / (o2[:, LANES:] + jnp.exp(sink - m)))
            for sub in range(2):
                rs_ = slice(sub * BLOCK, (sub + 1) * BLOCK)
                colv = jnp.where(lo_half_blk, by_half[0][rs_], by_half[1][rs_])
                oswa_ref[r0:r0 + BLOCK, (2 * h + sub) * LANES:(2 * h + sub + 1) * LANES] = colv.astype(BF16)

    ones = jnp.ones((MEM_LEN, LANES), BF16)
    outs = []
    for hd in range(MEM_HEADS):
        sl = slice(hd * MEM_HEAD_DIM, (hd + 1) * MEM_HEAD_DIM)
        qh = qm_ref[:, sl].astype(F32)
        rs = lax.rsqrt(jnp.mean(qh * qh, axis=-1, keepdims=True) + QK_EPS)
        qh = qh * (rs * (MEM_HEAD_DIM ** -0.5)) * gmq_ref[...]
        s = lax.dot_general(qh.astype(BF16), mk_ref[:, sl], (((1,), (1,)), ((), ())),
                            preferred_element_type=F32)
        p = jnp.exp(s - jnp.max(s, axis=-1, keepdims=True)).astype(BF16)
        o2 = dot(p, jnp.concatenate([mv_ref[:, sl], ones], axis=1))
        outs.append(o2[:, :LANES] / o2[:, LANES:])
    omem_ref[...] = jnp.concatenate(outs, axis=1).astype(BF16)


def _rope_constants(q_gain, k_gain):
    inv_freq = ROPE_THETA ** (-jnp.arange(ROPE_HALF, dtype=F32) / ROPE_HALF)
    lane = jnp.arange(LANES) % HEAD_DIM
    j = jnp.arange(ROPE_HALF)[:, None]
    e_cos = ((lane[None, :] < ROPE_DIM) & (lane[None, :] % ROPE_HALF == j)).astype(F32)
    e_sin = (lane[None, :] == j + ROPE_HALF).astype(F32) - (lane[None, :] == j).astype(F32)
    ones = (lane >= ROPE_DIM).astype(F32)[None, :]
    src = jnp.where(lane < ROPE_HALF, jnp.arange(LANES) + ROPE_HALF, jnp.arange(LANES) - ROPE_HALF)
    rotary = lane < ROPE_DIM
    perm = ((jnp.arange(LANES)[:, None] == src[None, :]) & rotary[None, :]).astype(BF16)

    def gains(g, scale):
        col = jnp.tile(g.astype(F32).reshape(-1), HEADS_PER_COL) * scale
        return col[None, :], jnp.where(rotary, col[jnp.clip(src, 0, LANES - 1)], 0.0)[None, :]

    qg, qgp = gains(q_gain, HEAD_DIM ** -0.5)
    kg, kgp = gains(k_gain, 1.0)
    rows = jnp.concatenate([jnp.broadcast_to(inv_freq[:, None], (ROPE_HALF, LANES)), ones, qg, qgp, kg, kgp,
                            jnp.zeros((ROPE_CONST_ROWS - ROPE_HALF - 5, LANES), F32)], axis=0)
    zero8 = jnp.zeros((ROPE_HALF, LANES), F32)
    expand = jnp.concatenate([
        jnp.concatenate([e_cos, zero8], axis=1),
        jnp.concatenate([zero8, e_sin], axis=1),
        jnp.zeros((LANES - 2 * ROPE_HALF, 2 * LANES), F32)], axis=0)
    return rows, expand, perm


def _attention(sinks, q, k, v, pos_rows, qm, mk, mv, q_gain, k_gain, gmq, bsz, seq, tq):
    t = bsz * seq
    nj = seq // tq
    row = lambda b, j: (b * nj + j, 0)
    mem = lambda b, j: (b, 0)
    half = jnp.arange(LANES) // HEAD_DIM
    seg = (half[:, None] == half[None, :]).astype(BF16)
    kern = functools.partial(_attn_kernel, tq=tq)
    nver = 2 * (2 * SWA_KV_WIDTH // LANES)
    return pl.pallas_call(
        kern,
        grid=(bsz, nj),
        in_specs=[
            pl.BlockSpec(memory_space=pltpu.SMEM),
            pl.BlockSpec((tq, SWA_Q_WIDTH), row),
            pl.BlockSpec((tq, SWA_KV_WIDTH), row),
            pl.BlockSpec((tq, SWA_KV_WIDTH), row),
            pl.BlockSpec((SUBLANES, tq), row),
            pl.BlockSpec((tq, MEM_WIDTH), row),
            pl.BlockSpec((MEM_LEN, MEM_WIDTH), mem),
            pl.BlockSpec((MEM_LEN, MEM_WIDTH), mem),
            _const_spec((ROPE_CONST_ROWS, LANES)),
            _const_spec((LANES, 2 * LANES)),
            _const_spec((LANES, LANES)),
            _const_spec((LANES, LANES)),
            _const_spec((1, MEM_HEAD_DIM)),
        ],
        out_specs=[
            pl.BlockSpec((tq, SWA_Q_WIDTH), row),
            pl.BlockSpec((tq, MEM_WIDTH), row),
        ],
        out_shape=[
            jax.ShapeDtypeStruct((t, SWA_Q_WIDTH), BF16),
            jax.ShapeDtypeStruct((t, MEM_WIDTH), BF16),
        ],
        scratch_shapes=[
            pltpu.VMEM((nver, BLOCK + tq, LANES), BF16),
            pltpu.VMEM((nver, BLOCK + tq, 2 * LANES), BF16),
        ],
        compiler_params=_cparams(("parallel", "arbitrary")),
    )(sinks, q, k, v, pos_rows, qm, mk, mv, *_rope_constants(q_gain, k_gain), seg, gmq)


def _merge_kernel(x_ref, ys_ref, yw_ref, ym_ref, gate_ref, w0_ref, w1_ref, w2_ref, wo_ref, gf_ref,
                  wr_cat_ref, br_ref, x1_ref, tsl_ref, idx_ref, wt_ref, *, tm):
    gates = gate_ref[...]
    dot = lambda a, b: jnp.dot(a, b, preferred_element_type=F32)
    merged = (gates[:, 0:D_MODEL].astype(F32) * dot(ys_ref[...], w0_ref[...])
              + gates[:, D_MODEL:2 * D_MODEL].astype(F32) * dot(yw_ref[...], w1_ref[...])
              + gates[:, 2 * D_MODEL:].astype(F32) * dot(ym_ref[...], w2_ref[...]))
    x1 = x_ref[...] + dot(merged.astype(BF16), wo_ref[...])
    x1_ref[...] = x1
    t = x1 * lax.rsqrt(jnp.mean(x1 * x1, axis=-1, keepdims=True) + NORM_EPS) * gf_ref[...]

    for j in range(ROW_SUBLANES):
        tsl_ref[pl.ds(j, tm, stride=ROW_SUBLANES), :] = t[:, j * LANES:(j + 1) * LANES]

    t_hi, t_lo = _split_bf16(t)
    both = dot(t_hi, wr_cat_ref[...])
    logits = both[:, :LANES] + both[:, LANES:] + dot(t_lo, wr_cat_ref[:, :LANES]) + br_ref[...]
    lane = lax.broadcasted_iota(jnp.int32, logits.shape, 1).astype(F32)
    work = logits
    vals, firsts = [], []
    for _ in range(TOP_K):
        m = jnp.max(work, axis=-1, keepdims=True)
        first = jnp.min(jnp.where(work == m, lane, float(LANES)), axis=-1, keepdims=True)
        work = jnp.where(lane == first, NEG_BIG * 2.0, work)
        vals.append(m)
        firsts.append(first)
    exps = [jnp.exp(v - vals[0]) for v in vals]
    den = exps[0] + exps[1] + exps[2] + exps[3]
    idx_out = jnp.zeros_like(logits)
    wt_out = jnp.zeros_like(logits)
    for k in range(TOP_K):
        idx_out = jnp.where(lane == float(k), firsts[k], idx_out)
        wt_out = jnp.where(lane == float(k), exps[k] / den, wt_out)
    idx_ref[...] = idx_out.astype(jnp.int32)
    wt_ref[...] = wt_out


def _merge(x2, y_ssm_tm, y_swa, y_mem, gates, w0, w1, w2, wo, gf, wr_cat, br, bsz, seq, tm):
    t = bsz * seq
    nj = seq // tm
    row = lambda b, j: (b * nj + j, 0)
    return pl.pallas_call(
        functools.partial(_merge_kernel, tm=tm),
        grid=(bsz, nj),
        in_specs=[
            pl.BlockSpec((tm, D_MODEL), row),
            pl.BlockSpec((tm, SSM_WIDTH), lambda b, j: (j, b)),
            pl.BlockSpec((tm, SWA_Q_WIDTH), row),
            pl.BlockSpec((tm, MEM_WIDTH), row),
            pl.BlockSpec((tm, GATE_WIDTH), row),
            _const_spec((SSM_WIDTH, D_MODEL)),
            _const_spec((SWA_Q_WIDTH, D_MODEL)),
            _const_spec((MEM_WIDTH, D_MODEL)),
            _const_spec((D_MODEL, D_MODEL)),
            _const_spec((1, D_MODEL)),
            _const_spec((D_MODEL, 2 * LANES)),
            _const_spec((1, LANES)),
        ],
        out_specs=[
            pl.BlockSpec((tm, D_MODEL), row),
            pl.BlockSpec((tm * ROW_SUBLANES, LANES), row),
            pl.BlockSpec((tm, LANES), row),
            pl.BlockSpec((tm, LANES), row),
        ],
        out_shape=[
            jax.ShapeDtypeStruct((t, D_MODEL), F32),
            jax.ShapeDtypeStruct((t * ROW_SUBLANES, LANES), F32),
            jax.ShapeDtypeStruct((t, LANES), jnp.int32),
            jax.ShapeDtypeStruct((t, LANES), F32),
        ],
        compiler_params=_cparams(("parallel", "parallel")),
    )(x2, y_ssm_tm, y_swa, y_mem, gates, w0, w1, w2, wo, gf, wr_cat, br)


DEINT_BLOCK = 2 * LANES


def _expert_prep_kernel(w1_ref, perm_ref, w1d_ref):
    perm = perm_ref[...]
    for blk in range(2 * D_EXPERT // DEINT_BLOCK):
        cols = w1_ref[0, :, blk * DEINT_BLOCK:(blk + 1) * DEINT_BLOCK].astype(BF16)
        z = jnp.dot(cols, perm, preferred_element_type=F32).astype(BF16)
        w1d_ref[0, :, blk * LANES:(blk + 1) * LANES] = z[:, :LANES]
        w1d_ref[0, :, D_EXPERT + blk * LANES:D_EXPERT + (blk + 1) * LANES] = z[:, LANES:]


def _expert_prep(w1):
    src = jnp.arange(DEINT_BLOCK)
    dst = jnp.where(src % 2 == 0, src // 2, LANES + src // 2)
    perm = (dst[:, None] == jnp.arange(DEINT_BLOCK)[None, :]).astype(BF16)
    exp3 = lambda e: (e, 0, 0)
    return pl.pallas_call(
        _expert_prep_kernel,
        grid=(N_EXPERTS,),
        in_specs=[pl.BlockSpec((1, D_MODEL, 2 * D_EXPERT), exp3), _const_spec((DEINT_BLOCK, DEINT_BLOCK))],
        out_specs=pl.BlockSpec((1, D_MODEL, 2 * D_EXPERT), exp3),
        out_shape=jax.ShapeDtypeStruct((N_EXPERTS, D_MODEL, 2 * D_EXPERT), BF16),
        compiler_params=_cparams(("parallel",)),
    )(w1, perm)


SCATTER_BATCH = 8


def _moe_kernel(cnt_ref, off_ref, tok_ref, wt_ref, src_ref, w1_ref, w2_ref, bias_ref,
                out_hbm, acc_ref, gbuf_ref, ybuf_ref, state_ref, sem, *, nb_tokens, tile):
    blk = pl.program_id(0)
    e = pl.program_id(1)
    acc_rows = nb_tokens * ROW_SUBLANES
    buf_rows = tile * ROW_SUBLANES

    def gather(base, half):
        row0 = half * buf_rows
        for i in range(tile):
            tok = tok_ref[0, 0, base + i]
            gbuf_ref[pl.ds(pl.multiple_of(row0 + i * ROW_SUBLANES, ROW_SUBLANES), ROW_SUBLANES), :] = (
                src_ref[pl.ds(pl.multiple_of(tok * ROW_SUBLANES, ROW_SUBLANES), ROW_SUBLANES), :])

    def scatter(base, nvalid, half):
        row0 = half * buf_rows
        for i0 in range(0, tile, SCATTER_BATCH):
            sums, dsts = [], []
            for i in range(i0, i0 + SCATTER_BATCH):
                live = i < nvalid
                tok = jnp.where(live, tok_ref[0, 0, base + i], nb_tokens)
                w = jnp.where(live, wt_ref[0, 0, base + i], 0.0)
                dst = pl.ds(pl.multiple_of(tok * ROW_SUBLANES, ROW_SUBLANES), ROW_SUBLANES)
                src = pl.ds(pl.multiple_of(row0 + i * ROW_SUBLANES, ROW_SUBLANES), ROW_SUBLANES)
                sums.append(acc_ref[dst, :] + w * ybuf_ref[src, :])
                dsts.append(dst)
            for dst, s in zip(dsts, sums):
                acc_ref[dst, :] = s

    def writeback(block):
        return pltpu.make_async_copy(acc_ref.at[pl.ds(0, acc_rows), :],
                                     out_hbm.at[pl.ds(block * acc_rows, acc_rows), :], sem)

    @pl.when(e == 0)
    def _():
        @pl.when(blk > 0)
        def _():
            writeback(blk - 1).wait()

        acc_ref[...] = jnp.zeros_like(acc_ref)
        ybuf_ref[...] = jnp.zeros_like(ybuf_ref)
        state_ref[0] = 0
        state_ref[1] = 0
        state_ref[2] = 0
        gather(0, 0)

    n = cnt_ref[blk * N_EXPERTS + e]
    start = off_ref[blk * N_EXPERTS + e]
    ntiles = (n + tile - 1) // tile

    def stages(rows, next_base, prev_base, prev_n, half):
        other = 1 - half
        gather(next_base, other)
        row0 = half * buf_rows
        x = jnp.concatenate(
            [gbuf_ref[pl.ds(row0 + j, rows, stride=ROW_SUBLANES), :] for j in range(ROW_SUBLANES)],
            axis=1).astype(BF16)
        hg = jnp.dot(x, w1_ref[0, :, :D_EXPERT], preferred_element_type=F32) + bias_ref[0, :, :D_EXPERT]
        hl = (jnp.dot(x, w1_ref[0, :, D_EXPERT:], preferred_element_type=F32)
              + bias_ref[0, :, D_EXPERT:2 * D_EXPERT])
        xg = jnp.minimum(hg, SWIGLU_LIMIT)
        xl = jnp.clip(hl, -SWIGLU_LIMIT, SWIGLU_LIMIT)
        act = xg * _sigmoid(SWIGLU_ALPHA * xg) * (xl + 1.0)
        y = jnp.dot(act.astype(BF16), w2_ref[0], preferred_element_type=F32) + bias_ref[0, :, 2 * D_EXPERT:]
        scatter(prev_base, prev_n, other)
        for j in range(ROW_SUBLANES):
            ybuf_ref[pl.ds(row0 + j, rows, stride=ROW_SUBLANES), :] = y[:, j * LANES:(j + 1) * LANES]

    def tile_body(ti, carry):
        prev_base, prev_n, half = carry
        base = start + ti * tile
        live = n - ti * tile
        next_base = jnp.where(ti + 1 < ntiles, base + tile, start + n)
        lax.cond(live <= MOE_SMALL_TILE,
                 functools.partial(stages, MOE_SMALL_TILE, next_base, prev_base, prev_n, half),
                 functools.partial(stages, tile, next_base, prev_base, prev_n, half))
        return base, live, 1 - half

    prev_base, prev_n, half = lax.fori_loop(0, ntiles, tile_body,
                                            (state_ref[0], state_ref[1], state_ref[2]))
    state_ref[0] = prev_base
    state_ref[1] = prev_n
    state_ref[2] = half

    @pl.when(e == N_EXPERTS - 1)
    def _():
        scatter(prev_base, prev_n, 1 - half)
        writeback(blk).start()

        @pl.when(blk == pl.num_programs(0) - 1)
        def _():
            writeback(blk).wait()


def _moe(counts, offsets, tok_sorted, wt_sorted, tsl, w1d, w2, bias, t, nb_tokens, tile):
    nblk = t // nb_tokens
    slots = tok_sorted.shape[-1]
    exp3 = lambda b, e, *_: (e, 0, 0)
    blk3 = lambda b, e, *_: (b, 0, 0)
    grid_spec = pltpu.PrefetchScalarGridSpec(
        num_scalar_prefetch=2,
        grid=(nblk, N_EXPERTS),
        in_specs=[
            pl.BlockSpec((1, 1, slots), blk3, memory_space=pltpu.SMEM),
            pl.BlockSpec((1, 1, slots), blk3, memory_space=pltpu.SMEM),
            pl.BlockSpec((nb_tokens * ROW_SUBLANES, LANES), lambda b, e, *_: (b, 0),
                         pipeline_mode=pl.Buffered(1)),
            pl.BlockSpec((1, D_MODEL, 2 * D_EXPERT), exp3),
            pl.BlockSpec((1, D_EXPERT, D_MODEL), exp3),
            pl.BlockSpec((1, 1, 2 * D_EXPERT + D_MODEL), exp3),
        ],
        out_specs=pl.BlockSpec(memory_space=pl.ANY),
        scratch_shapes=[
            pltpu.VMEM(((nb_tokens + 1) * ROW_SUBLANES, LANES), F32),
            pltpu.VMEM((2 * tile * ROW_SUBLANES, LANES), F32),
            pltpu.VMEM((2 * tile * ROW_SUBLANES, LANES), F32),
            pltpu.SMEM((3,), jnp.int32),
            pltpu.SemaphoreType.DMA,
        ],
    )
    return pl.pallas_call(
        functools.partial(_moe_kernel, nb_tokens=nb_tokens, tile=tile),
        grid_spec=grid_spec,
        out_shape=jax.ShapeDtypeStruct((t * ROW_SUBLANES, LANES), F32),
        compiler_params=_cparams(("arbitrary", "arbitrary")),
    )(counts, offsets, tok_sorted, wt_sorted, tsl, w1d, w2, bias)


def _route(idx, wts, t, nb_tokens, tile):
    nblk = t // nb_tokens
    eid = idx[:, :TOP_K].reshape(nblk, nb_tokens * TOP_K)
    w = wts[:, :TOP_K].reshape(nblk, nb_tokens * TOP_K)
    order = jnp.argsort(eid, axis=1, stable=True)
    tok_sorted = (order // TOP_K).astype(jnp.int32)
    wt_sorted = jnp.take_along_axis(w, order, axis=1)
    counts = jnp.sum((eid[:, :, None] == jnp.arange(N_EXPERTS)[None, None, :]).astype(jnp.int32), axis=1)
    offsets = jnp.cumsum(counts, axis=1) - counts
    pad = ((0, 0), (0, tile))
    tok_sorted = jnp.pad(tok_sorted, pad)[:, None, :]
    wt_sorted = jnp.pad(wt_sorted, pad)[:, None, :]
    return counts.reshape(-1), offsets.reshape(-1).astype(jnp.int32), tok_sorted, wt_sorted


def _residual_kernel(x1_ref, moe_ref, o_ref, *, tm):
    parts = [moe_ref[pl.ds(j, tm, stride=ROW_SUBLANES), :] for j in range(ROW_SUBLANES)]
    o_ref[...] = x1_ref[...] + jnp.concatenate(parts, axis=1)


def _residual(x1, moe, tm):
    t = x1.shape[0]
    return pl.pallas_call(
        functools.partial(_residual_kernel, tm=tm),
        grid=(t // tm,),
        in_specs=[pl.BlockSpec((tm, D_MODEL), lambda i: (i, 0)),
                  pl.BlockSpec((tm * ROW_SUBLANES, LANES), lambda i: (i, 0))],
        out_specs=pl.BlockSpec((tm, D_MODEL), lambda i: (i, 0)),
        out_shape=jax.ShapeDtypeStruct((t, D_MODEL), F32),
        compiler_params=_cparams(("parallel",)),
    )(x1, moe)


def kernel(x, mem, positions, attn_norm_g, mem_norm_g, w_in, b_gate, ssm_log_dt, ssm_a_re, ssm_a_im,
           ssm_b_re, ssm_b_im, ssm_c_re, ssm_c_im, ssm_d, w_glu_v, w_glu_g, swa_q_gain, swa_k_gain,
           swa_sinks, w_mem_kv, mem_q_gain, mem_k_gain, w_br_ssm, w_br_swa, w_br_mem, w_out,
           ffn_norm_g, w_router, b_router, w_mlp1, b_mlp1, w_mlp2, b_mlp2):
    bsz, seq, _ = x.shape
    depth = w_in.shape[0]
    t = bsz * seq
    tm = min(512, seq)
    tq = min(512, seq)
    steps = min(32, seq)

    pos_rows = jnp.broadcast_to(positions.astype(F32).reshape(t // tq, 1, tq),
                                (t // tq, SUBLANES, tq)).reshape(t // tq * SUBLANES, tq)
    mem2 = mem.reshape(bsz * MEM_LEN, D_MODEL)

    x2 = x.reshape(t, D_MODEL)
    for l in range(depth):
        row = lambda a: a[l].reshape(1, -1).astype(F32)
        u_tm, q, k, v, qm, gates = _in_projection(
            x2, row(attn_norm_g), w_in[l].astype(BF16), row(b_gate), bsz, seq, tm)

        bmat, abr, abi, cmat = _s5_discretize(ssm_log_dt[l], ssm_a_re[l], ssm_a_im[l], ssm_b_re[l],
                                              ssm_b_im[l], ssm_c_re[l], ssm_c_im[l])
        y_ssm_tm = _s5_branch(u_tm, bmat, abr, abi, cmat, row(ssm_d),
                              w_glu_v[l].astype(BF16), w_glu_g[l].astype(BF16), bsz, seq, steps)

        mk, mv = _mem_kv(mem2, row(mem_norm_g), w_mem_kv[l].astype(BF16), row(mem_k_gain), bsz)
        y_swa, y_mem = _attention(
            swa_sinks[l].astype(F32), q, k, v, pos_rows, qm, mk, mv, swa_q_gain[l], swa_k_gain[l],
            row(mem_q_gain), bsz, seq, tq)

        wr = jnp.pad(w_router[l].astype(F32), ((0, 0), (0, LANES - N_EXPERTS)))
        wr_hi = wr.astype(BF16)
        wr_lo = (wr - wr_hi.astype(F32)).astype(BF16)
        br = jnp.pad(b_router[l].astype(F32), (0, LANES - N_EXPERTS), constant_values=NEG_BIG).reshape(1, LANES)
        x1, tsl, idx, wts = _merge(x2, y_ssm_tm, y_swa, y_mem, gates, w_br_ssm[l].astype(BF16),
                                   w_br_swa[l].astype(BF16), w_br_mem[l].astype(BF16),
                                   w_out[l].astype(BF16), row(ffn_norm_g), jnp.concatenate([wr_hi, wr_lo], axis=1), br, bsz, seq, tm)

        nb_tokens = min(MOE_TOKEN_BLOCK, t)
        idx, w1_l, w2_l = lax.optimization_barrier((idx, w_mlp1[l], w_mlp2[l]))
        counts, offsets, tok_sorted, wt_sorted = _route(idx, wts, t, nb_tokens, MOE_TILE)
        b1 = b_mlp1[l].astype(F32)
        bias = jnp.concatenate([b1[:, 0::2], b1[:, 1::2], b_mlp2[l].astype(F32)], axis=1)[:, None, :]
        moe = _moe(counts, offsets, tok_sorted, wt_sorted, tsl, _expert_prep(w1_l), w2_l.astype(BF16), bias,
                   t, nb_tokens, MOE_TILE)
        x2 = _residual(x1, moe, tm)
    return x2.reshape(bsz, seq, D_MODEL)
```

```python
import functools
import math

import jax
import jax.numpy as jnp
from jax import lax
from jax.experimental import pallas as pl
from jax.experimental.pallas import tpu as pltpu

F32 = jnp.float32
BF16 = jnp.bfloat16

D_MODEL = 1024
MEM_LEN = 256
NORM_EPS = 1e-5
QK_EPS = 1e-6

SSM_WIDTH = 512
SSM_GROUP = 16
SSM_GROUPS = 32
SSM_STATE = 64
SSM_COMPLEX = SSM_GROUPS * SSM_STATE
S5_SLABS = SSM_WIDTH // 128
S5_SLAB_STATES = SSM_COMPLEX // S5_SLABS

HEAD_DIM = 64
SWA_KV_HEADS = 4
SWA_GROUP = 4
BLOCK = 128
ROPE_THETA = 500000.0
ROPE_DIM = 16
ROPE_HALF = 8

MEM_HEADS = 4
MEM_HEAD_DIM = 128

SWA_Q_WIDTH = 1024
SWA_KV_WIDTH = 256
MEM_WIDTH = 512
N_BRANCH = 3
GATE_WIDTH = N_BRANCH * D_MODEL
OFF_U, OFF_Q, OFF_K, OFF_V, OFF_QM, OFF_G, OFF_END = 0, 512, 1536, 1792, 2048, 2560, 5632

N_EXPERTS = 32
TOP_K = 4
D_EXPERT = 1024
SWIGLU_ALPHA = 1.702
SWIGLU_LIMIT = 7.0

LANES = 128
SUBLANES = 8
VMEM_LIMIT = 56 * 1024 * 1024
NEG_BIG = -1e30

ROW_SUBLANES = D_MODEL // LANES
MOE_TOKEN_BLOCK = 4096
MOE_TILE = 272
MOE_SMALL_TILE = 144


def _sigmoid(x):
    return 1.0 / (1.0 + jnp.exp(-x))


def _cparams(sem):
    return pltpu.CompilerParams(dimension_semantics=sem, vmem_limit_bytes=VMEM_LIMIT)


def _const_spec(shape):
    nd = len(shape)
    return pl.BlockSpec(shape, lambda *_: (0,) * nd)


def _inproj_kernel(x_ref, g_ref, w_ref, bg_ref, u_ref, q_ref, k_ref, v_ref, qm_ref, gate_ref):
    x = x_ref[...]
    h = x * lax.rsqrt(jnp.mean(x * x, axis=-1, keepdims=True) + NORM_EPS) * g_ref[...]
    hb = h.astype(BF16)

    def proj(lo, hi):
        return jnp.dot(hb, w_ref[:, lo:hi], preferred_element_type=F32)

    u_ref[...] = proj(OFF_U, OFF_Q).astype(BF16)
    q_ref[...] = proj(OFF_Q, OFF_K).astype(BF16)
    k_ref[...] = proj(OFF_K, OFF_V).astype(BF16)
    v_ref[...] = proj(OFF_V, OFF_QM).astype(BF16)
    qm_ref[...] = proj(OFF_QM, OFF_G).astype(BF16)
    gate_ref[...] = _sigmoid(proj(OFF_G, OFF_END) + bg_ref[...]).astype(BF16)


def _in_projection(x2, g, w_in_b, b_gate, bsz, seq, tm):
    t = bsz * seq
    nj = seq // tm
    row = lambda b, j: (b * nj + j, 0)
    return pl.pallas_call(
        _inproj_kernel,
        grid=(bsz, nj),
        in_specs=[
            pl.BlockSpec((tm, D_MODEL), row),
            _const_spec((1, D_MODEL)),
            _const_spec((D_MODEL, OFF_END)),
            _const_spec((1, GATE_WIDTH)),
        ],
        out_specs=[
            pl.BlockSpec((tm, SSM_WIDTH), lambda b, j: (j, b)),
            pl.BlockSpec((tm, SWA_Q_WIDTH), row),
            pl.BlockSpec((tm, SWA_KV_WIDTH), row),
            pl.BlockSpec((tm, SWA_KV_WIDTH), row),
            pl.BlockSpec((tm, MEM_WIDTH), row),
            pl.BlockSpec((tm, GATE_WIDTH), row),
        ],
        out_shape=[
            jax.ShapeDtypeStruct((seq, bsz * SSM_WIDTH), BF16),
            jax.ShapeDtypeStruct((t, SWA_Q_WIDTH), BF16),
            jax.ShapeDtypeStruct((t, SWA_KV_WIDTH), BF16),
            jax.ShapeDtypeStruct((t, SWA_KV_WIDTH), BF16),
            jax.ShapeDtypeStruct((t, MEM_WIDTH), BF16),
            jax.ShapeDtypeStruct((t, GATE_WIDTH), BF16),
        ],
        compiler_params=_cparams(("parallel", "parallel")),
    )(x2, g, w_in_b, b_gate)


def _gelu_tanh(x):
    c = math.sqrt(2.0 / math.pi)
    return 0.5 * x * (1.0 + jnp.tanh(c * (x + 0.044715 * (x * x * x))))


def _s5_kernel(u_ref, perm_ref, permt_ref, bmat_ref, are_ref, aim_ref, cmat_ref, d_ref, wv_ref, wg_ref,
               o_ref, st_ref, bu_ref, *, bsz, steps, col_chunk):
    nc = SSM_COMPLEX

    @pl.when(pl.program_id(0) == 0)
    def _():
        st_ref[...] = jnp.zeros_like(st_ref)

    u_bt = jnp.concatenate([u_ref[:, b * SSM_WIDTH:(b + 1) * SSM_WIDTH] for b in range(bsz)], axis=0)
    u = jnp.dot(perm_ref[...], u_bt, preferred_element_type=F32).astype(BF16)
    for m in range(S5_SLABS):
        um = u[:, m * LANES:(m + 1) * LANES]
        re = slice(m * S5_SLAB_STATES, (m + 1) * S5_SLAB_STATES)
        im = slice(nc + m * S5_SLAB_STATES, nc + (m + 1) * S5_SLAB_STATES)
        bu_ref[:, re] = jnp.dot(um, bmat_ref[m, :, :S5_SLAB_STATES], preferred_element_type=F32)
        bu_ref[:, im] = jnp.dot(um, bmat_ref[m, :, S5_SLAB_STATES:], preferred_element_type=F32)

    for lo in range(0, nc, col_chunk):
        re = slice(lo, lo + col_chunk)
        im = slice(nc + lo, nc + lo + col_chunk)
        ar = jnp.broadcast_to(are_ref[:, re], (bsz, col_chunk))
        ai = jnp.broadcast_to(aim_ref[:, re], (bsz, col_chunk))

        def step(t, carry):
            sr, si = carry
            rows = pl.ds(pl.multiple_of(t * bsz, bsz), bsz)
            nr = ar * sr - ai * si + bu_ref[rows, re]
            ni = ar * si + ai * sr + bu_ref[rows, im]
            bu_ref[rows, re] = nr
            bu_ref[rows, im] = ni
            return nr, ni

        sr, si = lax.fori_loop(0, steps, step, (st_ref[:, re], st_ref[:, im]))
        st_ref[:, re] = sr
        st_ref[:, im] = si

    ys = []
    for m in range(S5_SLABS):
        re = slice(m * S5_SLAB_STATES, (m + 1) * S5_SLAB_STATES)
        im = slice(nc + m * S5_SLAB_STATES, nc + (m + 1) * S5_SLAB_STATES)
        ys.append(jnp.dot(bu_ref[:, re].astype(BF16), cmat_ref[m, :S5_SLAB_STATES, :],
                          preferred_element_type=F32)
                  + jnp.dot(bu_ref[:, im].astype(BF16), cmat_ref[m, S5_SLAB_STATES:, :],
                            preferred_element_type=F32))
    y = jnp.concatenate(ys, axis=1) + d_ref[...] * u.astype(F32)
    yb = _gelu_tanh(y).astype(BF16)
    val = jnp.dot(yb, wv_ref[...], preferred_element_type=F32)
    gate = jnp.dot(yb, wg_ref[...], preferred_element_type=F32)
    out = (val * _sigmoid(gate)).astype(BF16)
    out_bt = jnp.dot(permt_ref[...], out, preferred_element_type=F32).astype(BF16)
    for b in range(bsz):
        o_ref[:, b * SSM_WIDTH:(b + 1) * SSM_WIDTH] = out_bt[b * steps:(b + 1) * steps]


def _s5_branch(u_tm, bmat, a_re, a_im, cmat, d_skip, wv, wg, bsz, seq, steps):
    rows = steps * bsz
    kern = functools.partial(_s5_kernel, bsz=bsz, steps=steps, col_chunk=512)
    r = jnp.arange(rows)
    perm = (((r % bsz) * steps + r // bsz)[:, None] == r[None, :]).astype(BF16)
    return pl.pallas_call(
        kern,
        grid=(seq // steps,),
        in_specs=[
            pl.BlockSpec((steps, bsz * SSM_WIDTH), lambda i: (i, 0)),
            _const_spec((rows, rows)),
            _const_spec((rows, rows)),
            _const_spec((S5_SLABS, LANES, 2 * S5_SLAB_STATES)),
            _const_spec((1, SSM_COMPLEX)),
            _const_spec((1, SSM_COMPLEX)),
            _const_spec((S5_SLABS, 2 * S5_SLAB_STATES, LANES)),
            _const_spec((1, SSM_WIDTH)),
            _const_spec((SSM_WIDTH, SSM_WIDTH)),
            _const_spec((SSM_WIDTH, SSM_WIDTH)),
        ],
        out_specs=pl.BlockSpec((steps, bsz * SSM_WIDTH), lambda i: (i, 0)),
        out_shape=jax.ShapeDtypeStruct((seq, bsz * SSM_WIDTH), BF16),
        scratch_shapes=[
            pltpu.VMEM((bsz, 2 * SSM_COMPLEX), F32),
            pltpu.VMEM((rows, 2 * SSM_COMPLEX), F32),
        ],
        compiler_params=_cparams(("arbitrary",)),
    )(u_tm, perm, perm.T, bmat, a_re, a_im, cmat, d_skip, wv, wg)


def _s5_discretize(log_dt, a_re, a_im, b_re, b_im, c_re, c_im):
    dt = jnp.exp(log_dt.astype(F32))[:, None]
    lr = jnp.minimum(a_re.astype(F32), -1e-4)
    li = a_im.astype(F32)
    mag = jnp.exp(lr * dt)
    abr = mag * jnp.cos(li * dt)
    abi = mag * jnp.sin(li * dt)
    nr, ni = abr - 1.0, abi
    den = lr * lr + li * li
    fr = (nr * lr + ni * li) / den
    fi = (ni * lr - nr * li) / den
    br, bi = b_re.astype(F32), b_im.astype(F32)
    bbr = fr[..., None] * br - fi[..., None] * bi
    bbi = fr[..., None] * bi + fi[..., None] * br
    gps = SSM_GROUPS // S5_SLABS
    eye = jnp.eye(gps, dtype=F32)

    def blockdiag_in(m):
        m = m.reshape(S5_SLABS, gps, SSM_STATE, SSM_GROUP)
        return jnp.einsum('sgpc,gh->sgchp', m, eye).reshape(S5_SLABS, LANES, S5_SLAB_STATES)

    def blockdiag_out(m):
        m = m.reshape(S5_SLABS, gps, SSM_GROUP, SSM_STATE)
        return jnp.einsum('sgcp,gh->sgphc', m, eye).reshape(S5_SLABS, S5_SLAB_STATES, LANES)

    bmat = jnp.concatenate([blockdiag_in(bbr), blockdiag_in(bbi)], axis=2).astype(BF16)
    cmat = jnp.concatenate([blockdiag_out(c_re.astype(F32)), -blockdiag_out(c_im.astype(F32))],
                           axis=1).astype(BF16)
    return bmat, abr.reshape(1, SSM_COMPLEX), abi.reshape(1, SSM_COMPLEX), cmat


def _memkv_kernel(mem_ref, g_ref, w_ref, kg_ref, mk_ref, mv_ref):
    x = mem_ref[...]
    h = x * lax.rsqrt(jnp.mean(x * x, axis=-1, keepdims=True) + NORM_EPS) * g_ref[...]
    kv = jnp.dot(h.astype(BF16), w_ref[...], preferred_element_type=F32)
    parts = []
    for hd in range(MEM_HEADS):
        kh = kv[:, hd * MEM_HEAD_DIM:(hd + 1) * MEM_HEAD_DIM]
        kh = kh * lax.rsqrt(jnp.mean(kh * kh, axis=-1, keepdims=True) + QK_EPS) * kg_ref[...]
        parts.append(kh)
    mk_ref[...] = jnp.concatenate(parts, axis=1).astype(BF16)
    mv_ref[...] = kv[:, MEM_WIDTH:].astype(BF16)


def _mem_kv(mem2, g, w_b, k_gain, bsz):
    return pl.pallas_call(
        _memkv_kernel,
        grid=(bsz,),
        in_specs=[
            pl.BlockSpec((MEM_LEN, D_MODEL), lambda b: (b, 0)),
            _const_spec((1, D_MODEL)),
            _const_spec((D_MODEL, 2 * MEM_WIDTH)),
            _const_spec((1, MEM_HEAD_DIM)),
        ],
        out_specs=[
            pl.BlockSpec((MEM_LEN, MEM_WIDTH), lambda b: (b, 0)),
            pl.BlockSpec((MEM_LEN, MEM_WIDTH), lambda b: (b, 0)),
        ],
        out_shape=[
            jax.ShapeDtypeStruct((bsz * MEM_LEN, MEM_WIDTH), BF16),
            jax.ShapeDtypeStruct((bsz * MEM_LEN, MEM_WIDTH), BF16),
        ],
        compiler_params=_cparams(("parallel",)),
    )(mem2, g, w_b, k_gain)


HEADS_PER_COL = LANES // HEAD_DIM
ROPE_CONST_ROWS = 16


def _split_bf16(a):
    hi = a.astype(BF16)
    lo = (a - hi.astype(F32)).astype(BF16)
    return hi, lo


def _rope_tables(pos_row, rc_ref, re_ref):
    tq = pos_row.shape[1]
    freq = jnp.concatenate([rc_ref[0:ROPE_HALF, :]] * (tq // LANES), axis=1)
    ang = freq * pos_row
    trig = jnp.concatenate([jnp.cos(ang), jnp.sin(ang),
                            jnp.zeros((LANES - 2 * ROPE_HALF, tq), F32)], axis=0)
    tab = jnp.dot(trig.T, re_ref[...], precision=lax.Precision.HIGHEST, preferred_element_type=F32)
    return tab[:, 0:LANES] + rc_ref[ROPE_HALF:ROPE_HALF + 1, :], tab[:, LANES:2 * LANES]


def _attn_kernel(sink_ref, q_ref, k_ref, v_ref, pos_ref, qm_ref, mk_ref, mv_ref,
                 rc_ref, re_ref, perm_ref, seg_ref, gmq_ref, oswa_ref, omem_ref, kbuf_ref, vbuf_ref, *, tq):
    nblk = tq // BLOCK
    first_tile = pl.program_id(1) == 0
    dot = lambda a, b: jnp.dot(a, b, preferred_element_type=F32)

    nver = kbuf_ref.shape[0] // 2
    mine = (pl.program_id(1) % 2) * nver
    other = nver - mine

    @pl.when(first_tile)
    def _():
        kbuf_ref[0:nver, 0:BLOCK, :] = jnp.zeros((nver, BLOCK, LANES), BF16)
        vbuf_ref[0:nver, 0:BLOCK, :] = jnp.zeros((nver, BLOCK, 2 * LANES), BF16)
        vbuf_ref[:, :, LANES:2 * LANES] = jnp.ones((2 * nver, BLOCK + tq, LANES), BF16)

    cos_t, sin_t = _rope_tables(pos_ref[0:1, :], rc_ref, re_ref)
    seg = seg_ref[...]
    perm = perm_ref[...]
    lo_half_pk = lax.broadcasted_iota(jnp.int32, (tq, LANES), 1) < HEAD_DIM
    lo_half_blk = lax.broadcasted_iota(jnp.int32, (BLOCK, LANES), 1) < HEAD_DIM

    def norm_rope(raw_bf, gain_cos, gain_sin):
        raw = raw_bf.astype(F32)
        hi, lo = _split_bf16(raw * raw)
        rs = lax.rsqrt((dot(hi, seg) + dot(lo, seg)) * (1.0 / HEAD_DIM) + QK_EPS)
        return (raw * gain_cos + dot(raw_bf, perm) * gain_sin) * rs

    q_cos = cos_t * rc_ref[ROPE_HALF + 1:ROPE_HALF + 2, :]
    q_sin = sin_t * rc_ref[ROPE_HALF + 2:ROPE_HALF + 3, :]
    k_cos = cos_t * rc_ref[ROPE_HALF + 3:ROPE_HALF + 4, :]
    k_sin = sin_t * rc_ref[ROPE_HALF + 4:ROPE_HALF + 5, :]

    for col in range(SWA_KV_WIDTH // LANES):
        cs = slice(col * LANES, (col + 1) * LANES)
        kc = norm_rope(k_ref[:, cs], k_cos, k_sin)
        versions = ((kbuf_ref, slice(None), kc.astype(BF16), pltpu.roll(kc, HEAD_DIM, 1).astype(BF16)),
                    (vbuf_ref, slice(0, LANES), v_ref[:, cs],
                     pltpu.roll(v_ref[:, cs].astype(F32), HEAD_DIM, 1).astype(BF16)))
        for buf, lanes, plain, rotated in versions:
            for ver, val in ((2 * col, plain), (2 * col + 1, rotated)):
                buf[mine + ver, BLOCK:BLOCK + tq, lanes] = val
                buf[other + ver, 0:BLOCK, lanes] = val[tq - BLOCK:tq]

    q_half = []
    for col in range(SWA_Q_WIDTH // LANES):
        cs = slice(col * LANES, (col + 1) * LANES)
        qc = norm_rope(q_ref[:, cs], q_cos, q_sin)
        q_half.append((jnp.where(lo_half_pk, qc, 0.0).astype(BF16), jnp.where(lo_half_pk, 0.0, qc).astype(BF16)))

    qi = lax.broadcasted_iota(jnp.int32, (2 * BLOCK, BLOCK), 0) & (BLOCK - 1)
    kj = lax.broadcasted_iota(jnp.int32, (2 * BLOCK, BLOCK), 1)
    own = kj <= qi
    no_prev = jnp.where(first_tile, 1, 0).astype(F32) * NEG_BIG

    for n in range(nblk):
        r0 = n * BLOCK
        for h in range(SWA_KV_HEADS):
            by_half = []
            for half in range(HEADS_PER_COL):
                ver = 2 * (h // HEADS_PER_COL) + (0 if half == h % HEADS_PER_COL else 1)
                qs = jnp.concatenate([q_half[2 * h][half][r0:r0 + BLOCK],
                                      q_half[2 * h + 1][half][r0:r0 + BLOCK]], axis=0)
                s2 = lax.dot_general(qs, kbuf_ref[mine + ver, r0:r0 + 2 * BLOCK, :], (((1,), (1,)), ((), ())),
                                     preferred_element_type=F32)
                s_prev = s2[:, :BLOCK] + no_prev if n == 0 else s2[:, :BLOCK]
                s = jnp.where(own, s2[:, BLOCK:], s_prev)
                sink = jnp.concatenate(
                    [jnp.full((BLOCK, BLOCK), sink_ref[h * SWA_GROUP + half], F32),
                     jnp.full((BLOCK, BLOCK), sink_ref[h * SWA_GROUP + half + HEADS_PER_COL], F32)], axis=0)
                m = jnp.maximum(jnp.broadcast_to(jnp.max(s, axis=-1, keepdims=True), s.shape), sink)
                p = jnp.exp(s - m)
                p2 = jnp.concatenate([jnp.where(own, 0.0, p).astype(BF16),
                                      jnp.where(own, p, 0.0).astype(BF16)], axis=1)
                o2 = dot(p2, vbuf_ref[mine + ver, r0:r0 + 2 * BLOCK, :])
                by_half.append(o2[:, :LANES] / (o2[:, LANES:] + jnp.exp(sink - m)))
            for sub in range(2):
                rs_ = slice(sub * BLOCK, (sub + 1) * BLOCK)
                colv = jnp.where(lo_half_blk, by_half[0][rs_], by_half[1][rs_])
                oswa_ref[r0:r0 + BLOCK, (2 * h + sub) * LANES:(2 * h + sub + 1) * LANES] = colv.astype(BF16)

    ones = jnp.ones((MEM_LEN, LANES), BF16)
    outs = []
    for hd in range(MEM_HEADS):
        sl = slice(hd * MEM_HEAD_DIM, (hd + 1) * MEM_HEAD_DIM)
        qh = qm_ref[:, sl].astype(F32)
        rs = lax.rsqrt(jnp.mean(qh * qh, axis=-1, keepdims=True) + QK_EPS)
        qh = qh * (rs * (MEM_HEAD_DIM ** -0.5)) * gmq_ref[...]
        s = lax.dot_general(qh.astype(BF16), mk_ref[:, sl], (((1,), (1,)), ((), ())),
                            preferred_element_type=F32)
        p = jnp.exp(s - jnp.max(s, axis=-1, keepdims=True)).astype(BF16)
        o2 = dot(p, jnp.concatenate([mv_ref[:, sl], ones], axis=1))
        outs.append(o2[:, :LANES] / o2[:, LANES:])
    omem_ref[...] = jnp.concatenate(outs, axis=1).astype(BF16)


def _rope_constants(q_gain, k_gain):
    inv_freq = ROPE_THETA ** (-jnp.arange(ROPE_HALF, dtype=F32) / ROPE_HALF)
    lane = jnp.arange(LANES) % HEAD_DIM
    j = jnp.arange(ROPE_HALF)[:, None]
    e_cos = ((lane[None, :] < ROPE_DIM) & (lane[None, :] % ROPE_HALF == j)).astype(F32)
    e_sin = (lane[None, :] == j + ROPE_HALF).astype(F32) - (lane[None, :] == j).astype(F32)
    ones = (lane >= ROPE_DIM).astype(F32)[None, :]
    src = jnp.where(lane < ROPE_HALF, jnp.arange(LANES) + ROPE_HALF, jnp.arange(LANES) - ROPE_HALF)
    rotary = lane < ROPE_DIM
    perm = ((jnp.arange(LANES)[:, None] == src[None, :]) & rotary[None, :]).astype(BF16)

    def gains(g, scale):
        col = jnp.tile(g.astype(F32).reshape(-1), HEADS_PER_COL) * scale
        return col[None, :], jnp.where(rotary, col[jnp.clip(src, 0, LANES - 1)], 0.0)[None, :]

    qg, qgp = gains(q_gain, HEAD_DIM ** -0.5)
    kg, kgp = gains(k_gain, 1.0)
    rows = jnp.concatenate([jnp.broadcast_to(inv_freq[:, None], (ROPE_HALF, LANES)), ones, qg, qgp, kg, kgp,
                            jnp.zeros((ROPE_CONST_ROWS - ROPE_HALF - 5, LANES), F32)], axis=0)
    zero8 = jnp.zeros((ROPE_HALF, LANES), F32)
    expand = jnp.concatenate([
        jnp.concatenate([e_cos, zero8], axis=1),
        jnp.concatenate([zero8, e_sin], axis=1),
        jnp.zeros((LANES - 2 * ROPE_HALF, 2 * LANES), F32)], axis=0)
    return rows, expand, perm


def _attention(sinks, q, k, v, pos_rows, qm, mk, mv, q_gain, k_gain, gmq, bsz, seq, tq):
    t = bsz * seq
    nj = seq // tq
    row = lambda b, j: (b * nj + j, 0)
    mem = lambda b, j: (b, 0)
    half = jnp.arange(LANES) // HEAD_DIM
    seg = (half[:, None] == half[None, :]).astype(BF16)
    kern = functools.partial(_attn_kernel, tq=tq)
    nver = 2 * (2 * SWA_KV_WIDTH // LANES)
    return pl.pallas_call(
        kern,
        grid=(bsz, nj),
        in_specs=[
            pl.BlockSpec(memory_space=pltpu.SMEM),
            pl.BlockSpec((tq, SWA_Q_WIDTH), row),
            pl.BlockSpec((tq, SWA_KV_WIDTH), row),
            pl.BlockSpec((tq, SWA_KV_WIDTH), row),
            pl.BlockSpec((SUBLANES, tq), row),
            pl.BlockSpec((tq, MEM_WIDTH), row),
            pl.BlockSpec((MEM_LEN, MEM_WIDTH), mem),
            pl.BlockSpec((MEM_LEN, MEM_WIDTH), mem),
            _const_spec((ROPE_CONST_ROWS, LANES)),
            _const_spec((LANES, 2 * LANES)),
            _const_spec((LANES, LANES)),
            _const_spec((LANES, LANES)),
            _const_spec((1, MEM_HEAD_DIM)),
        ],
        out_specs=[
            pl.BlockSpec((tq, SWA_Q_WIDTH), row),
            pl.BlockSpec((tq, MEM_WIDTH), row),
        ],
        out_shape=[
            jax.ShapeDtypeStruct((t, SWA_Q_WIDTH), BF16),
            jax.ShapeDtypeStruct((t, MEM_WIDTH), BF16),
        ],
        scratch_shapes=[
            pltpu.VMEM((nver, BLOCK + tq, LANES), BF16),
            pltpu.VMEM((nver, BLOCK + tq, 2 * LANES), BF16),
        ],
        compiler_params=_cparams(("parallel", "arbitrary")),
    )(sinks, q, k, v, pos_rows, qm, mk, mv, *_rope_constants(q_gain, k_gain), seg, gmq)


def _merge_kernel(x_ref, ys_ref, yw_ref, ym_ref, gate_ref, w0_ref, w1_ref, w2_ref, wo_ref, gf_ref,
                  wr_cat_ref, br_ref, x1_ref, tsl_ref, idx_ref, wt_ref, *, tm):
    gates = gate_ref[...]
    dot = lambda a, b: jnp.dot(a, b, preferred_element_type=F32)
    merged = (gates[:, 0:D_MODEL].astype(F32) * dot(ys_ref[...], w0_ref[...])
              + gates[:, D_MODEL:2 * D_MODEL].astype(F32) * dot(yw_ref[...], w1_ref[...])
              + gates[:, 2 * D_MODEL:].astype(F32) * dot(ym_ref[...], w2_ref[...]))
    x1 = x_ref[...] + dot(merged.astype(BF16), wo_ref[...])
    x1_ref[...] = x1
    t = x1 * lax.rsqrt(jnp.mean(x1 * x1, axis=-1, keepdims=True) + NORM_EPS) * gf_ref[...]

    for j in range(ROW_SUBLANES):
        tsl_ref[pl.ds(j, tm, stride=ROW_SUBLANES), :] = t[:, j * LANES:(j + 1) * LANES]

    t_hi, t_lo = _split_bf16(t)
    both = dot(t_hi, wr_cat_ref[...])
    logits = both[:, :LANES] + both[:, LANES:] + dot(t_lo, wr_cat_ref[:, :LANES]) + br_ref[...]
    lane = lax.broadcasted_iota(jnp.int32, logits.shape, 1).astype(F32)
    work = logits
    vals, firsts = [], []
    for _ in range(TOP_K):
        m = jnp.max(work, axis=-1, keepdims=True)
        first = jnp.min(jnp.where(work == m, lane, float(LANES)), axis=-1, keepdims=True)
        work = jnp.where(lane == first, NEG_BIG * 2.0, work)
        vals.append(m)
        firsts.append(first)
    exps = [jnp.exp(v - vals[0]) for v in vals]
    den = exps[0] + exps[1] + exps[2] + exps[3]
    idx_out = jnp.zeros_like(logits)
    wt_out = jnp.zeros_like(logits)
    for k in range(TOP_K):
        idx_out = jnp.where(lane == float(k), firsts[k], idx_out)
        wt_out = jnp.where(lane == float(k), exps[k] / den, wt_out)
    idx_ref[...] = idx_out.T[0:SUBLANES, :].astype(jnp.int32)
    wt_ref[...] = wt_out.T[0:SUBLANES, :]


def _merge(x2, y_ssm_tm, y_swa, y_mem, gates, w0, w1, w2, wo, gf, wr_cat, br, bsz, seq, tm):
    t = bsz * seq
    nj = seq // tm
    row = lambda b, j: (b * nj + j, 0)
    return pl.pallas_call(
        functools.partial(_merge_kernel, tm=tm),
        grid=(bsz, nj),
        in_specs=[
            pl.BlockSpec((tm, D_MODEL), row),
            pl.BlockSpec((tm, SSM_WIDTH), lambda b, j: (j, b)),
            pl.BlockSpec((tm, SWA_Q_WIDTH), row),
            pl.BlockSpec((tm, MEM_WIDTH), row),
            pl.BlockSpec((tm, GATE_WIDTH), row),
            _const_spec((SSM_WIDTH, D_MODEL)),
            _const_spec((SWA_Q_WIDTH, D_MODEL)),
            _const_spec((MEM_WIDTH, D_MODEL)),
            _const_spec((D_MODEL, D_MODEL)),
            _const_spec((1, D_MODEL)),
            _const_spec((D_MODEL, 2 * LANES)),
            _const_spec((1, LANES)),
        ],
        out_specs=[
            pl.BlockSpec((tm, D_MODEL), row),
            pl.BlockSpec((tm * ROW_SUBLANES, LANES), row),
            pl.BlockSpec((SUBLANES, tm), row),
            pl.BlockSpec((SUBLANES, tm), row),
        ],
        out_shape=[
            jax.ShapeDtypeStruct((t, D_MODEL), F32),
            jax.ShapeDtypeStruct((t * ROW_SUBLANES, LANES), F32),
            jax.ShapeDtypeStruct((t // tm * SUBLANES, tm), jnp.int32),
            jax.ShapeDtypeStruct((t // tm * SUBLANES, tm), F32),
        ],
        compiler_params=_cparams(("parallel", "parallel")),
    )(x2, y_ssm_tm, y_swa, y_mem, gates, w0, w1, w2, wo, gf, wr_cat, br)


DEINT_BLOCK = 2 * LANES


def _expert_prep_kernel(w1_ref, perm_ref, w1d_ref):
    perm = perm_ref[...]
    for blk in range(2 * D_EXPERT // DEINT_BLOCK):
        cols = w1_ref[0, :, blk * DEINT_BLOCK:(blk + 1) * DEINT_BLOCK].astype(BF16)
        z = jnp.dot(cols, perm, preferred_element_type=F32).astype(BF16)
        w1d_ref[0, :, blk * LANES:(blk + 1) * LANES] = z[:, :LANES]
        w1d_ref[0, :, D_EXPERT + blk * LANES:D_EXPERT + (blk + 1) * LANES] = z[:, LANES:]


def _expert_prep(w1):
    src = jnp.arange(DEINT_BLOCK)
    dst = jnp.where(src % 2 == 0, src // 2, LANES + src // 2)
    perm = (dst[:, None] == jnp.arange(DEINT_BLOCK)[None, :]).astype(BF16)
    exp3 = lambda e: (e, 0, 0)
    return pl.pallas_call(
        _expert_prep_kernel,
        grid=(N_EXPERTS,),
        in_specs=[pl.BlockSpec((1, D_MODEL, 2 * D_EXPERT), exp3), _const_spec((DEINT_BLOCK, DEINT_BLOCK))],
        out_specs=pl.BlockSpec((1, D_MODEL, 2 * D_EXPERT), exp3),
        out_shape=jax.ShapeDtypeStruct((N_EXPERTS, D_MODEL, 2 * D_EXPERT), BF16),
        compiler_params=_cparams(("parallel",)),
    )(w1, perm)


SCATTER_BATCH = 8


def _moe_kernel(cnt_ref, off_ref, tok_ref, wt_ref, src_ref, w1_ref, w2_ref, bias_ref,
                out_hbm, acc_ref, gbuf_ref, ybuf_ref, state_ref, sem, *, nb_tokens, tile):
    blk = pl.program_id(0)
    e = pl.program_id(1)
    acc_rows = nb_tokens * ROW_SUBLANES
    buf_rows = tile * ROW_SUBLANES

    def gather(base, half):
        row0 = half * buf_rows
        for i in range(tile):
            tok = tok_ref[0, 0, base + i]
            gbuf_ref[pl.ds(pl.multiple_of(row0 + i * ROW_SUBLANES, ROW_SUBLANES), ROW_SUBLANES), :] = (
                src_ref[pl.ds(pl.multiple_of(tok * ROW_SUBLANES, ROW_SUBLANES), ROW_SUBLANES), :])

    def scatter(base, nvalid, half):
        row0 = half * buf_rows
        for i0 in range(0, tile, SCATTER_BATCH):
            sums, dsts = [], []
            for i in range(i0, i0 + SCATTER_BATCH):
                live = i < nvalid
                tok = jnp.where(live, tok_ref[0, 0, base + i], nb_tokens)
                w = jnp.where(live, wt_ref[0, 0, base + i], 0.0)
                dst = pl.ds(pl.multiple_of(tok * ROW_SUBLANES, ROW_SUBLANES), ROW_SUBLANES)
                src = pl.ds(pl.multiple_of(row0 + i * ROW_SUBLANES, ROW_SUBLANES), ROW_SUBLANES)
                sums.append(acc_ref[dst, :] + w * ybuf_ref[src, :])
                dsts.append(dst)
            for dst, s in zip(dsts, sums):
                acc_ref[dst, :] = s

    def writeback(block):
        return pltpu.make_async_copy(acc_ref.at[pl.ds(0, acc_rows), :],
                                     out_hbm.at[pl.ds(block * acc_rows, acc_rows), :], sem)

    @pl.when(e == 0)
    def _():
        @pl.when(blk > 0)
        def _():
            writeback(blk - 1).wait()

        acc_ref[...] = jnp.zeros_like(acc_ref)
        ybuf_ref[...] = jnp.zeros_like(ybuf_ref)
        state_ref[0] = 0
        state_ref[1] = 0
        state_ref[2] = 0
        gather(0, 0)

    n = cnt_ref[blk * N_EXPERTS + e]
    start = off_ref[blk * N_EXPERTS + e]
    ntiles = (n + tile - 1) // tile

    def stages(rows, next_base, prev_base, prev_n, half):
        other = 1 - half
        gather(next_base, other)
        row0 = half * buf_rows
        x = jnp.concatenate(
            [gbuf_ref[pl.ds(row0 + j, rows, stride=ROW_SUBLANES), :] for j in range(ROW_SUBLANES)],
            axis=1).astype(BF16)
        hg = jnp.dot(x, w1_ref[0, :, :D_EXPERT], preferred_element_type=F32) + bias_ref[0, :, :D_EXPERT]
        hl = (jnp.dot(x, w1_ref[0, :, D_EXPERT:], preferred_element_type=F32)
              + bias_ref[0, :, D_EXPERT:2 * D_EXPERT])
        xg = jnp.minimum(hg, SWIGLU_LIMIT)
        xl = jnp.clip(hl, -SWIGLU_LIMIT, SWIGLU_LIMIT)
        act = xg * _sigmoid(SWIGLU_ALPHA * xg) * (xl + 1.0)
        y = jnp.dot(act.astype(BF16), w2_ref[0], preferred_element_type=F32) + bias_ref[0, :, 2 * D_EXPERT:]
        scatter(prev_base, prev_n, other)
        for j in range(ROW_SUBLANES):
            ybuf_ref[pl.ds(row0 + j, rows, stride=ROW_SUBLANES), :] = y[:, j * LANES:(j + 1) * LANES]

    def tile_body(ti, carry):
        prev_base, prev_n, half = carry
        base = start + ti * tile
        live = n - ti * tile
        next_base = jnp.where(ti + 1 < ntiles, base + tile, start + n)
        lax.cond(live <= MOE_SMALL_TILE,
                 functools.partial(stages, MOE_SMALL_TILE, next_base, prev_base, prev_n, half),
                 functools.partial(stages, tile, next_base, prev_base, prev_n, half))
        return base, live, 1 - half

    prev_base, prev_n, half = lax.fori_loop(0, ntiles, tile_body,
                                            (state_ref[0], state_ref[1], state_ref[2]))
    state_ref[0] = prev_base
    state_ref[1] = prev_n
    state_ref[2] = half

    @pl.when(e == N_EXPERTS - 1)
    def _():
        scatter(prev_base, prev_n, 1 - half)
        writeback(blk).start()

        @pl.when(blk == pl.num_programs(0) - 1)
        def _():
            writeback(blk).wait()


def _moe(counts, offsets, tok_sorted, wt_sorted, tsl, w1d, w2, bias, t, nb_tokens, tile):
    nblk = t // nb_tokens
    slots = tok_sorted.shape[-1]
    exp3 = lambda b, e, *_: (e, 0, 0)
    blk3 = lambda b, e, *_: (b, 0, 0)
    grid_spec = pltpu.PrefetchScalarGridSpec(
        num_scalar_prefetch=2,
        grid=(nblk, N_EXPERTS),
        in_specs=[
            pl.BlockSpec((1, 1, slots), blk3, memory_space=pltpu.SMEM),
            pl.BlockSpec((1, 1, slots), blk3, memory_space=pltpu.SMEM),
            pl.BlockSpec((nb_tokens * ROW_SUBLANES, LANES), lambda b, e, *_: (b, 0),
                         pipeline_mode=pl.Buffered(1)),
            pl.BlockSpec((1, D_MODEL, 2 * D_EXPERT), exp3),
            pl.BlockSpec((1, D_EXPERT, D_MODEL), exp3),
            pl.BlockSpec((1, 1, 2 * D_EXPERT + D_MODEL), exp3),
        ],
        out_specs=pl.BlockSpec(memory_space=pl.ANY),
        scratch_shapes=[
            pltpu.VMEM(((nb_tokens + 1) * ROW_SUBLANES, LANES), F32),
            pltpu.VMEM((2 * tile * ROW_SUBLANES, LANES), F32),
            pltpu.VMEM((2 * tile * ROW_SUBLANES, LANES), F32),
            pltpu.SMEM((3,), jnp.int32),
            pltpu.SemaphoreType.DMA,
        ],
    )
    return pl.pallas_call(
        functools.partial(_moe_kernel, nb_tokens=nb_tokens, tile=tile),
        grid_spec=grid_spec,
        out_shape=jax.ShapeDtypeStruct((t * ROW_SUBLANES, LANES), F32),
        compiler_params=_cparams(("arbitrary", "arbitrary")),
    )(counts, offsets, tok_sorted, wt_sorted, tsl, w1d, w2, bias)


def _route(idx_t, wts_t, t, nb_tokens, tm, tile):
    nblk = t // nb_tokens
    per_blk = nb_tokens // tm
    pick = lambda a: a.reshape(t // tm, SUBLANES, tm)[:, :TOP_K, :].reshape(nblk, nb_tokens * TOP_K)
    eid, w = pick(idx_t), pick(wts_t)
    pos = lax.broadcasted_iota(jnp.int32, (per_blk, TOP_K, tm), 0) * tm + lax.broadcasted_iota(
        jnp.int32, (per_blk, TOP_K, tm), 2)
    tok = jnp.broadcast_to(pos.reshape(1, nb_tokens * TOP_K), eid.shape)
    eid_sorted, tok_sorted, wt_sorted = lax.sort((eid, tok, w), dimension=1, is_stable=True, num_keys=1)
    bounds = jax.vmap(lambda keys: jnp.searchsorted(keys, jnp.arange(N_EXPERTS + 1), side='left'))(eid_sorted)
    offsets = bounds[:, :N_EXPERTS].astype(jnp.int32)
    counts = (bounds[:, 1:] - bounds[:, :N_EXPERTS]).astype(jnp.int32)
    pad = ((0, 0), (0, tile))
    tok_sorted = jnp.pad(tok_sorted, pad)[:, None, :]
    wt_sorted = jnp.pad(wt_sorted, pad)[:, None, :]
    return counts.reshape(-1), offsets.reshape(-1).astype(jnp.int32), tok_sorted, wt_sorted


def _residual_kernel(x1_ref, moe_ref, o_ref, *, tm):
    parts = [moe_ref[pl.ds(j, tm, stride=ROW_SUBLANES), :] for j in range(ROW_SUBLANES)]
    o_ref[...] = x1_ref[...] + jnp.concatenate(parts, axis=1)


def _residual(x1, moe, tm):
    t = x1.shape[0]
    return pl.pallas_call(
        functools.partial(_residual_kernel, tm=tm),
        grid=(t // tm,),
        in_specs=[pl.BlockSpec((tm, D_MODEL), lambda i: (i, 0)),
                  pl.BlockSpec((tm * ROW_SUBLANES, LANES), lambda i: (i, 0))],
        out_specs=pl.BlockSpec((tm, D_MODEL), lambda i: (i, 0)),
        out_shape=jax.ShapeDtypeStruct((t, D_MODEL), F32),
        compiler_params=_cparams(("parallel",)),
    )(x1, moe)


def kernel(x, mem, positions, attn_norm_g, mem_norm_g, w_in, b_gate, ssm_log_dt, ssm_a_re, ssm_a_im,
           ssm_b_re, ssm_b_im, ssm_c_re, ssm_c_im, ssm_d, w_glu_v, w_glu_g, swa_q_gain, swa_k_gain,
           swa_sinks, w_mem_kv, mem_q_gain, mem_k_gain, w_br_ssm, w_br_swa, w_br_mem, w_out,
           ffn_norm_g, w_router, b_router, w_mlp1, b_mlp1, w_mlp2, b_mlp2):
    bsz, seq, _ = x.shape
    depth = w_in.shape[0]
    t = bsz * seq
    tm = min(512, seq)
    tq = min(512, seq)
    steps = min(32, seq)

    pos_rows = jnp.broadcast_to(positions.astype(F32).reshape(t // tq, 1, tq),
                                (t // tq, SUBLANES, tq)).reshape(t // tq * SUBLANES, tq)
    mem2 = mem.reshape(bsz * MEM_LEN, D_MODEL)

    x2 = x.reshape(t, D_MODEL)
    for l in range(depth):
        row = lambda a: a[l].reshape(1, -1).astype(F32)
        u_tm, q, k, v, qm, gates = _in_projection(
            x2, row(attn_norm_g), w_in[l].astype(BF16), row(b_gate), bsz, seq, tm)

        bmat, abr, abi, cmat = _s5_discretize(ssm_log_dt[l], ssm_a_re[l], ssm_a_im[l], ssm_b_re[l],
                                              ssm_b_im[l], ssm_c_re[l], ssm_c_im[l])
        y_ssm_tm = _s5_branch(u_tm, bmat, abr, abi, cmat, row(ssm_d),
                              w_glu_v[l].astype(BF16), w_glu_g[l].astype(BF16), bsz, seq, steps)

        mk, mv = _mem_kv(mem2, row(mem_norm_g), w_mem_kv[l].astype(BF16), row(mem_k_gain), bsz)
        y_swa, y_mem = _attention(
            swa_sinks[l].astype(F32), q, k, v, pos_rows, qm, mk, mv, swa_q_gain[l], swa_k_gain[l],
            row(mem_q_gain), bsz, seq, tq)

        wr = jnp.pad(w_router[l].astype(F32), ((0, 0), (0, LANES - N_EXPERTS)))
        wr_hi = wr.astype(BF16)
        wr_lo = (wr - wr_hi.astype(F32)).astype(BF16)
        br = jnp.pad(b_router[l].astype(F32), (0, LANES - N_EXPERTS), constant_values=NEG_BIG).reshape(1, LANES)
        x1, tsl, idx, wts = _merge(x2, y_ssm_tm, y_swa, y_mem, gates, w_br_ssm[l].astype(BF16),
                                   w_br_swa[l].astype(BF16), w_br_mem[l].astype(BF16),
                                   w_out[l].astype(BF16), row(ffn_norm_g), jnp.concatenate([wr_hi, wr_lo], axis=1), br, bsz, seq, tm)

        nb_tokens = min(MOE_TOKEN_BLOCK, t)
        idx, w1_l, w2_l = lax.optimization_barrier((idx, w_mlp1[l], w_mlp2[l]))
        counts, offsets, tok_sorted, wt_sorted = _route(idx, wts, t, nb_tokens, tm, MOE_TILE)
        b1 = b_mlp1[l].astype(F32)
        bias = jnp.concatenate([b1[:, 0::2], b1[:, 1::2], b_mlp2[l].astype(F32)], axis=1)[:, None, :]
        moe = _moe(counts, offsets, tok_sorted, wt_sorted, tsl, _expert_prep(w1_l), w2_l.astype(BF16), bias,
                   t, nb_tokens, MOE_TILE)
        x2 = _residual(x1, moe, tm)
    return x2.reshape(bsz, seq, D_MODEL)
```

```python
import functools
import math

import jax
import jax.numpy as jnp
from jax import lax
from jax.experimental import pallas as pl
from jax.experimental.pallas import tpu as pltpu

F32 = jnp.float32
BF16 = jnp.bfloat16

D_MODEL = 1024
MEM_LEN = 256
NORM_EPS = 1e-5
QK_EPS = 1e-6

SSM_WIDTH = 512
SSM_GROUP = 16
SSM_GROUPS = 32
SSM_STATE = 64
SSM_COMPLEX = SSM_GROUPS * SSM_STATE
S5_SLABS = SSM_WIDTH // 128
S5_SLAB_STATES = SSM_COMPLEX // S5_SLABS

HEAD_DIM = 64
SWA_KV_HEADS = 4
SWA_GROUP = 4
BLOCK = 128
ROPE_THETA = 500000.0
ROPE_DIM = 16
ROPE_HALF = 8

MEM_HEADS = 4
MEM_HEAD_DIM = 128

SWA_Q_WIDTH = 1024
SWA_KV_WIDTH = 256
MEM_WIDTH = 512
N_BRANCH = 3
GATE_WIDTH = N_BRANCH * D_MODEL
OFF_U, OFF_Q, OFF_K, OFF_V, OFF_QM, OFF_G, OFF_END = 0, 512, 1536, 1792, 2048, 2560, 5632

N_EXPERTS = 32
TOP_K = 4
D_EXPERT = 1024
SWIGLU_ALPHA = 1.702
SWIGLU_LIMIT = 7.0

LANES = 128
SUBLANES = 8
VMEM_LIMIT = 56 * 1024 * 1024
NEG_BIG = -1e30

ROW_SUBLANES = D_MODEL // LANES
MOE_TOKEN_BLOCK = 4096
MOE_TILE = 272
MOE_SMALL_TILE = 144


def _sigmoid(x):
    return 1.0 / (1.0 + jnp.exp(-x))


def _cparams(sem):
    return pltpu.CompilerParams(dimension_semantics=sem, vmem_limit_bytes=VMEM_LIMIT)


def _const_spec(shape):
    nd = len(shape)
    return pl.BlockSpec(shape, lambda *_: (0,) * nd)


def _inproj_kernel(x_ref, g_ref, w_ref, bg_ref, u_ref, q_ref, k_ref, v_ref, qm_ref, gate_ref):
    x = x_ref[...]
    h = x * lax.rsqrt(jnp.mean(x * x, axis=-1, keepdims=True) + NORM_EPS) * g_ref[...]
    hb = h.astype(BF16)

    def proj(lo, hi):
        return jnp.dot(hb, w_ref[:, lo:hi], preferred_element_type=F32)

    u_ref[...] = proj(OFF_U, OFF_Q).astype(BF16)
    q_ref[...] = proj(OFF_Q, OFF_K).astype(BF16)
    k_ref[...] = proj(OFF_K, OFF_V).astype(BF16)
    v_ref[...] = proj(OFF_V, OFF_QM).astype(BF16)
    qm_ref[...] = proj(OFF_QM, OFF_G).astype(BF16)
    gate_ref[...] = _sigmoid(proj(OFF_G, OFF_END) + bg_ref[...]).astype(BF16)


def _in_projection(x2, g, w_in_b, b_gate, bsz, seq, tm):
    t = bsz * seq
    nj = seq // tm
    row = lambda b, j: (b * nj + j, 0)
    return pl.pallas_call(
        _inproj_kernel,
        grid=(bsz, nj),
        in_specs=[
            pl.BlockSpec((tm, D_MODEL), row),
            _const_spec((1, D_MODEL)),
            _const_spec((D_MODEL, OFF_END)),
            _const_spec((1, GATE_WIDTH)),
        ],
        out_specs=[
            pl.BlockSpec((tm, SSM_WIDTH), lambda b, j: (j, b)),
            pl.BlockSpec((tm, SWA_Q_WIDTH), row),
            pl.BlockSpec((tm, SWA_KV_WIDTH), row),
            pl.BlockSpec((tm, SWA_KV_WIDTH), row),
            pl.BlockSpec((tm, MEM_WIDTH), row),
            pl.BlockSpec((tm, GATE_WIDTH), row),
        ],
        out_shape=[
            jax.ShapeDtypeStruct((seq, bsz * SSM_WIDTH), BF16),
            jax.ShapeDtypeStruct((t, SWA_Q_WIDTH), BF16),
            jax.ShapeDtypeStruct((t, SWA_KV_WIDTH), BF16),
            jax.ShapeDtypeStruct((t, SWA_KV_WIDTH), BF16),
            jax.ShapeDtypeStruct((t, MEM_WIDTH), BF16),
            jax.ShapeDtypeStruct((t, GATE_WIDTH), BF16),
        ],
        compiler_params=_cparams(("parallel", "parallel")),
    )(x2, g, w_in_b, b_gate)


def _gelu_tanh(x):
    c = math.sqrt(2.0 / math.pi)
    return 0.5 * x * (1.0 + jnp.tanh(c * (x + 0.044715 * (x * x * x))))


def _s5_kernel(u_ref, perm_ref, permt_ref, bmat_ref, are_ref, aim_ref, cmat_ref, d_ref, wv_ref, wg_ref,
               o_ref, st_ref, bu_ref, *, bsz, steps, col_chunk):
    nc = SSM_COMPLEX

    @pl.when(pl.program_id(0) == 0)
    def _():
        st_ref[...] = jnp.zeros_like(st_ref)

    u_bt = jnp.concatenate([u_ref[:, b * SSM_WIDTH:(b + 1) * SSM_WIDTH] for b in range(bsz)], axis=0)
    u = jnp.dot(perm_ref[...], u_bt, preferred_element_type=F32).astype(BF16)
    for m in range(S5_SLABS):
        um = u[:, m * LANES:(m + 1) * LANES]
        re = slice(m * S5_SLAB_STATES, (m + 1) * S5_SLAB_STATES)
        im = slice(nc + m * S5_SLAB_STATES, nc + (m + 1) * S5_SLAB_STATES)
        bu_ref[:, re] = jnp.dot(um, bmat_ref[m, :, :S5_SLAB_STATES], preferred_element_type=F32)
        bu_ref[:, im] = jnp.dot(um, bmat_ref[m, :, S5_SLAB_STATES:], preferred_element_type=F32)

    for lo in range(0, nc, col_chunk):
        re = slice(lo, lo + col_chunk)
        im = slice(nc + lo, nc + lo + col_chunk)
        ar = jnp.broadcast_to(are_ref[:, re], (bsz, col_chunk))
        ai = jnp.broadcast_to(aim_ref[:, re], (bsz, col_chunk))

        def step(t, carry):
            sr, si = carry
            rows = pl.ds(pl.multiple_of(t * bsz, bsz), bsz)
            nr = ar * sr - ai * si + bu_ref[rows, re]
            ni = ar * si + ai * sr + bu_ref[rows, im]
            bu_ref[rows, re] = nr
            bu_ref[rows, im] = ni
            return nr, ni

        sr, si = lax.fori_loop(0, steps, step, (st_ref[:, re], st_ref[:, im]))
        st_ref[:, re] = sr
        st_ref[:, im] = si

    ys = []
    for m in range(S5_SLABS):
        re = slice(m * S5_SLAB_STATES, (m + 1) * S5_SLAB_STATES)
        im = slice(nc + m * S5_SLAB_STATES, nc + (m + 1) * S5_SLAB_STATES)
        ys.append(jnp.dot(bu_ref[:, re].astype(BF16), cmat_ref[m, :S5_SLAB_STATES, :],
                          preferred_element_type=F32)
                  + jnp.dot(bu_ref[:, im].astype(BF16), cmat_ref[m, S5_SLAB_STATES:, :],
                            preferred_element_type=F32))
    y = jnp.concatenate(ys, axis=1) + d_ref[...] * u.astype(F32)
    yb = _gelu_tanh(y).astype(BF16)
    val = jnp.dot(yb, wv_ref[...], preferred_element_type=F32)
    gate = jnp.dot(yb, wg_ref[...], preferred_element_type=F32)
    out = (val * _sigmoid(gate)).astype(BF16)
    out_bt = jnp.dot(permt_ref[...], out, preferred_element_type=F32).astype(BF16)
    for b in range(bsz):
        o_ref[:, b * SSM_WIDTH:(b + 1) * SSM_WIDTH] = out_bt[b * steps:(b + 1) * steps]


def _s5_branch(u_tm, bmat, a_re, a_im, cmat, d_skip, wv, wg, bsz, seq, steps):
    rows = steps * bsz
    kern = functools.partial(_s5_kernel, bsz=bsz, steps=steps, col_chunk=512)
    r = jnp.arange(rows)
    perm = (((r % bsz) * steps + r // bsz)[:, None] == r[None, :]).astype(BF16)
    return pl.pallas_call(
        kern,
        grid=(seq // steps,),
        in_specs=[
            pl.BlockSpec((steps, bsz * SSM_WIDTH), lambda i: (i, 0)),
            _const_spec((rows, rows)),
            _const_spec((rows, rows)),
            _const_spec((S5_SLABS, LANES, 2 * S5_SLAB_STATES)),
            _const_spec((1, SSM_COMPLEX)),
            _const_spec((1, SSM_COMPLEX)),
            _const_spec((S5_SLABS, 2 * S5_SLAB_STATES, LANES)),
            _const_spec((1, SSM_WIDTH)),
            _const_spec((SSM_WIDTH, SSM_WIDTH)),
            _const_spec((SSM_WIDTH, SSM_WIDTH)),
        ],
        out_specs=pl.BlockSpec((steps, bsz * SSM_WIDTH), lambda i: (i, 0)),
        out_shape=jax.ShapeDtypeStruct((seq, bsz * SSM_WIDTH), BF16),
        scratch_shapes=[
            pltpu.VMEM((bsz, 2 * SSM_COMPLEX), F32),
            pltpu.VMEM((rows, 2 * SSM_COMPLEX), F32),
        ],
        compiler_params=_cparams(("arbitrary",)),
    )(u_tm, perm, perm.T, bmat, a_re, a_im, cmat, d_skip, wv, wg)


def _s5_discretize(log_dt, a_re, a_im, b_re, b_im, c_re, c_im):
    dt = jnp.exp(log_dt.astype(F32))[:, None]
    lr = jnp.minimum(a_re.astype(F32), -1e-4)
    li = a_im.astype(F32)
    mag = jnp.exp(lr * dt)
    abr = mag * jnp.cos(li * dt)
    abi = mag * jnp.sin(li * dt)
    nr, ni = abr - 1.0, abi
    den = lr * lr + li * li
    fr = (nr * lr + ni * li) / den
    fi = (ni * lr - nr * li) / den
    br, bi = b_re.astype(F32), b_im.astype(F32)
    bbr = fr[..., None] * br - fi[..., None] * bi
    bbi = fr[..., None] * bi + fi[..., None] * br
    gps = SSM_GROUPS // S5_SLABS
    eye = jnp.eye(gps, dtype=F32)

    def blockdiag_in(m):
        m = m.reshape(S5_SLABS, gps, SSM_STATE, SSM_GROUP)
        return jnp.einsum('sgpc,gh->sgchp', m, eye).reshape(S5_SLABS, LANES, S5_SLAB_STATES)

    def blockdiag_out(m):
        m = m.reshape(S5_SLABS, gps, SSM_GROUP, SSM_STATE)
        return jnp.einsum('sgcp,gh->sgphc', m, eye).reshape(S5_SLABS, S5_SLAB_STATES, LANES)

    bmat = jnp.concatenate([blockdiag_in(bbr), blockdiag_in(bbi)], axis=2).astype(BF16)
    cmat = jnp.concatenate([blockdiag_out(c_re.astype(F32)), -blockdiag_out(c_im.astype(F32))],
                           axis=1).astype(BF16)
    return bmat, abr.reshape(1, SSM_COMPLEX), abi.reshape(1, SSM_COMPLEX), cmat


def _memkv_kernel(mem_ref, g_ref, w_ref, kg_ref, mk_ref, mv_ref):
    x = mem_ref[...]
    h = x * lax.rsqrt(jnp.mean(x * x, axis=-1, keepdims=True) + NORM_EPS) * g_ref[...]
    kv = jnp.dot(h.astype(BF16), w_ref[...], preferred_element_type=F32)
    parts = []
    for hd in range(MEM_HEADS):
        kh = kv[:, hd * MEM_HEAD_DIM:(hd + 1) * MEM_HEAD_DIM]
        kh = kh * lax.rsqrt(jnp.mean(kh * kh, axis=-1, keepdims=True) + QK_EPS) * kg_ref[...]
        parts.append(kh)
    mk_ref[...] = jnp.concatenate(parts, axis=1).astype(BF16)
    mv_ref[...] = kv[:, MEM_WIDTH:].astype(BF16)


def _mem_kv(mem2, g, w_b, k_gain, bsz):
    return pl.pallas_call(
        _memkv_kernel,
        grid=(bsz,),
        in_specs=[
            pl.BlockSpec((MEM_LEN, D_MODEL), lambda b: (b, 0)),
            _const_spec((1, D_MODEL)),
            _const_spec((D_MODEL, 2 * MEM_WIDTH)),
            _const_spec((1, MEM_HEAD_DIM)),
        ],
        out_specs=[
            pl.BlockSpec((MEM_LEN, MEM_WIDTH), lambda b: (b, 0)),
            pl.BlockSpec((MEM_LEN, MEM_WIDTH), lambda b: (b, 0)),
        ],
        out_shape=[
            jax.ShapeDtypeStruct((bsz * MEM_LEN, MEM_WIDTH), BF16),
            jax.ShapeDtypeStruct((bsz * MEM_LEN, MEM_WIDTH), BF16),
        ],
        compiler_params=_cparams(("parallel",)),
    )(mem2, g, w_b, k_gain)


HEADS_PER_COL = LANES // HEAD_DIM
ROPE_CONST_ROWS = 16


def _split_bf16(a):
    hi = a.astype(BF16)
    lo = (a - hi.astype(F32)).astype(BF16)
    return hi, lo


def _rope_tables(pos_row, rc_ref, re_ref):
    tq = pos_row.shape[1]
    freq = jnp.concatenate([rc_ref[0:ROPE_HALF, :]] * (tq // LANES), axis=1)
    ang = freq * pos_row
    trig = jnp.concatenate([jnp.cos(ang), jnp.sin(ang),
                            jnp.zeros((LANES - 2 * ROPE_HALF, tq), F32)], axis=0)
    tab = jnp.dot(trig.T, re_ref[...], precision=lax.Precision.HIGHEST, preferred_element_type=F32)
    return tab[:, 0:LANES] + rc_ref[ROPE_HALF:ROPE_HALF + 1, :], tab[:, LANES:2 * LANES]


def _attn_kernel(sink_ref, q_ref, k_ref, v_ref, pos_ref, qm_ref, mk_ref, mv_ref,
                 rc_ref, re_ref, perm_ref, seg_ref, gmq_ref, oswa_ref, omem_ref, kbuf_ref, vbuf_ref, *, tq):
    nblk = tq // BLOCK
    first_tile = pl.program_id(1) == 0
    dot = lambda a, b: jnp.dot(a, b, preferred_element_type=F32)

    nver = kbuf_ref.shape[0] // 2
    mine = (pl.program_id(1) % 2) * nver
    other = nver - mine

    @pl.when(first_tile)
    def _():
        kbuf_ref[0:nver, 0:BLOCK, :] = jnp.zeros((nver, BLOCK, LANES), BF16)
        vbuf_ref[0:nver, 0:BLOCK, :] = jnp.zeros((nver, BLOCK, 2 * LANES), BF16)
        vbuf_ref[:, :, LANES:2 * LANES] = jnp.ones((2 * nver, BLOCK + tq, LANES), BF16)

    cos_t, sin_t = _rope_tables(pos_ref[0:1, :], rc_ref, re_ref)
    seg = seg_ref[...]
    perm = perm_ref[...]
    lo_half_pk = lax.broadcasted_iota(jnp.int32, (tq, LANES), 1) < HEAD_DIM
    lo_half_blk = lax.broadcasted_iota(jnp.int32, (BLOCK, LANES), 1) < HEAD_DIM

    def norm_rope(raw_bf, gain_cos, gain_sin):
        raw = raw_bf.astype(F32)
        hi, lo = _split_bf16(raw * raw)
        rs = lax.rsqrt((dot(hi, seg) + dot(lo, seg)) * (1.0 / HEAD_DIM) + QK_EPS)
        return (raw * gain_cos + dot(raw_bf, perm) * gain_sin) * rs

    q_cos = cos_t * rc_ref[ROPE_HALF + 1:ROPE_HALF + 2, :]
    q_sin = sin_t * rc_ref[ROPE_HALF + 2:ROPE_HALF + 3, :]
    k_cos = cos_t * rc_ref[ROPE_HALF + 3:ROPE_HALF + 4, :]
    k_sin = sin_t * rc_ref[ROPE_HALF + 4:ROPE_HALF + 5, :]

    for col in range(SWA_KV_WIDTH // LANES):
        cs = slice(col * LANES, (col + 1) * LANES)
        kc = norm_rope(k_ref[:, cs], k_cos, k_sin)
        versions = ((kbuf_ref, slice(None), kc.astype(BF16), pltpu.roll(kc, HEAD_DIM, 1).astype(BF16)),
                    (vbuf_ref, slice(0, LANES), v_ref[:, cs],
                     pltpu.roll(v_ref[:, cs].astype(F32), HEAD_DIM, 1).astype(BF16)))
        for buf, lanes, plain, rotated in versions:
            for ver, val in ((2 * col, plain), (2 * col + 1, rotated)):
                buf[mine + ver, BLOCK:BLOCK + tq, lanes] = val
                buf[other + ver, 0:BLOCK, lanes] = val[tq - BLOCK:tq]

    q_half = []
    for col in range(SWA_Q_WIDTH // LANES):
        cs = slice(col * LANES, (col + 1) * LANES)
        qc = norm_rope(q_ref[:, cs], q_cos, q_sin)
        q_half.append((jnp.where(lo_half_pk, qc, 0.0).astype(BF16), jnp.where(lo_half_pk, 0.0, qc).astype(BF16)))

    qi = lax.broadcasted_iota(jnp.int32, (2 * BLOCK, BLOCK), 0) & (BLOCK - 1)
    kj = lax.broadcasted_iota(jnp.int32, (2 * BLOCK, BLOCK), 1)
    own = kj <= qi
    no_prev = jnp.where(first_tile, 1, 0).astype(F32) * NEG_BIG

    for n in range(nblk):
        r0 = n * BLOCK
        for h in range(SWA_KV_HEADS):
            by_half = []
            for half in range(HEADS_PER_COL):
                ver = 2 * (h // HEADS_PER_COL) + (0 if half == h % HEADS_PER_COL else 1)
                qs = jnp.concatenate([q_half[2 * h][half][r0:r0 + BLOCK],
                                      q_half[2 * h + 1][half][r0:r0 + BLOCK]], axis=0)
                s2 = lax.dot_general(qs, kbuf_ref[mine + ver, r0:r0 + 2 * BLOCK, :], (((1,), (1,)), ((), ())),
                                     preferred_element_type=F32)
                s_prev = s2[:, :BLOCK] + no_prev if n == 0 else s2[:, :BLOCK]
                s = jnp.where(own, s2[:, BLOCK:], s_prev)
                sink = jnp.concatenate(
                    [jnp.full((BLOCK, BLOCK), sink_ref[h * SWA_GROUP + half], F32),
                     jnp.full((BLOCK, BLOCK), sink_ref[h * SWA_GROUP + half + HEADS_PER_COL], F32)], axis=0)
                m = jnp.maximum(jnp.broadcast_to(jnp.max(s, axis=-1, keepdims=True), s.shape), sink)
                p = jnp.exp(s - m)
                p2 = jnp.concatenate([jnp.where(own, 0.0, p).astype(BF16),
                                      jnp.where(own, p, 0.0).astype(BF16)], axis=1)
                o2 = dot(p2, vbuf_ref[mine + ver, r0:r0 + 2 * BLOCK, :])
                by_half.append(o2[:, :LANES] / (o2[:, LANES:] + jnp.exp(sink - m)))
            for sub in range(2):
                rs_ = slice(sub * BLOCK, (sub + 1) * BLOCK)
                colv = jnp.where(lo_half_blk, by_half[0][rs_], by_half[1][rs_])
                oswa_ref[r0:r0 + BLOCK, (2 * h + sub) * LANES:(2 * h + sub + 1) * LANES] = colv.astype(BF16)

    ones = jnp.ones((MEM_LEN, LANES), BF16)
    outs = []
    for hd in range(MEM_HEADS):
        sl = slice(hd * MEM_HEAD_DIM, (hd + 1) * MEM_HEAD_DIM)
        qh = qm_ref[:, sl].astype(F32)
        rs = lax.rsqrt(jnp.mean(qh * qh, axis=-1, keepdims=True) + QK_EPS)
        qh = qh * (rs * (MEM_HEAD_DIM ** -0.5)) * gmq_ref[...]
        s = lax.dot_general(qh.astype(BF16), mk_ref[:, sl], (((1,), (1,)), ((), ())),
                            preferred_element_type=F32)
        p = jnp.exp(s - jnp.max(s, axis=-1, keepdims=True)).astype(BF16)
        o2 = dot(p, jnp.concatenate([mv_ref[:, sl], ones], axis=1))
        outs.append(o2[:, :LANES] / o2[:, LANES:])
    omem_ref[...] = jnp.concatenate(outs, axis=1).astype(BF16)


def _rope_constants(q_gain, k_gain):
    inv_freq = ROPE_THETA ** (-jnp.arange(ROPE_HALF, dtype=F32) / ROPE_HALF)
    lane = jnp.arange(LANES) % HEAD_DIM
    j = jnp.arange(ROPE_HALF)[:, None]
    e_cos = ((lane[None, :] < ROPE_DIM) & (lane[None, :] % ROPE_HALF == j)).astype(F32)
    e_sin = (lane[None, :] == j + ROPE_HALF).astype(F32) - (lane[None, :] == j).astype(F32)
    ones = (lane >= ROPE_DIM).astype(F32)[None, :]
    src = jnp.where(lane < ROPE_HALF, jnp.arange(LANES) + ROPE_HALF, jnp.arange(LANES) - ROPE_HALF)
    rotary = lane < ROPE_DIM
    perm = ((jnp.arange(LANES)[:, None] == src[None, :]) & rotary[None, :]).astype(BF16)

    def gains(g, scale):
        col = jnp.tile(g.astype(F32).reshape(-1), HEADS_PER_COL) * scale
        return col[None, :], jnp.where(rotary, col[jnp.clip(src, 0, LANES - 1)], 0.0)[None, :]

    qg, qgp = gains(q_gain, HEAD_DIM ** -0.5)
    kg, kgp = gains(k_gain, 1.0)
    rows = jnp.concatenate([jnp.broadcast_to(inv_freq[:, None], (ROPE_HALF, LANES)), ones, qg, qgp, kg, kgp,
                            jnp.zeros((ROPE_CONST_ROWS - ROPE_HALF - 5, LANES), F32)], axis=0)
    zero8 = jnp.zeros((ROPE_HALF, LANES), F32)
    expand = jnp.concatenate([
        jnp.concatenate([e_cos, zero8], axis=1),
        jnp.concatenate([zero8, e_sin], axis=1),
        jnp.zeros((LANES - 2 * ROPE_HALF, 2 * LANES), F32)], axis=0)
    return rows, expand, perm


def _attention(sinks, q, k, v, pos_rows, qm, mk, mv, q_gain, k_gain, gmq, bsz, seq, tq):
    t = bsz * seq
    nj = seq // tq
    row = lambda b, j: (b * nj + j, 0)
    mem = lambda b, j: (b, 0)
    half = jnp.arange(LANES) // HEAD_DIM
    seg = (half[:, None] == half[None, :]).astype(BF16)
    kern = functools.partial(_attn_kernel, tq=tq)
    nver = 2 * (2 * SWA_KV_WIDTH // LANES)
    return pl.pallas_call(
        kern,
        grid=(bsz, nj),
        in_specs=[
            pl.BlockSpec(memory_space=pltpu.SMEM),
            pl.BlockSpec((tq, SWA_Q_WIDTH), row),
            pl.BlockSpec((tq, SWA_KV_WIDTH), row),
            pl.BlockSpec((tq, SWA_KV_WIDTH), row),
            pl.BlockSpec((SUBLANES, tq), row),
            pl.BlockSpec((tq, MEM_WIDTH), row),
            pl.BlockSpec((MEM_LEN, MEM_WIDTH), mem),
            pl.BlockSpec((MEM_LEN, MEM_WIDTH), mem),
            _const_spec((ROPE_CONST_ROWS, LANES)),
            _const_spec((LANES, 2 * LANES)),
            _const_spec((LANES, LANES)),
            _const_spec((LANES, LANES)),
            _const_spec((1, MEM_HEAD_DIM)),
        ],
        out_specs=[
            pl.BlockSpec((tq, SWA_Q_WIDTH), row),
            pl.BlockSpec((tq, MEM_WIDTH), row),
        ],
        out_shape=[
            jax.ShapeDtypeStruct((t, SWA_Q_WIDTH), BF16),
            jax.ShapeDtypeStruct((t, MEM_WIDTH), BF16),
        ],
        scratch_shapes=[
            pltpu.VMEM((nver, BLOCK + tq, LANES), BF16),
            pltpu.VMEM((nver, BLOCK + tq, 2 * LANES), BF16),
        ],
        compiler_params=_cparams(("parallel", "arbitrary")),
    )(sinks, q, k, v, pos_rows, qm, mk, mv, *_rope_constants(q_gain, k_gain), seg, gmq)


def _merge_kernel(x_ref, ys_ref, yw_ref, ym_ref, gate_ref, w0_ref, w1_ref, w2_ref, wo_ref, gf_ref,
                  wr_cat_ref, br_ref, x1_ref, tsl_ref, idx_ref, wt_ref, *, tm):
    gates = gate_ref[...]
    dot = lambda a, b: jnp.dot(a, b, preferred_element_type=F32)
    merged = (gates[:, 0:D_MODEL].astype(F32) * dot(ys_ref[...], w0_ref[...])
              + gates[:, D_MODEL:2 * D_MODEL].astype(F32) * dot(yw_ref[...], w1_ref[...])
              + gates[:, 2 * D_MODEL:].astype(F32) * dot(ym_ref[...], w2_ref[...]))
    x1 = x_ref[...] + dot(merged.astype(BF16), wo_ref[...])
    x1_ref[...] = x1
    t = x1 * lax.rsqrt(jnp.mean(x1 * x1, axis=-1, keepdims=True) + NORM_EPS) * gf_ref[...]

    for j in range(ROW_SUBLANES):
        tsl_ref[pl.ds(j, tm, stride=ROW_SUBLANES), :] = t[:, j * LANES:(j + 1) * LANES]

    t_hi, t_lo = _split_bf16(t)
    both = dot(t_hi, wr_cat_ref[...])
    logits = both[:, :LANES] + both[:, LANES:] + dot(t_lo, wr_cat_ref[:, :LANES]) + br_ref[...]
    lane = lax.broadcasted_iota(jnp.int32, logits.shape, 1).astype(F32)
    work = logits
    vals, firsts = [], []
    for _ in range(TOP_K):
        m = jnp.max(work, axis=-1, keepdims=True)
        first = jnp.min(jnp.where(work == m, lane, float(LANES)), axis=-1, keepdims=True)
        work = jnp.where(lane == first, NEG_BIG * 2.0, work)
        vals.append(m)
        firsts.append(first)
    exps = [jnp.exp(v - vals[0]) for v in vals]
    den = exps[0] + exps[1] + exps[2] + exps[3]
    idx_out = jnp.zeros_like(logits)
    wt_out = jnp.zeros_like(logits)
    for k in range(TOP_K):
        idx_out = jnp.where(lane == float(k), firsts[k], idx_out)
        wt_out = jnp.where(lane == float(k), exps[k] / den, wt_out)
    idx_ref[...] = idx_out.T[0:SUBLANES, :].astype(jnp.int32)
    wt_ref[...] = wt_out.T[0:SUBLANES, :]


def _merge(x2, y_ssm_tm, y_swa, y_mem, gates, w0, w1, w2, wo, gf, wr_cat, br, bsz, seq, tm):
    t = bsz * seq
    nj = seq // tm
    row = lambda b, j: (b * nj + j, 0)
    return pl.pallas_call(
        functools.partial(_merge_kernel, tm=tm),
        grid=(bsz, nj),
        in_specs=[
            pl.BlockSpec((tm, D_MODEL), row),
            pl.BlockSpec((tm, SSM_WIDTH), lambda b, j: (j, b)),
            pl.BlockSpec((tm, SWA_Q_WIDTH), row),
            pl.BlockSpec((tm, MEM_WIDTH), row),
            pl.BlockSpec((tm, GATE_WIDTH), row),
            _const_spec((SSM_WIDTH, D_MODEL)),
            _const_spec((SWA_Q_WIDTH, D_MODEL)),
            _const_spec((MEM_WIDTH, D_MODEL)),
            _const_spec((D_MODEL, D_MODEL)),
            _const_spec((1, D_MODEL)),
            _const_spec((D_MODEL, 2 * LANES)),
            _const_spec((1, LANES)),
        ],
        out_specs=[
            pl.BlockSpec((tm, D_MODEL), row),
            pl.BlockSpec((tm * ROW_SUBLANES, LANES), row),
            pl.BlockSpec((SUBLANES, tm), row),
            pl.BlockSpec((SUBLANES, tm), row),
        ],
        out_shape=[
            jax.ShapeDtypeStruct((t, D_MODEL), F32),
            jax.ShapeDtypeStruct((t * ROW_SUBLANES, LANES), F32),
            jax.ShapeDtypeStruct((t // tm * SUBLANES, tm), jnp.int32),
            jax.ShapeDtypeStruct((t // tm * SUBLANES, tm), F32),
        ],
        compiler_params=_cparams(("parallel", "parallel")),
    )(x2, y_ssm_tm, y_swa, y_mem, gates, w0, w1, w2, wo, gf, wr_cat, br)


DEINT_BLOCK = 2 * LANES


def _expert_prep_kernel(w1_ref, perm_ref, w1d_ref):
    perm = perm_ref[...]
    for blk in range(2 * D_EXPERT // DEINT_BLOCK):
        cols = w1_ref[0, :, blk * DEINT_BLOCK:(blk + 1) * DEINT_BLOCK].astype(BF16)
        z = jnp.dot(cols, perm, preferred_element_type=F32).astype(BF16)
        w1d_ref[0, :, blk * LANES:(blk + 1) * LANES] = z[:, :LANES]
        w1d_ref[0, :, D_EXPERT + blk * LANES:D_EXPERT + (blk + 1) * LANES] = z[:, LANES:]


def _expert_prep(w1):
    src = jnp.arange(DEINT_BLOCK)
    dst = jnp.where(src % 2 == 0, src // 2, LANES + src // 2)
    perm = (dst[:, None] == jnp.arange(DEINT_BLOCK)[None, :]).astype(BF16)
    exp3 = lambda e: (e, 0, 0)
    return pl.pallas_call(
        _expert_prep_kernel,
        grid=(N_EXPERTS,),
        in_specs=[pl.BlockSpec((1, D_MODEL, 2 * D_EXPERT), exp3), _const_spec((DEINT_BLOCK, DEINT_BLOCK))],
        out_specs=pl.BlockSpec((1, D_MODEL, 2 * D_EXPERT), exp3),
        out_shape=jax.ShapeDtypeStruct((N_EXPERTS, D_MODEL, 2 * D_EXPERT), BF16),
        compiler_params=_cparams(("parallel",)),
    )(w1, perm)


SCATTER_BATCH = 8


def _moe_kernel(cnt_ref, off_ref, tok_ref, wt_ref, src_ref, w1_ref, w2_ref, bias_ref,
                out_hbm, acc_ref, gbuf_ref, ybuf_ref, state_ref, sem, *, nb_tokens, tile):
    blk = pl.program_id(0)
    e = pl.program_id(1)
    acc_rows = nb_tokens * ROW_SUBLANES
    buf_rows = tile * ROW_SUBLANES

    def gather(base, half):
        row0 = half * buf_rows
        for i in range(tile):
            tok = tok_ref[0, 0, base + i]
            gbuf_ref[pl.ds(pl.multiple_of(row0 + i * ROW_SUBLANES, ROW_SUBLANES), ROW_SUBLANES), :] = (
                src_ref[pl.ds(pl.multiple_of(tok * ROW_SUBLANES, ROW_SUBLANES), ROW_SUBLANES), :])

    def scatter(base, nvalid, half):
        row0 = half * buf_rows
        for i0 in range(0, tile, SCATTER_BATCH):
            sums, dsts = [], []
            for i in range(i0, i0 + SCATTER_BATCH):
                live = i < nvalid
                tok = jnp.where(live, tok_ref[0, 0, base + i], nb_tokens)
                w = jnp.where(live, wt_ref[0, 0, base + i], 0.0)
                dst = pl.ds(pl.multiple_of(tok * ROW_SUBLANES, ROW_SUBLANES), ROW_SUBLANES)
                src = pl.ds(pl.multiple_of(row0 + i * ROW_SUBLANES, ROW_SUBLANES), ROW_SUBLANES)
                sums.append(acc_ref[dst, :] + w * ybuf_ref[src, :])
                dsts.append(dst)
            for dst, s in zip(dsts, sums):
                acc_ref[dst, :] = s

    def writeback(block):
        return pltpu.make_async_copy(acc_ref.at[pl.ds(0, acc_rows), :],
                                     out_hbm.at[pl.ds(block * acc_rows, acc_rows), :], sem)

    @pl.when(e == 0)
    def _():
        @pl.when(blk > 0)
        def _():
            writeback(blk - 1).wait()

        acc_ref[...] = jnp.zeros_like(acc_ref)
        ybuf_ref[...] = jnp.zeros_like(ybuf_ref)
        state_ref[0] = 0
        state_ref[1] = 0
        state_ref[2] = 0
        gather(0, 0)

    n = cnt_ref[blk * N_EXPERTS + e]
    start = off_ref[blk * N_EXPERTS + e]
    ntiles = (n + tile - 1) // tile

    def stages(rows, next_base, prev_base, prev_n, half):
        other = 1 - half
        gather(next_base, other)
        row0 = half * buf_rows
        x = jnp.concatenate(
            [gbuf_ref[pl.ds(row0 + j, rows, stride=ROW_SUBLANES), :] for j in range(ROW_SUBLANES)],
            axis=1).astype(BF16)
        hg = jnp.dot(x, w1_ref[0, :, :D_EXPERT], preferred_element_type=F32) + bias_ref[0, :, :D_EXPERT]
        hl = (jnp.dot(x, w1_ref[0, :, D_EXPERT:], preferred_element_type=F32)
              + bias_ref[0, :, D_EXPERT:2 * D_EXPERT])
        xg = jnp.minimum(hg, SWIGLU_LIMIT)
        xl = jnp.clip(hl, -SWIGLU_LIMIT, SWIGLU_LIMIT)
        act = xg * _sigmoid(SWIGLU_ALPHA * xg) * (xl + 1.0)
        y = jnp.dot(act.astype(BF16), w2_ref[0], preferred_element_type=F32) + bias_ref[0, :, 2 * D_EXPERT:]
        scatter(prev_base, prev_n, other)
        for j in range(ROW_SUBLANES):
            ybuf_ref[pl.ds(row0 + j, rows, stride=ROW_SUBLANES), :] = y[:, j * LANES:(j + 1) * LANES]

    def tile_body(ti, carry):
        prev_base, prev_n, half = carry
        base = start + ti * tile
        live = n - ti * tile
        next_base = jnp.where(ti + 1 < ntiles, base + tile, start + n)
        lax.cond(live <= MOE_SMALL_TILE,
                 functools.partial(stages, MOE_SMALL_TILE, next_base, prev_base, prev_n, half),
                 functools.partial(stages, tile, next_base, prev_base, prev_n, half))
        return base, live, 1 - half

    prev_base, prev_n, half = lax.fori_loop(0, ntiles, tile_body,
                                            (state_ref[0], state_ref[1], state_ref[2]))
    state_ref[0] = prev_base
    state_ref[1] = prev_n
    state_ref[2] = half

    @pl.when(e == N_EXPERTS - 1)
    def _():
        scatter(prev_base, prev_n, 1 - half)
        writeback(blk).start()

        @pl.when(blk == pl.num_programs(0) - 1)
        def _():
            writeback(blk).wait()


def _moe(counts, offsets, tok_sorted, wt_sorted, tsl, w1d, w2, bias, t, nb_tokens, tile):
    nblk = t // nb_tokens
    slots = tok_sorted.shape[-1]
    exp3 = lambda b, e, *_: (e, 0, 0)
    blk3 = lambda b, e, *_: (b, 0, 0)
    grid_spec = pltpu.PrefetchScalarGridSpec(
        num_scalar_prefetch=2,
        grid=(nblk, N_EXPERTS),
        in_specs=[
            pl.BlockSpec((1, 1, slots), blk3, memory_space=pltpu.SMEM),
            pl.BlockSpec((1, 1, slots), blk3, memory_space=pltpu.SMEM),
            pl.BlockSpec((nb_tokens * ROW_SUBLANES, LANES), lambda b, e, *_: (b, 0),
                         pipeline_mode=pl.Buffered(1)),
            pl.BlockSpec((1, D_MODEL, 2 * D_EXPERT), exp3),
            pl.BlockSpec((1, D_EXPERT, D_MODEL), exp3),
            pl.BlockSpec((1, 1, 2 * D_EXPERT + D_MODEL), exp3),
        ],
        out_specs=pl.BlockSpec(memory_space=pl.ANY),
        scratch_shapes=[
            pltpu.VMEM(((nb_tokens + 1) * ROW_SUBLANES, LANES), F32),
            pltpu.VMEM((2 * tile * ROW_SUBLANES, LANES), F32),
            pltpu.VMEM((2 * tile * ROW_SUBLANES, LANES), F32),
            pltpu.SMEM((3,), jnp.int32),
            pltpu.SemaphoreType.DMA,
        ],
    )
    return pl.pallas_call(
        functools.partial(_moe_kernel, nb_tokens=nb_tokens, tile=tile),
        grid_spec=grid_spec,
        out_shape=jax.ShapeDtypeStruct((t * ROW_SUBLANES, LANES), F32),
        compiler_params=_cparams(("arbitrary", "arbitrary")),
    )(counts, offsets, tok_sorted, wt_sorted, tsl, w1d, w2, bias)


def _route(idx_t, wts_t, t, nb_tokens, tm, tile):
    nblk = t // nb_tokens
    pick = lambda a: a.reshape(t // tm, SUBLANES, tm)[:, :TOP_K, :].reshape(nblk, nb_tokens * TOP_K)
    eid, w = pick(idx_t), pick(wts_t)
    span = nb_tokens * TOP_K
    key = eid * span + lax.broadcasted_iota(jnp.int32, (1, span), 1)
    key_sorted, wt_sorted = lax.sort((key, w), dimension=1, is_stable=False, num_keys=1)
    flat = key_sorted % span
    tok_sorted = (flat // (TOP_K * tm)) * tm + flat % tm
    counts = jnp.sum((eid[:, :, None] == jnp.arange(N_EXPERTS)[None, None, :]).astype(jnp.int32), axis=1)
    offsets = jnp.cumsum(counts, axis=1) - counts
    pad = ((0, 0), (0, tile))
    tok_sorted = jnp.pad(tok_sorted, pad)[:, None, :]
    wt_sorted = jnp.pad(wt_sorted, pad)[:, None, :]
    return counts.reshape(-1), offsets.reshape(-1).astype(jnp.int32), tok_sorted, wt_sorted


def _residual_kernel(x1_ref, moe_ref, o_ref, *, tm):
    parts = [moe_ref[pl.ds(j, tm, stride=ROW_SUBLANES), :] for j in range(ROW_SUBLANES)]
    o_ref[...] = x1_ref[...] + jnp.concatenate(parts, axis=1)


def _residual(x1, moe, tm):
    t = x1.shape[0]
    return pl.pallas_call(
        functools.partial(_residual_kernel, tm=tm),
        grid=(t // tm,),
        in_specs=[pl.BlockSpec((tm, D_MODEL), lambda i: (i, 0)),
                  pl.BlockSpec((tm * ROW_SUBLANES, LANES), lambda i: (i, 0))],
        out_specs=pl.BlockSpec((tm, D_MODEL), lambda i: (i, 0)),
        out_shape=jax.ShapeDtypeStruct((t, D_MODEL), F32),
        compiler_params=_cparams(("parallel",)),
    )(x1, moe)


def kernel(x, mem, positions, attn_norm_g, mem_norm_g, w_in, b_gate, ssm_log_dt, ssm_a_re, ssm_a_im,
           ssm_b_re, ssm_b_im, ssm_c_re, ssm_c_im, ssm_d, w_glu_v, w_glu_g, swa_q_gain, swa_k_gain,
           swa_sinks, w_mem_kv, mem_q_gain, mem_k_gain, w_br_ssm, w_br_swa, w_br_mem, w_out,
           ffn_norm_g, w_router, b_router, w_mlp1, b_mlp1, w_mlp2, b_mlp2):
    bsz, seq, _ = x.shape
    depth = w_in.shape[0]
    t = bsz * seq
    tm = min(512, seq)
    tq = min(1024, seq)
    steps = min(32, seq)

    pos_rows = jnp.broadcast_to(positions.astype(F32).reshape(t // tq, 1, tq),
                                (t // tq, SUBLANES, tq)).reshape(t // tq * SUBLANES, tq)
    mem2 = mem.reshape(bsz * MEM_LEN, D_MODEL)

    x2 = x.reshape(t, D_MODEL)
    for l in range(depth):
        row = lambda a: a[l].reshape(1, -1).astype(F32)
        u_tm, q, k, v, qm, gates = _in_projection(
            x2, row(attn_norm_g), w_in[l].astype(BF16), row(b_gate), bsz, seq, tm)

        bmat, abr, abi, cmat = _s5_discretize(ssm_log_dt[l], ssm_a_re[l], ssm_a_im[l], ssm_b_re[l],
                                              ssm_b_im[l], ssm_c_re[l], ssm_c_im[l])
        y_ssm_tm = _s5_branch(u_tm, bmat, abr, abi, cmat, row(ssm_d),
                              w_glu_v[l].astype(BF16), w_glu_g[l].astype(BF16), bsz, seq, steps)

        mk, mv = _mem_kv(mem2, row(mem_norm_g), w_mem_kv[l].astype(BF16), row(mem_k_gain), bsz)
        y_swa, y_mem = _attention(
            swa_sinks[l].astype(F32), q, k, v, pos_rows, qm, mk, mv, swa_q_gain[l], swa_k_gain[l],
            row(mem_q_gain), bsz, seq, tq)

        wr = jnp.pad(w_router[l].astype(F32), ((0, 0), (0, LANES - N_EXPERTS)))
        wr_hi = wr.astype(BF16)
        wr_lo = (wr - wr_hi.astype(F32)).astype(BF16)
        br = jnp.pad(b_router[l].astype(F32), (0, LANES - N_EXPERTS), constant_values=NEG_BIG).reshape(1, LANES)
        x1, tsl, idx, wts = _merge(x2, y_ssm_tm, y_swa, y_mem, gates, w_br_ssm[l].astype(BF16),
                                   w_br_swa[l].astype(BF16), w_br_mem[l].astype(BF16),
                                   w_out[l].astype(BF16), row(ffn_norm_g), jnp.concatenate([wr_hi, wr_lo], axis=1), br, bsz, seq, tm)

        nb_tokens = min(MOE_TOKEN_BLOCK, t)
        idx, w1_l, w2_l = lax.optimization_barrier((idx, w_mlp1[l], w_mlp2[l]))
        counts, offsets, tok_sorted, wt_sorted = _route(idx, wts, t, nb_tokens, tm, MOE_TILE)
        b1 = b_mlp1[l].astype(F32)
        bias = jnp.concatenate([b1[:, 0::2], b1[:, 1::2], b_mlp2[l].astype(F32)], axis=1)[:, None, :]
        moe = _moe(counts, offsets, tok_sorted, wt_sorted, tsl, _expert_prep(w1_l), w2_l.astype(BF16), bias,
                   t, nb_tokens, MOE_TILE)
        x2 = _residual(x1, moe, min(1024, t))
    return x2.reshape(bsz, seq, D_MODEL)
```

```python
import functools
import math

import jax
import jax.numpy as jnp
from jax import lax
from jax.experimental import pallas as pl
from jax.experimental.pallas import tpu as pltpu

F32 = jnp.float32
BF16 = jnp.bfloat16

LANES = 128
SUBLANES = 8
VMEM_LIMIT = 56 * 1024 * 1024
NEG_BIG = -1e30

D_MODEL = 1024
MEM_LEN = 256
NORM_EPS = 1e-5
QK_EPS = 1e-6

SSM_WIDTH = 512
SSM_GROUP = 16
SSM_GROUPS = SSM_WIDTH // SSM_GROUP
SSM_STATE = 64
SSM_COMPLEX = SSM_GROUPS * SSM_STATE
S5_SLABS = SSM_WIDTH // LANES
S5_SLAB_STATES = SSM_COMPLEX // S5_SLABS
S5_STEPS = 32
S5_SCAN_COLS = 512

HEAD_DIM = 64
SWA_KV_HEADS = 4
SWA_GROUP = 4
BLOCK = 128
ROPE_THETA = 500000.0
ROPE_DIM = HEAD_DIM // 4
ROPE_HALF = ROPE_DIM // 2

MEM_HEADS = 4
MEM_HEAD_DIM = 128

SWA_Q_WIDTH = SWA_KV_HEADS * SWA_GROUP * HEAD_DIM
SWA_KV_WIDTH = SWA_KV_HEADS * HEAD_DIM
MEM_WIDTH = MEM_HEADS * MEM_HEAD_DIM
N_BRANCH = 3
GATE_WIDTH = N_BRANCH * D_MODEL
OFF_U = 0
OFF_Q = OFF_U + SSM_WIDTH
OFF_K = OFF_Q + SWA_Q_WIDTH
OFF_V = OFF_K + SWA_KV_WIDTH
OFF_QM = OFF_V + SWA_KV_WIDTH
OFF_G = OFF_QM + MEM_WIDTH
OFF_END = OFF_G + GATE_WIDTH

N_EXPERTS = 32
TOP_K = 4
D_EXPERT = D_MODEL
SWIGLU_ALPHA = 1.702
SWIGLU_LIMIT = 7.0

ROW_SUBLANES = D_MODEL // LANES
ROW_TILE = 512
ATTN_TILE = 1024
RESIDUAL_TILE = 1024
MOE_TOKEN_BLOCK = 4096
MOE_TILE = 272
MOE_SMALL_TILE = 144


def _sigmoid(x):
    return 1.0 / (1.0 + jnp.exp(-x))


def _cparams(sem):
    return pltpu.CompilerParams(dimension_semantics=sem, vmem_limit_bytes=VMEM_LIMIT)


def _const_spec(shape):
    nd = len(shape)
    return pl.BlockSpec(shape, lambda *_: (0,) * nd)


def _inproj_kernel(x_ref, g_ref, w_ref, bg_ref, u_ref, q_ref, k_ref, v_ref, qm_ref, gate_ref):
    x = x_ref[...]
    h = x * lax.rsqrt(jnp.mean(x * x, axis=-1, keepdims=True) + NORM_EPS) * g_ref[...]
    hb = h.astype(BF16)

    def proj(lo, hi):
        return jnp.dot(hb, w_ref[:, lo:hi], preferred_element_type=F32)

    u_ref[...] = proj(OFF_U, OFF_Q).astype(BF16)
    q_ref[...] = proj(OFF_Q, OFF_K).astype(BF16)
    k_ref[...] = proj(OFF_K, OFF_V).astype(BF16)
    v_ref[...] = proj(OFF_V, OFF_QM).astype(BF16)
    qm_ref[...] = proj(OFF_QM, OFF_G).astype(BF16)
    gate_ref[...] = _sigmoid(proj(OFF_G, OFF_END) + bg_ref[...]).astype(BF16)


def _in_projection(x2, g, w_in_b, b_gate, bsz, seq, tm):
    t = bsz * seq
    nj = seq // tm
    row = lambda b, j: (b * nj + j, 0)
    return pl.pallas_call(
        _inproj_kernel,
        grid=(bsz, nj),
        in_specs=[
            pl.BlockSpec((tm, D_MODEL), row),
            _const_spec((1, D_MODEL)),
            _const_spec((D_MODEL, OFF_END)),
            _const_spec((1, GATE_WIDTH)),
        ],
        out_specs=[
            pl.BlockSpec((tm, SSM_WIDTH), lambda b, j: (j, b)),
            pl.BlockSpec((tm, SWA_Q_WIDTH), row),
            pl.BlockSpec((tm, SWA_KV_WIDTH), row),
            pl.BlockSpec((tm, SWA_KV_WIDTH), row),
            pl.BlockSpec((tm, MEM_WIDTH), row),
            pl.BlockSpec((tm, GATE_WIDTH), row),
        ],
        out_shape=[
            jax.ShapeDtypeStruct((seq, bsz * SSM_WIDTH), BF16),
            jax.ShapeDtypeStruct((t, SWA_Q_WIDTH), BF16),
            jax.ShapeDtypeStruct((t, SWA_KV_WIDTH), BF16),
            jax.ShapeDtypeStruct((t, SWA_KV_WIDTH), BF16),
            jax.ShapeDtypeStruct((t, MEM_WIDTH), BF16),
            jax.ShapeDtypeStruct((t, GATE_WIDTH), BF16),
        ],
        compiler_params=_cparams(("parallel", "parallel")),
    )(x2, g, w_in_b, b_gate)


def _gelu_tanh(x):
    c = math.sqrt(2.0 / math.pi)
    return 0.5 * x * (1.0 + jnp.tanh(c * (x + 0.044715 * (x * x * x))))


def _s5_kernel(u_ref, perm_ref, permt_ref, bmat_ref, are_ref, aim_ref, cmat_ref, d_ref, wv_ref, wg_ref,
               o_ref, st_ref, bu_ref, *, bsz, steps, col_chunk):
    nc = SSM_COMPLEX

    @pl.when(pl.program_id(0) == 0)
    def _():
        st_ref[...] = jnp.zeros_like(st_ref)

    u_bt = jnp.concatenate([u_ref[:, b * SSM_WIDTH:(b + 1) * SSM_WIDTH] for b in range(bsz)], axis=0)
    u = jnp.dot(perm_ref[...], u_bt, preferred_element_type=F32).astype(BF16)
    for m in range(S5_SLABS):
        um = u[:, m * LANES:(m + 1) * LANES]
        re = slice(m * S5_SLAB_STATES, (m + 1) * S5_SLAB_STATES)
        im = slice(nc + m * S5_SLAB_STATES, nc + (m + 1) * S5_SLAB_STATES)
        bu_ref[:, re] = jnp.dot(um, bmat_ref[m, :, :S5_SLAB_STATES], preferred_element_type=F32)
        bu_ref[:, im] = jnp.dot(um, bmat_ref[m, :, S5_SLAB_STATES:], preferred_element_type=F32)

    for lo in range(0, nc, col_chunk):
        re = slice(lo, lo + col_chunk)
        im = slice(nc + lo, nc + lo + col_chunk)
        ar = jnp.broadcast_to(are_ref[:, re], (bsz, col_chunk))
        ai = jnp.broadcast_to(aim_ref[:, re], (bsz, col_chunk))

        def step(t, carry):
            sr, si = carry
            rows = pl.ds(pl.multiple_of(t * bsz, bsz), bsz)
            nr = ar * sr - ai * si + bu_ref[rows, re]
            ni = ar * si + ai * sr + bu_ref[rows, im]
            bu_ref[rows, re] = nr
            bu_ref[rows, im] = ni
            return nr, ni

        sr, si = lax.fori_loop(0, steps, step, (st_ref[:, re], st_ref[:, im]))
        st_ref[:, re] = sr
        st_ref[:, im] = si

    ys = []
    for m in range(S5_SLABS):
        re = slice(m * S5_SLAB_STATES, (m + 1) * S5_SLAB_STATES)
        im = slice(nc + m * S5_SLAB_STATES, nc + (m + 1) * S5_SLAB_STATES)
        ys.append(jnp.dot(bu_ref[:, re].astype(BF16), cmat_ref[m, :S5_SLAB_STATES, :],
                          preferred_element_type=F32)
                  + jnp.dot(bu_ref[:, im].astype(BF16), cmat_ref[m, S5_SLAB_STATES:, :],
                            preferred_element_type=F32))
    y = jnp.concatenate(ys, axis=1) + d_ref[...] * u.astype(F32)
    yb = _gelu_tanh(y).astype(BF16)
    val = jnp.dot(yb, wv_ref[...], preferred_element_type=F32)
    gate = jnp.dot(yb, wg_ref[...], preferred_element_type=F32)
    out = (val * _sigmoid(gate)).astype(BF16)
    out_bt = jnp.dot(permt_ref[...], out, preferred_element_type=F32).astype(BF16)
    for b in range(bsz):
        o_ref[:, b * SSM_WIDTH:(b + 1) * SSM_WIDTH] = out_bt[b * steps:(b + 1) * steps]


def _s5_branch(u_tm, bmat, a_re, a_im, cmat, d_skip, wv, wg, bsz, seq, steps):
    rows = steps * bsz
    kern = functools.partial(_s5_kernel, bsz=bsz, steps=steps, col_chunk=S5_SCAN_COLS)
    r = jnp.arange(rows)
    perm = (((r % bsz) * steps + r // bsz)[:, None] == r[None, :]).astype(BF16)
    return pl.pallas_call(
        kern,
        grid=(seq // steps,),
        in_specs=[
            pl.BlockSpec((steps, bsz * SSM_WIDTH), lambda i: (i, 0)),
            _const_spec((rows, rows)),
            _const_spec((rows, rows)),
            _const_spec((S5_SLABS, LANES, 2 * S5_SLAB_STATES)),
            _const_spec((1, SSM_COMPLEX)),
            _const_spec((1, SSM_COMPLEX)),
            _const_spec((S5_SLABS, 2 * S5_SLAB_STATES, LANES)),
            _const_spec((1, SSM_WIDTH)),
            _const_spec((SSM_WIDTH, SSM_WIDTH)),
            _const_spec((SSM_WIDTH, SSM_WIDTH)),
        ],
        out_specs=pl.BlockSpec((steps, bsz * SSM_WIDTH), lambda i: (i, 0)),
        out_shape=jax.ShapeDtypeStruct((seq, bsz * SSM_WIDTH), BF16),
        scratch_shapes=[
            pltpu.VMEM((bsz, 2 * SSM_COMPLEX), F32),
            pltpu.VMEM((rows, 2 * SSM_COMPLEX), F32),
        ],
        compiler_params=_cparams(("arbitrary",)),
    )(u_tm, perm, perm.T, bmat, a_re, a_im, cmat, d_skip, wv, wg)


def _s5_discretize(log_dt, a_re, a_im, b_re, b_im, c_re, c_im):
    dt = jnp.exp(log_dt.astype(F32))[:, None]
    lr = jnp.minimum(a_re.astype(F32), -1e-4)
    li = a_im.astype(F32)
    mag = jnp.exp(lr * dt)
    abr = mag * jnp.cos(li * dt)
    abi = mag * jnp.sin(li * dt)
    nr, ni = abr - 1.0, abi
    den = lr * lr + li * li
    fr = (nr * lr + ni * li) / den
    fi = (ni * lr - nr * li) / den
    br, bi = b_re.astype(F32), b_im.astype(F32)
    bbr = fr[..., None] * br - fi[..., None] * bi
    bbi = fr[..., None] * bi + fi[..., None] * br
    gps = SSM_GROUPS // S5_SLABS
    eye = jnp.eye(gps, dtype=F32)

    def blockdiag_in(m):
        m = m.reshape(S5_SLABS, gps, SSM_STATE, SSM_GROUP)
        return jnp.einsum('sgpc,gh->sgchp', m, eye).reshape(S5_SLABS, LANES, S5_SLAB_STATES)

    def blockdiag_out(m):
        m = m.reshape(S5_SLABS, gps, SSM_GROUP, SSM_STATE)
        return jnp.einsum('sgcp,gh->sgphc', m, eye).reshape(S5_SLABS, S5_SLAB_STATES, LANES)

    bmat = jnp.concatenate([blockdiag_in(bbr), blockdiag_in(bbi)], axis=2).astype(BF16)
    cmat = jnp.concatenate([blockdiag_out(c_re.astype(F32)), -blockdiag_out(c_im.astype(F32))],
                           axis=1).astype(BF16)
    return bmat, abr.reshape(1, SSM_COMPLEX), abi.reshape(1, SSM_COMPLEX), cmat


def _memkv_kernel(mem_ref, g_ref, w_ref, kg_ref, mk_ref, mv_ref):
    x = mem_ref[...]
    h = x * lax.rsqrt(jnp.mean(x * x, axis=-1, keepdims=True) + NORM_EPS) * g_ref[...]
    kv = jnp.dot(h.astype(BF16), w_ref[...], preferred_element_type=F32)
    parts = []
    for hd in range(MEM_HEADS):
        kh = kv[:, hd * MEM_HEAD_DIM:(hd + 1) * MEM_HEAD_DIM]
        kh = kh * lax.rsqrt(jnp.mean(kh * kh, axis=-1, keepdims=True) + QK_EPS) * kg_ref[...]
        parts.append(kh)
    mk_ref[...] = jnp.concatenate(parts, axis=1).astype(BF16)
    mv_ref[...] = kv[:, MEM_WIDTH:].astype(BF16)


def _mem_kv(mem2, g, w_b, k_gain, bsz):
    return pl.pallas_call(
        _memkv_kernel,
        grid=(bsz,),
        in_specs=[
            pl.BlockSpec((MEM_LEN, D_MODEL), lambda b: (b, 0)),
            _const_spec((1, D_MODEL)),
            _const_spec((D_MODEL, 2 * MEM_WIDTH)),
            _const_spec((1, MEM_HEAD_DIM)),
        ],
        out_specs=[
            pl.BlockSpec((MEM_LEN, MEM_WIDTH), lambda b: (b, 0)),
            pl.BlockSpec((MEM_LEN, MEM_WIDTH), lambda b: (b, 0)),
        ],
        out_shape=[
            jax.ShapeDtypeStruct((bsz * MEM_LEN, MEM_WIDTH), BF16),
            jax.ShapeDtypeStruct((bsz * MEM_LEN, MEM_WIDTH), BF16),
        ],
        compiler_params=_cparams(("parallel",)),
    )(mem2, g, w_b, k_gain)


HEADS_PER_COL = LANES // HEAD_DIM
ROPE_CONST_ROWS = 16


def _split_bf16(a):
    hi = a.astype(BF16)
    lo = (a - hi.astype(F32)).astype(BF16)
    return hi, lo


def _rope_tables(pos_row, rc_ref, re_ref):
    tq = pos_row.shape[1]
    freq = jnp.concatenate([rc_ref[0:ROPE_HALF, :]] * (tq // LANES), axis=1)
    ang = freq * pos_row
    trig = jnp.concatenate([jnp.cos(ang), jnp.sin(ang),
                            jnp.zeros((LANES - 2 * ROPE_HALF, tq), F32)], axis=0)
    tab = jnp.dot(trig.T, re_ref[...], precision=lax.Precision.HIGHEST, preferred_element_type=F32)
    return tab[:, 0:LANES] + rc_ref[ROPE_HALF:ROPE_HALF + 1, :], tab[:, LANES:2 * LANES]


def _attn_kernel(sink_ref, q_ref, k_ref, v_ref, pos_ref, qm_ref, mk_ref, mv_ref,
                 rc_ref, re_ref, perm_ref, seg_ref, gmq_ref, oswa_ref, omem_ref, kbuf_ref, vbuf_ref, *, tq):
    nblk = tq // BLOCK
    first_tile = pl.program_id(1) == 0
    dot = lambda a, b: jnp.dot(a, b, preferred_element_type=F32)

    nver = kbuf_ref.shape[0] // 2
    mine = (pl.program_id(1) % 2) * nver
    other = nver - mine

    @pl.when(first_tile)
    def _():
        kbuf_ref[0:nver, 0:BLOCK, :] = jnp.zeros((nver, BLOCK, LANES), BF16)
        vbuf_ref[0:nver, 0:BLOCK, :] = jnp.zeros((nver, BLOCK, 2 * LANES), BF16)
        vbuf_ref[:, :, LANES:2 * LANES] = jnp.ones((2 * nver, BLOCK + tq, LANES), BF16)

    cos_t, sin_t = _rope_tables(pos_ref[0:1, :], rc_ref, re_ref)
    seg = seg_ref[...]
    perm = perm_ref[...]
    lo_half_pk = lax.broadcasted_iota(jnp.int32, (tq, LANES), 1) < HEAD_DIM
    lo_half_blk = lax.broadcasted_iota(jnp.int32, (BLOCK, LANES), 1) < HEAD_DIM

    def norm_rope(raw_bf, gain_cos, gain_sin):
        raw = raw_bf.astype(F32)
        hi, lo = _split_bf16(raw * raw)
        rs = lax.rsqrt((dot(hi, seg) + dot(lo, seg)) * (1.0 / HEAD_DIM) + QK_EPS)
        return (raw * gain_cos + dot(raw_bf, perm) * gain_sin) * rs

    q_cos = cos_t * rc_ref[ROPE_HALF + 1:ROPE_HALF + 2, :]
    q_sin = sin_t * rc_ref[ROPE_HALF + 2:ROPE_HALF + 3, :]
    k_cos = cos_t * rc_ref[ROPE_HALF + 3:ROPE_HALF + 4, :]
    k_sin = sin_t * rc_ref[ROPE_HALF + 4:ROPE_HALF + 5, :]

    for col in range(SWA_KV_WIDTH // LANES):
        cs = slice(col * LANES, (col + 1) * LANES)
        kc = norm_rope(k_ref[:, cs], k_cos, k_sin)
        versions = ((kbuf_ref, slice(None), kc.astype(BF16), pltpu.roll(kc, HEAD_DIM, 1).astype(BF16)),
                    (vbuf_ref, slice(0, LANES), v_ref[:, cs],
                     pltpu.roll(v_ref[:, cs].astype(F32), HEAD_DIM, 1).astype(BF16)))
        for buf, lanes, plain, rotated in versions:
            for ver, val in ((2 * col, plain), (2 * col + 1, rotated)):
                buf[mine + ver, BLOCK:BLOCK + tq, lanes] = val
                buf[other + ver, 0:BLOCK, lanes] = val[tq - BLOCK:tq]

    q_half = []
    for col in range(SWA_Q_WIDTH // LANES):
        cs = slice(col * LANES, (col + 1) * LANES)
        qc = norm_rope(q_ref[:, cs], q_cos, q_sin)
        q_half.append((jnp.where(lo_half_pk, qc, 0.0).astype(BF16), jnp.where(lo_half_pk, 0.0, qc).astype(BF16)))

    qi = lax.broadcasted_iota(jnp.int32, (2 * BLOCK, BLOCK), 0) & (BLOCK - 1)
    kj = lax.broadcasted_iota(jnp.int32, (2 * BLOCK, BLOCK), 1)
    own = kj <= qi
    no_prev = jnp.where(first_tile, 1, 0).astype(F32) * NEG_BIG

    for n in range(nblk):
        r0 = n * BLOCK
        for h in range(SWA_KV_HEADS):
            by_half = []
            for half in range(HEADS_PER_COL):
                ver = 2 * (h // HEADS_PER_COL) + (0 if half == h % HEADS_PER_COL else 1)
                qs = jnp.concatenate([q_half[2 * h][half][r0:r0 + BLOCK],
                                      q_half[2 * h + 1][half][r0:r0 + BLOCK]], axis=0)
                s2 = lax.dot_general(qs, kbuf_ref[mine + ver, r0:r0 + 2 * BLOCK, :], (((1,), (1,)), ((), ())),
                                     preferred_element_type=F32)
                s_prev = s2[:, :BLOCK] + no_prev if n == 0 else s2[:, :BLOCK]
                s = jnp.where(own, s2[:, BLOCK:], s_prev)
                sink = jnp.concatenate(
                    [jnp.full((BLOCK, BLOCK), sink_ref[h * SWA_GROUP + half], F32),
                     jnp.full((BLOCK, BLOCK), sink_ref[h * SWA_GROUP + half + HEADS_PER_COL], F32)], axis=0)
                m = jnp.maximum(jnp.broadcast_to(jnp.max(s, axis=-1, keepdims=True), s.shape), sink)
                p = jnp.exp(s - m)
                p2 = jnp.concatenate([jnp.where(own, 0.0, p).astype(BF16),
                                      jnp.where(own, p, 0.0).astype(BF16)], axis=1)
                o2 = dot(p2, vbuf_ref[mine + ver, r0:r0 + 2 * BLOCK, :])
                by_half.append(o2[:, :LANES] / (o2[:, LANES:] + jnp.exp(sink - m)))
            for sub in range(2):
                rs_ = slice(sub * BLOCK, (sub + 1) * BLOCK)
                colv = jnp.where(lo_half_blk, by_half[0][rs_], by_half[1][rs_])
                oswa_ref[r0:r0 + BLOCK, (2 * h + sub) * LANES:(2 * h + sub + 1) * LANES] = colv.astype(BF16)

    ones = jnp.ones((MEM_LEN, LANES), BF16)
    outs = []
    for hd in range(MEM_HEADS):
        sl = slice(hd * MEM_HEAD_DIM, (hd + 1) * MEM_HEAD_DIM)
        qh = qm_ref[:, sl].astype(F32)
        rs = lax.rsqrt(jnp.mean(qh * qh, axis=-1, keepdims=True) + QK_EPS)
        qh = qh * (rs * (MEM_HEAD_DIM ** -0.5)) * gmq_ref[...]
        s = lax.dot_general(qh.astype(BF16), mk_ref[:, sl], (((1,), (1,)), ((), ())),
                            preferred_element_type=F32)
        p = jnp.exp(s - jnp.max(s, axis=-1, keepdims=True)).astype(BF16)
        o2 = dot(p, jnp.concatenate([mv_ref[:, sl], ones], axis=1))
        outs.append(o2[:, :LANES] / o2[:, LANES:])
    omem_ref[...] = jnp.concatenate(outs, axis=1).astype(BF16)


def _rope_constants(q_gain, k_gain):
    inv_freq = ROPE_THETA ** (-jnp.arange(ROPE_HALF, dtype=F32) / ROPE_HALF)
    lane = jnp.arange(LANES) % HEAD_DIM
    j = jnp.arange(ROPE_HALF)[:, None]
    e_cos = ((lane[None, :] < ROPE_DIM) & (lane[None, :] % ROPE_HALF == j)).astype(F32)
    e_sin = (lane[None, :] == j + ROPE_HALF).astype(F32) - (lane[None, :] == j).astype(F32)
    ones = (lane >= ROPE_DIM).astype(F32)[None, :]
    src = jnp.where(lane < ROPE_HALF, jnp.arange(LANES) + ROPE_HALF, jnp.arange(LANES) - ROPE_HALF)
    rotary = lane < ROPE_DIM
    perm = ((jnp.arange(LANES)[:, None] == src[None, :]) & rotary[None, :]).astype(BF16)

    def gains(g, scale):
        col = jnp.tile(g.astype(F32).reshape(-1), HEADS_PER_COL) * scale
        return col[None, :], jnp.where(rotary, col[jnp.clip(src, 0, LANES - 1)], 0.0)[None, :]

    qg, qgp = gains(q_gain, HEAD_DIM ** -0.5)
    kg, kgp = gains(k_gain, 1.0)
    rows = jnp.concatenate([jnp.broadcast_to(inv_freq[:, None], (ROPE_HALF, LANES)), ones, qg, qgp, kg, kgp,
                            jnp.zeros((ROPE_CONST_ROWS - ROPE_HALF - 5, LANES), F32)], axis=0)
    zero8 = jnp.zeros((ROPE_HALF, LANES), F32)
    expand = jnp.concatenate([
        jnp.concatenate([e_cos, zero8], axis=1),
        jnp.concatenate([zero8, e_sin], axis=1),
        jnp.zeros((LANES - 2 * ROPE_HALF, 2 * LANES), F32)], axis=0)
    return rows, expand, perm


def _attention(sinks, q, k, v, pos_rows, qm, mk, mv, q_gain, k_gain, gmq, bsz, seq, tq):
    t = bsz * seq
    nj = seq // tq
    row = lambda b, j: (b * nj + j, 0)
    mem = lambda b, j: (b, 0)
    half = jnp.arange(LANES) // HEAD_DIM
    seg = (half[:, None] == half[None, :]).astype(BF16)
    kern = functools.partial(_attn_kernel, tq=tq)
    nver = 2 * (2 * SWA_KV_WIDTH // LANES)
    return pl.pallas_call(
        kern,
        grid=(bsz, nj),
        in_specs=[
            pl.BlockSpec(memory_space=pltpu.SMEM),
            pl.BlockSpec((tq, SWA_Q_WIDTH), row),
            pl.BlockSpec((tq, SWA_KV_WIDTH), row),
            pl.BlockSpec((tq, SWA_KV_WIDTH), row),
            pl.BlockSpec((SUBLANES, tq), row),
            pl.BlockSpec((tq, MEM_WIDTH), row),
            pl.BlockSpec((MEM_LEN, MEM_WIDTH), mem),
            pl.BlockSpec((MEM_LEN, MEM_WIDTH), mem),
            _const_spec((ROPE_CONST_ROWS, LANES)),
            _const_spec((LANES, 2 * LANES)),
            _const_spec((LANES, LANES)),
            _const_spec((LANES, LANES)),
            _const_spec((1, MEM_HEAD_DIM)),
        ],
        out_specs=[
            pl.BlockSpec((tq, SWA_Q_WIDTH), row),
            pl.BlockSpec((tq, MEM_WIDTH), row),
        ],
        out_shape=[
            jax.ShapeDtypeStruct((t, SWA_Q_WIDTH), BF16),
            jax.ShapeDtypeStruct((t, MEM_WIDTH), BF16),
        ],
        scratch_shapes=[
            pltpu.VMEM((nver, BLOCK + tq, LANES), BF16),
            pltpu.VMEM((nver, BLOCK + tq, 2 * LANES), BF16),
        ],
        compiler_params=_cparams(("parallel", "arbitrary")),
    )(sinks, q, k, v, pos_rows, qm, mk, mv, *_rope_constants(q_gain, k_gain), seg, gmq)


def _merge_kernel(x_ref, ys_ref, yw_ref, ym_ref, gate_ref, w0_ref, w1_ref, w2_ref, wo_ref, gf_ref,
                  wr_cat_ref, br_ref, x1_ref, tsl_ref, idx_ref, wt_ref, *, tm):
    gates = gate_ref[...]
    dot = lambda a, b: jnp.dot(a, b, preferred_element_type=F32)
    merged = (gates[:, 0:D_MODEL].astype(F32) * dot(ys_ref[...], w0_ref[...])
              + gates[:, D_MODEL:2 * D_MODEL].astype(F32) * dot(yw_ref[...], w1_ref[...])
              + gates[:, 2 * D_MODEL:].astype(F32) * dot(ym_ref[...], w2_ref[...]))
    x1 = x_ref[...] + dot(merged.astype(BF16), wo_ref[...])
    x1_ref[...] = x1
    t = x1 * lax.rsqrt(jnp.mean(x1 * x1, axis=-1, keepdims=True) + NORM_EPS) * gf_ref[...]

    for j in range(ROW_SUBLANES):
        tsl_ref[pl.ds(j, tm, stride=ROW_SUBLANES), :] = t[:, j * LANES:(j + 1) * LANES]

    t_hi, t_lo = _split_bf16(t)
    both = dot(t_hi, wr_cat_ref[...])
    logits = both[:, :LANES] + both[:, LANES:] + dot(t_lo, wr_cat_ref[:, :LANES]) + br_ref[...]
    lane = lax.broadcasted_iota(jnp.int32, logits.shape, 1).astype(F32)
    work = logits
    vals, firsts = [], []
    for _ in range(TOP_K):
        m = jnp.max(work, axis=-1, keepdims=True)
        first = jnp.min(jnp.where(work == m, lane, float(LANES)), axis=-1, keepdims=True)
        work = jnp.where(lane == first, NEG_BIG * 2.0, work)
        vals.append(m)
        firsts.append(first)
    exps = [jnp.exp(v - vals[0]) for v in vals]
    den = exps[0] + exps[1] + exps[2] + exps[3]
    idx_out = jnp.zeros_like(logits)
    wt_out = jnp.zeros_like(logits)
    for k in range(TOP_K):
        idx_out = jnp.where(lane == float(k), firsts[k], idx_out)
        wt_out = jnp.where(lane == float(k), exps[k] / den, wt_out)
    idx_ref[...] = idx_out.T[0:SUBLANES, :].astype(jnp.int32)
    wt_ref[...] = wt_out.T[0:SUBLANES, :]


def _merge(x2, y_ssm_tm, y_swa, y_mem, gates, w0, w1, w2, wo, gf, wr_cat, br, bsz, seq, tm):
    t = bsz * seq
    nj = seq // tm
    row = lambda b, j: (b * nj + j, 0)
    return pl.pallas_call(
        functools.partial(_merge_kernel, tm=tm),
        grid=(bsz, nj),
        in_specs=[
            pl.BlockSpec((tm, D_MODEL), row),
            pl.BlockSpec((tm, SSM_WIDTH), lambda b, j: (j, b)),
            pl.BlockSpec((tm, SWA_Q_WIDTH), row),
            pl.BlockSpec((tm, MEM_WIDTH), row),
            pl.BlockSpec((tm, GATE_WIDTH), row),
            _const_spec((SSM_WIDTH, D_MODEL)),
            _const_spec((SWA_Q_WIDTH, D_MODEL)),
            _const_spec((MEM_WIDTH, D_MODEL)),
            _const_spec((D_MODEL, D_MODEL)),
            _const_spec((1, D_MODEL)),
            _const_spec((D_MODEL, 2 * LANES)),
            _const_spec((1, LANES)),
        ],
        out_specs=[
            pl.BlockSpec((tm, D_MODEL), row),
            pl.BlockSpec((tm * ROW_SUBLANES, LANES), row),
            pl.BlockSpec((SUBLANES, tm), row),
            pl.BlockSpec((SUBLANES, tm), row),
        ],
        out_shape=[
            jax.ShapeDtypeStruct((t, D_MODEL), F32),
            jax.ShapeDtypeStruct((t * ROW_SUBLANES, LANES), F32),
            jax.ShapeDtypeStruct((t // tm * SUBLANES, tm), jnp.int32),
            jax.ShapeDtypeStruct((t // tm * SUBLANES, tm), F32),
        ],
        compiler_params=_cparams(("parallel", "parallel")),
    )(x2, y_ssm_tm, y_swa, y_mem, gates, w0, w1, w2, wo, gf, wr_cat, br)


DEINT_BLOCK = 2 * LANES


def _expert_prep_kernel(w1_ref, perm_ref, w1d_ref):
    perm = perm_ref[...]
    for blk in range(2 * D_EXPERT // DEINT_BLOCK):
        cols = w1_ref[0, :, blk * DEINT_BLOCK:(blk + 1) * DEINT_BLOCK].astype(BF16)
        z = jnp.dot(cols, perm, preferred_element_type=F32).astype(BF16)
        w1d_ref[0, :, blk * LANES:(blk + 1) * LANES] = z[:, :LANES]
        w1d_ref[0, :, D_EXPERT + blk * LANES:D_EXPERT + (blk + 1) * LANES] = z[:, LANES:]


def _expert_prep(w1):
    src = jnp.arange(DEINT_BLOCK)
    dst = jnp.where(src % 2 == 0, src // 2, LANES + src // 2)
    perm = (dst[:, None] == jnp.arange(DEINT_BLOCK)[None, :]).astype(BF16)
    exp3 = lambda e: (e, 0, 0)
    return pl.pallas_call(
        _expert_prep_kernel,
        grid=(N_EXPERTS,),
        in_specs=[pl.BlockSpec((1, D_MODEL, 2 * D_EXPERT), exp3), _const_spec((DEINT_BLOCK, DEINT_BLOCK))],
        out_specs=pl.BlockSpec((1, D_MODEL, 2 * D_EXPERT), exp3),
        out_shape=jax.ShapeDtypeStruct((N_EXPERTS, D_MODEL, 2 * D_EXPERT), BF16),
        compiler_params=_cparams(("parallel",)),
    )(w1, perm)


SCATTER_BATCH = 8


def _moe_kernel(cnt_ref, off_ref, tok_ref, wt_ref, src_ref, w1_ref, w2_ref, bias_ref,
                out_hbm, acc_ref, gbuf_ref, ybuf_ref, state_ref, sem, *, nb_tokens, tile):
    blk = pl.program_id(0)
    e = pl.program_id(1)
    acc_rows = nb_tokens * ROW_SUBLANES
    buf_rows = tile * ROW_SUBLANES

    def gather(base, half):
        row0 = half * buf_rows
        for i in range(tile):
            tok = tok_ref[0, 0, base + i]
            gbuf_ref[pl.ds(pl.multiple_of(row0 + i * ROW_SUBLANES, ROW_SUBLANES), ROW_SUBLANES), :] = (
                src_ref[pl.ds(pl.multiple_of(tok * ROW_SUBLANES, ROW_SUBLANES), ROW_SUBLANES), :])

    def scatter(base, nvalid, half):
        row0 = half * buf_rows
        for i0 in range(0, tile, SCATTER_BATCH):
            sums, dsts = [], []
            for i in range(i0, i0 + SCATTER_BATCH):
                live = i < nvalid
                tok = jnp.where(live, tok_ref[0, 0, base + i], nb_tokens)
                w = jnp.where(live, wt_ref[0, 0, base + i], 0.0)
                dst = pl.ds(pl.multiple_of(tok * ROW_SUBLANES, ROW_SUBLANES), ROW_SUBLANES)
                src = pl.ds(pl.multiple_of(row0 + i * ROW_SUBLANES, ROW_SUBLANES), ROW_SUBLANES)
                sums.append(acc_ref[dst, :] + w * ybuf_ref[src, :])
                dsts.append(dst)
            for dst, s in zip(dsts, sums):
                acc_ref[dst, :] = s

    def writeback(block):
        return pltpu.make_async_copy(acc_ref.at[pl.ds(0, acc_rows), :],
                                     out_hbm.at[pl.ds(block * acc_rows, acc_rows), :], sem)

    @pl.when(e == 0)
    def _():
        @pl.when(blk > 0)
        def _():
            writeback(blk - 1).wait()

        acc_ref[...] = jnp.zeros_like(acc_ref)
        ybuf_ref[...] = jnp.zeros_like(ybuf_ref)
        state_ref[0] = 0
        state_ref[1] = 0
        state_ref[2] = 0
        gather(0, 0)

    n = cnt_ref[blk * N_EXPERTS + e]
    start = off_ref[blk * N_EXPERTS + e]
    ntiles = (n + tile - 1) // tile

    def stages(rows, next_base, prev_base, prev_n, half):
        other = 1 - half
        gather(next_base, other)
        row0 = half * buf_rows
        x = jnp.concatenate(
            [gbuf_ref[pl.ds(row0 + j, rows, stride=ROW_SUBLANES), :] for j in range(ROW_SUBLANES)],
            axis=1).astype(BF16)
        hg = jnp.dot(x, w1_ref[0, :, :D_EXPERT], preferred_element_type=F32) + bias_ref[0, :, :D_EXPERT]
        hl = (jnp.dot(x, w1_ref[0, :, D_EXPERT:], preferred_element_type=F32)
              + bias_ref[0, :, D_EXPERT:2 * D_EXPERT])
        xg = jnp.minimum(hg, SWIGLU_LIMIT)
        xl = jnp.clip(hl, -SWIGLU_LIMIT, SWIGLU_LIMIT)
        act = xg * _sigmoid(SWIGLU_ALPHA * xg) * (xl + 1.0)
        y = jnp.dot(act.astype(BF16), w2_ref[0], preferred_element_type=F32) + bias_ref[0, :, 2 * D_EXPERT:]
        scatter(prev_base, prev_n, other)
        for j in range(ROW_SUBLANES):
            ybuf_ref[pl.ds(row0 + j, rows, stride=ROW_SUBLANES), :] = y[:, j * LANES:(j + 1) * LANES]

    def tile_body(ti, carry):
        prev_base, prev_n, half = carry
        base = start + ti * tile
        live = n - ti * tile
        next_base = jnp.where(ti + 1 < ntiles, base + tile, start + n)
        lax.cond(live <= MOE_SMALL_TILE,
                 functools.partial(stages, MOE_SMALL_TILE, next_base, prev_base, prev_n, half),
                 functools.partial(stages, tile, next_base, prev_base, prev_n, half))
        return base, live, 1 - half

    prev_base, prev_n, half = lax.fori_loop(0, ntiles, tile_body,
                                            (state_ref[0], state_ref[1], state_ref[2]))
    state_ref[0] = prev_base
    state_ref[1] = prev_n
    state_ref[2] = half

    @pl.when(e == N_EXPERTS - 1)
    def _():
        scatter(prev_base, prev_n, 1 - half)
        writeback(blk).start()

        @pl.when(blk == pl.num_programs(0) - 1)
        def _():
            writeback(blk).wait()


def _moe(counts, offsets, tok_sorted, wt_sorted, tsl, w1d, w2, bias, t, nb_tokens, tile):
    nblk = t // nb_tokens
    slots = tok_sorted.shape[-1]
    exp3 = lambda b, e, *_: (e, 0, 0)
    blk3 = lambda b, e, *_: (b, 0, 0)
    grid_spec = pltpu.PrefetchScalarGridSpec(
        num_scalar_prefetch=2,
        grid=(nblk, N_EXPERTS),
        in_specs=[
            pl.BlockSpec((1, 1, slots), blk3, memory_space=pltpu.SMEM),
            pl.BlockSpec((1, 1, slots), blk3, memory_space=pltpu.SMEM),
            pl.BlockSpec((nb_tokens * ROW_SUBLANES, LANES), lambda b, e, *_: (b, 0),
                         pipeline_mode=pl.Buffered(1)),
            pl.BlockSpec((1, D_MODEL, 2 * D_EXPERT), exp3),
            pl.BlockSpec((1, D_EXPERT, D_MODEL), exp3),
            pl.BlockSpec((1, 1, 2 * D_EXPERT + D_MODEL), exp3),
        ],
        out_specs=pl.BlockSpec(memory_space=pl.ANY),
        scratch_shapes=[
            pltpu.VMEM(((nb_tokens + 1) * ROW_SUBLANES, LANES), F32),
            pltpu.VMEM((2 * tile * ROW_SUBLANES, LANES), F32),
            pltpu.VMEM((2 * tile * ROW_SUBLANES, LANES), F32),
            pltpu.SMEM((3,), jnp.int32),
            pltpu.SemaphoreType.DMA,
        ],
    )
    return pl.pallas_call(
        functools.partial(_moe_kernel, nb_tokens=nb_tokens, tile=tile),
        grid_spec=grid_spec,
        out_shape=jax.ShapeDtypeStruct((t * ROW_SUBLANES, LANES), F32),
        compiler_params=_cparams(("arbitrary", "arbitrary")),
    )(counts, offsets, tok_sorted, wt_sorted, tsl, w1d, w2, bias)


def _route(idx_t, wts_t, t, nb_tokens, tm, tile):
    nblk = t // nb_tokens
    pick = lambda a: a.reshape(t // tm, SUBLANES, tm)[:, :TOP_K, :].reshape(nblk, nb_tokens * TOP_K)
    eid, w = pick(idx_t), pick(wts_t)
    span = nb_tokens * TOP_K
    key = eid * span + lax.broadcasted_iota(jnp.int32, (1, span), 1)
    key_sorted, wt_sorted = lax.sort((key, w), dimension=1, is_stable=False, num_keys=1)
    flat = key_sorted % span
    tok_sorted = (flat // (TOP_K * tm)) * tm + flat % tm
    counts = jnp.sum((eid[:, :, None] == jnp.arange(N_EXPERTS)[None, None, :]).astype(jnp.int32), axis=1)
    offsets = jnp.cumsum(counts, axis=1) - counts
    pad = ((0, 0), (0, tile))
    tok_sorted = jnp.pad(tok_sorted, pad)[:, None, :]
    wt_sorted = jnp.pad(wt_sorted, pad)[:, None, :]
    return counts.reshape(-1), offsets.reshape(-1).astype(jnp.int32), tok_sorted, wt_sorted


def _residual_kernel(x1_ref, moe_ref, o_ref, *, tm):
    parts = [moe_ref[pl.ds(j, tm, stride=ROW_SUBLANES), :] for j in range(ROW_SUBLANES)]
    o_ref[...] = x1_ref[...] + jnp.concatenate(parts, axis=1)


def _residual(x1, moe, tm):
    t = x1.shape[0]
    return pl.pallas_call(
        functools.partial(_residual_kernel, tm=tm),
        grid=(t // tm,),
        in_specs=[pl.BlockSpec((tm, D_MODEL), lambda i: (i, 0)),
                  pl.BlockSpec((tm * ROW_SUBLANES, LANES), lambda i: (i, 0))],
        out_specs=pl.BlockSpec((tm, D_MODEL), lambda i: (i, 0)),
        out_shape=jax.ShapeDtypeStruct((t, D_MODEL), F32),
        compiler_params=_cparams(("parallel",)),
    )(x1, moe)


def kernel(x, mem, positions, attn_norm_g, mem_norm_g, w_in, b_gate, ssm_log_dt, ssm_a_re, ssm_a_im,
           ssm_b_re, ssm_b_im, ssm_c_re, ssm_c_im, ssm_d, w_glu_v, w_glu_g, swa_q_gain, swa_k_gain,
           swa_sinks, w_mem_kv, mem_q_gain, mem_k_gain, w_br_ssm, w_br_swa, w_br_mem, w_out,
           ffn_norm_g, w_router, b_router, w_mlp1, b_mlp1, w_mlp2, b_mlp2):
    bsz, seq, _ = x.shape
    depth = w_in.shape[0]
    t = bsz * seq
    tm = min(ROW_TILE, seq)
    tq = min(ATTN_TILE, seq)
    steps = min(S5_STEPS, seq)

    pos_rows = jnp.broadcast_to(positions.astype(F32).reshape(t // tq, 1, tq),
                                (t // tq, SUBLANES, tq)).reshape(t // tq * SUBLANES, tq)
    mem2 = mem.reshape(bsz * MEM_LEN, D_MODEL)

    x2 = x.reshape(t, D_MODEL)
    for l in range(depth):
        row = lambda a: a[l].reshape(1, -1).astype(F32)
        u_tm, q, k, v, qm, gates = _in_projection(
            x2, row(attn_norm_g), w_in[l].astype(BF16), row(b_gate), bsz, seq, tm)

        bmat, abr, abi, cmat = _s5_discretize(ssm_log_dt[l], ssm_a_re[l], ssm_a_im[l], ssm_b_re[l],
                                              ssm_b_im[l], ssm_c_re[l], ssm_c_im[l])
        y_ssm_tm = _s5_branch(u_tm, bmat, abr, abi, cmat, row(ssm_d),
                              w_glu_v[l].astype(BF16), w_glu_g[l].astype(BF16), bsz, seq, steps)

        mk, mv = _mem_kv(mem2, row(mem_norm_g), w_mem_kv[l].astype(BF16), row(mem_k_gain), bsz)
        y_swa, y_mem = _attention(
            swa_sinks[l].astype(F32), q, k, v, pos_rows, qm, mk, mv, swa_q_gain[l], swa_k_gain[l],
            row(mem_q_gain), bsz, seq, tq)

        wr = jnp.pad(w_router[l].astype(F32), ((0, 0), (0, LANES - N_EXPERTS)))
        wr_hi = wr.astype(BF16)
        wr_lo = (wr - wr_hi.astype(F32)).astype(BF16)
        br = jnp.pad(b_router[l].astype(F32), (0, LANES - N_EXPERTS), constant_values=NEG_BIG).reshape(1, LANES)
        x1, tsl, idx, wts = _merge(x2, y_ssm_tm, y_swa, y_mem, gates, w_br_ssm[l].astype(BF16),
                                   w_br_swa[l].astype(BF16), w_br_mem[l].astype(BF16),
                                   w_out[l].astype(BF16), row(ffn_norm_g), jnp.concatenate([wr_hi, wr_lo], axis=1), br, bsz, seq, tm)

        nb_tokens = min(MOE_TOKEN_BLOCK, t)
        idx, w1_l, w2_l = lax.optimization_barrier((idx, w_mlp1[l], w_mlp2[l]))
        counts, offsets, tok_sorted, wt_sorted = _route(idx, wts, t, nb_tokens, tm, MOE_TILE)
        b1 = b_mlp1[l].astype(F32)
        bias = jnp.concatenate([b1[:, 0::2], b1[:, 1::2], b_mlp2[l].astype(F32)], axis=1)[:, None, :]
        moe = _moe(counts, offsets, tok_sorted, wt_sorted, tsl, _expert_prep(w1_l), w2_l.astype(BF16), bias,
                   t, nb_tokens, MOE_TILE)
        x2 = _residual(x1, moe, min(RESIDUAL_TILE, t))
    return x2.reshape(bsz, seq, D_MODEL)
```
